```python
import math
import jax, jax.numpy as jnp
from jax import lax
import numpy as np


D_MODEL = 1024
BATCH = 2
SEQ = 8192
DEPTH = 1

CHUNK = 128
A_GROUPS = 8
A_GROUP_DIM = 128
A_WIDTH = A_GROUPS * A_GROUP_DIM
N_HEADS = 8
QK_NOPE = 128
QK_ROPE = 64
QK_DIM = QK_NOPE + QK_ROPE
V_HEAD = 128
Q_LORA = 384
KV_LORA = 256
ROPE_THETA = 10000.0
Q_BLOCK = 128
N_EXPERTS = 256
TOP_K = 8
N_GROUPS = 8
TOPK_GROUPS = 4
D_EXPERT = 256
ROUTED_SCALE = 2.5
MOE_BLOCK = 128
DN_ALPHA = (2 * DEPTH) ** 0.25
DN_BETA = (8 * DEPTH) ** -0.25
LN_EPS = 1e-5
RMS_EPS = 1e-6

IN_WIDTHS = (A_WIDTH, A_WIDTH, Q_LORA, KV_LORA, QK_ROPE, D_MODEL, D_MODEL)
IN_TOTAL = sum(IN_WIDTHS)
SPLIT_POINTS = tuple(int(s) for s in np.cumsum(IN_WIDTHS)[:-1])

kernel_name = 'hybrid_gmlp_mla_moe_deepnorm'


def layer_norm(x, g, b):
    xf = x.astype(jnp.float32)
    mu = xf.mean(-1, keepdims=True)
    var = jnp.square(xf - mu).mean(-1, keepdims=True)
    return ((xf - mu) * lax.rsqrt(var + LN_EPS) * g.astype(jnp.float32) + b.astype(jnp.float32)).astype(x.dtype)


def rms_norm(x, g):
    xf = x.astype(jnp.float32)
    return (xf * lax.rsqrt(jnp.mean(xf * xf, -1, keepdims=True) + RMS_EPS) * g.astype(jnp.float32)).astype(x.dtype)


def rope_tables(positions):
    freq = ROPE_THETA ** (-jnp.arange(0, QK_ROPE // 2, dtype=jnp.float32) * 2.0 / QK_ROPE)
    ang = positions.astype(jnp.float32)[..., None] * freq
    return jnp.cos(ang), jnp.sin(ang)


def apply_rope(x, cos, sin):
    x1, x2 = jnp.split(x.astype(jnp.float32), 2, axis=-1)
    return jnp.concatenate([x1 * cos - x2 * sin, x1 * sin + x2 * cos], axis=-1).astype(x.dtype)


def spatial_gating(u, v, ln_g, ln_b, w_s, b_s):
    B, S, _ = u.shape
    n_chunks = S // CHUNK
    u = jax.nn.gelu(u)
    v = jax.nn.gelu(v).reshape(B, n_chunks, CHUNK, A_GROUPS, A_GROUP_DIM)
    v = layer_norm(v, ln_g.reshape(A_GROUPS, A_GROUP_DIM), ln_b.reshape(A_GROUPS, A_GROUP_DIM))
    mixed = jnp.einsum('hpq,bcqhd->bcphd', w_s, v) + b_s.T[:, :, None]
    return u * mixed.reshape(B, S, A_WIDTH)


def latent_attention(c_q, c_kv, k_r, cos, sin, q_norm_g, w_uq, kv_norm_g, w_uk, w_uv):
    B, S, _ = c_q.shape
    q = jnp.einsum('bsr,rhd->bshd', rms_norm(c_q, q_norm_g), w_uq)
    q_rope = apply_rope(q[..., QK_NOPE:], cos[:, :, None], sin[:, :, None])
    q = jnp.concatenate([q[..., :QK_NOPE], q_rope], axis=-1) * (1.0 / math.sqrt(QK_DIM))
    c_kv = rms_norm(c_kv, kv_norm_g)
    k_nope = jnp.einsum('bsr,rhd->bshd', c_kv, w_uk)
    v = jnp.einsum('bsr,rhd->bshd', c_kv, w_uv)
    k_rope = apply_rope(k_r, cos, sin)
    k = jnp.concatenate([k_nope, jnp.broadcast_to(k_rope[:, :, None, :], (B, S, N_HEADS, QK_ROPE))], axis=-1)
    q_blocks = q.reshape(B, S // Q_BLOCK, Q_BLOCK, N_HEADS, QK_DIM).transpose(1, 0, 2, 3, 4)

    def attend(q_blk):
        s = jnp.einsum('bqhd,bkhd->bhqk', q_blk, k).astype(jnp.float32)
        p = jax.nn.softmax(s, axis=-1).astype(v.dtype)
        return jnp.einsum('bhqk,bkhd->bqhd', p, v)

    o = lax.map(attend, q_blocks)
    return o.transpose(1, 0, 2, 3, 4).reshape(B, S, N_HEADS * V_HEAD)


def route(x_flat, w_router, router_bias):
    T = x_flat.shape[0]
    scores = jax.nn.sigmoid((x_flat @ w_router).astype(jnp.float32))
    biased = scores + router_bias.astype(jnp.float32)
    per_group = N_EXPERTS // N_GROUPS
    group_score = lax.top_k(biased.reshape(T, N_GROUPS, per_group), 2)[0].sum(-1)
    _, top_groups = lax.top_k(group_score, TOPK_GROUPS)
    group_mask = jnp.any(top_groups[..., None] == jnp.arange(N_GROUPS), axis=-2)
    masked = jnp.where(jnp.repeat(group_mask, per_group, axis=-1), biased, -jnp.inf)
    _, idx = lax.top_k(masked, TOP_K)
    w = jnp.take_along_axis(scores, idx, axis=-1)
    w = w / jnp.sum(w, -1, keepdims=True) * ROUTED_SCALE
    return idx, w.astype(x_flat.dtype)


def routed_experts(x_flat, idx, gate_w, w_gate, w_up, w_down):
    T, D = x_flat.shape
    N = T * TOP_K
    e_flat = idx.reshape(N)
    tok_flat = jnp.arange(N, dtype=jnp.int32) // TOP_K
    w_flat = gate_w.reshape(N)
    order = jnp.argsort(e_flat)
    e_sorted = e_flat[order]
    counts = jnp.bincount(e_flat, length=N_EXPERTS)
    starts = jnp.cumsum(counts) - counts
    padded = (counts + MOE_BLOCK - 1) // MOE_BLOCK * MOE_BLOCK
    pad_end = jnp.cumsum(padded)
    pad_start = pad_end - padded
    dest = pad_start[e_sorted] + (jnp.arange(N, dtype=jnp.int32) - starts[e_sorted])
    P = ((N + MOE_BLOCK - 1) // MOE_BLOCK + N_EXPERTS) * MOE_BLOCK
    NB = P // MOE_BLOCK
    buf_tok = jnp.zeros((P,), jnp.int32).at[dest].set(tok_flat[order])
    buf_w = jnp.zeros((P,), x_flat.dtype).at[dest].set(w_flat[order])
    block_e = jnp.minimum(jnp.searchsorted(pad_end, jnp.arange(NB) * MOE_BLOCK, side='right'), N_EXPERTS - 1)

    def expert_block(args):
        e, tok, gw = args
        xb = x_flat[tok]
        h = jax.nn.silu(xb @ w_gate[e]) * (xb @ w_up[e])
        return ((h @ w_down[e]) * gw[:, None]).astype(x_flat.dtype)

    ys = lax.map(expert_block, (block_e, buf_tok.reshape(NB, MOE_BLOCK), buf_w.reshape(NB, MOE_BLOCK)))
    return jnp.zeros_like(x_flat).at[buf_tok].add(ys.reshape(P, D))


def swiglu(x, w_g, w_u, w_d):
    return (jax.nn.silu(x @ w_g) * (x @ w_u)) @ w_d


def setup_inputs(seed: int = 0) -> dict:
    key = jax.random.key(seed)
    ks = jax.random.split(key, 32)
    f32 = jnp.float32
    L = DEPTH

    def nrm(k, shape, scale):
        return jax.random.normal(k, shape, f32) * scale

    x = jax.random.normal(ks[0], (BATCH, SEQ, D_MODEL), f32)
    positions = (jnp.arange(SEQ, dtype=jnp.int32)[None, :] + jax.random.randint(ks[1], (BATCH, 1), 0, SEQ, dtype=jnp.int32)).astype(jnp.int32)
    return {
        'x': x,
        'positions': positions,
        'w_in': nrm(ks[2], (L, D_MODEL, IN_TOTAL), D_MODEL ** -0.5),
        'gmlp_ln_g': 1.0 + nrm(ks[3], (L, A_WIDTH), 0.01),
        'gmlp_ln_b': nrm(ks[4], (L, A_WIDTH), 0.01),
        'w_spatial': nrm(ks[5], (L, A_GROUPS, CHUNK, CHUNK), 0.5 * CHUNK ** -0.5),
        'b_spatial': 1.0 + nrm(ks[6], (L, A_GROUPS, CHUNK), 0.01),
        'w_a_proj': nrm(ks[7], (L, A_WIDTH, D_MODEL), DN_BETA * A_WIDTH ** -0.5),
        'q_norm_g': 1.0 + nrm(ks[8], (L, Q_LORA), 0.01),
        'w_uq': nrm(ks[9], (L, Q_LORA, N_HEADS, QK_DIM), Q_LORA ** -0.5),
        'kv_norm_g': 1.0 + nrm(ks[10], (L, KV_LORA), 0.01),
        'w_uk': nrm(ks[11], (L, KV_LORA, N_HEADS, QK_NOPE), KV_LORA ** -0.5),
        'w_uv': nrm(ks[12], (L, KV_LORA, N_HEADS, V_HEAD), KV_LORA ** -0.5),
        'w_b_proj': nrm(ks[13], (L, N_HEADS * V_HEAD, D_MODEL), DN_BETA * (N_HEADS * V_HEAD) ** -0.5),
        'w_out': nrm(ks[14], (L, D_MODEL, D_MODEL), DN_BETA * D_MODEL ** -0.5),
        'ln1_g': 1.0 + nrm(ks[15], (L, D_MODEL), 0.01),
        'ln1_b': nrm(ks[16], (L, D_MODEL), 0.01),
        'w_router': nrm(ks[17], (L, D_MODEL, N_EXPERTS), D_MODEL ** -0.5),
        'router_bias': nrm(ks[18], (L, N_EXPERTS), 0.01),
        'w_gate': nrm(ks[19], (L, N_EXPERTS, D_MODEL, D_EXPERT), D_MODEL ** -0.5),
        'w_up': nrm(ks[20], (L, N_EXPERTS, D_MODEL, D_EXPERT), D_MODEL ** -0.5),
        'w_down': nrm(ks[21], (L, N_EXPERTS, D_EXPERT, D_MODEL), DN_BETA * D_EXPERT ** -0.5),
        'w_sh_gate': nrm(ks[22], (L, D_MODEL, D_EXPERT), D_MODEL ** -0.5),
        'w_sh_up': nrm(ks[23], (L, D_MODEL, D_EXPERT), D_MODEL ** -0.5),
        'w_sh_down': nrm(ks[24], (L, D_EXPERT, D_MODEL), DN_BETA * D_EXPERT ** -0.5),
        'ln2_g': 1.0 + nrm(ks[25], (L, D_MODEL), 0.01),
        'ln2_b': nrm(ks[26], (L, D_MODEL), 0.01),
    }


def reference(x, positions, w_in, gmlp_ln_g, gmlp_ln_b, w_spatial, b_spatial, w_a_proj, q_norm_g, w_uq, kv_norm_g, w_uk, w_uv, w_b_proj, w_out, ln1_g, ln1_b, w_router, router_bias, w_gate, w_up, w_down, w_sh_gate, w_sh_up, w_sh_down, ln2_g, ln2_b):
    B, S, D = x.shape
    cos, sin = rope_tables(positions)
    h = x
    for l in range(DEPTH):
        proj = h @ w_in[l]
        u, v, c_q, c_kv, k_r, g_a, g_b = jnp.split(proj, SPLIT_POINTS, axis=-1)
        y_a = spatial_gating(u, v, gmlp_ln_g[l], gmlp_ln_b[l], w_spatial[l], b_spatial[l]) @ w_a_proj[l]
        y_b = latent_attention(c_q, c_kv, k_r, cos, sin, q_norm_g[l], w_uq[l], kv_norm_g[l], w_uk[l], w_uv[l]) @ w_b_proj[l]
        mixed = (jax.nn.sigmoid(g_a) * y_a + jax.nn.sigmoid(g_b) * y_b) @ w_out[l]
        h = layer_norm(DN_ALPHA * h + mixed, ln1_g[l], ln1_b[l])
        hf = h.reshape(B * S, D)
        idx, gw = route(hf, w_router[l], router_bias[l])
        moe = routed_experts(hf, idx, gw, w_gate[l], w_up[l], w_down[l]) + swiglu(hf, w_sh_gate[l], w_sh_up[l], w_sh_down[l])
        h = layer_norm(DN_ALPHA * h + moe.reshape(B, S, D), ln2_g[l], ln2_b[l])
    return h
```

```python
import functools
import math

import jax
import jax.numpy as jnp
from jax import lax
from jax.experimental import pallas as pl
from jax.experimental.pallas import tpu as pltpu

D_MODEL = 1024
CHUNK = 128
A_GROUPS = 8
A_GROUP_DIM = 128
A_WIDTH = A_GROUPS * A_GROUP_DIM
N_HEADS = 8
QK_NOPE = 128
QK_ROPE = 64
QK_DIM = QK_NOPE + QK_ROPE
V_HEAD = 128
Q_LORA = 384
KV_LORA = 256
ROPE_THETA = 10000.0
N_EXPERTS = 256
TOP_K = 8
N_GROUPS = 8
TOPK_GROUPS = 4
GROUP_SIZE = N_EXPERTS // N_GROUPS
D_EXPERT = 256
ROUTED_SCALE = 2.5
DN_ALPHA = 2.0 ** 0.25
LN_EPS = 1e-5
RMS_EPS = 1e-6

QK_PAD = 256
LANES = 128
TOKEN_TILE = 256
Q_TILE = 512
KV_TILE = 512
ROW_BLOCK = 256
VMEM_LIMIT = 56 * 1024 * 1024

F32 = jnp.float32
BF16 = jnp.bfloat16
NEG_INF = float("-inf")


def _dot(a, b):
    return jnp.dot(a, b, preferred_element_type=F32)


def _sigmoid(x):
    return 1.0 / (1.0 + jnp.exp(-x))


def _gelu_tanh(x):
    return 0.5 * x * (1.0 + jnp.tanh(math.sqrt(2.0 / math.pi) * (x + 0.044715 * (x * x * x))))


def _layer_norm(x, g, b):
    mu = jnp.mean(x, axis=-1, keepdims=True)
    d = x - mu
    var = jnp.mean(d * d, axis=-1, keepdims=True)
    return d * lax.rsqrt(var + LN_EPS) * g + b


def _rms_norm(x, g):
    return x * lax.rsqrt(jnp.mean(x * x, axis=-1, keepdims=True) + RMS_EPS) * g


def _pack_bf16_pair(lo, hi):
    lo_bits = pltpu.bitcast(lo.astype(BF16).astype(F32), jnp.uint32) >> 16
    hi_bits = pltpu.bitcast(hi.astype(BF16).astype(F32), jnp.uint32) & jnp.uint32(0xFFFF0000)
    return lo_bits | hi_bits


def _unpack_bf16_pair(packed):
    lo = pltpu.bitcast(packed << 16, F32)
    hi = pltpu.bitcast(packed & jnp.uint32(0xFFFF0000), F32)
    return lo, hi


def _const_spec(shape):
    zeros = (0,) * len(shape)
    return pl.BlockSpec(shape, lambda *_: zeros)


def _params(semantics):
    return pltpu.CompilerParams(dimension_semantics=semantics, vmem_limit_bytes=VMEM_LIMIT)


def _gmlp_kernel(x_ref, wu_ref, wv_ref, wg_ref, lng_ref, lnb_ref, ws_ref, bias_ref, wa_ref, o_ref, mix_ref):
    tm = x_ref.shape[0]
    xb = x_ref[...].astype(BF16)
    v = _gelu_tanh(_dot(xb, wv_ref[...]))
    for h in range(A_GROUPS):
        cols = slice(h * A_GROUP_DIM, (h + 1) * A_GROUP_DIM)
        vn = _layer_norm(v[:, cols], lng_ref[:, cols], lnb_ref[:, cols]).astype(BF16)
        for c in range(tm // CHUNK):
            rows = slice(c * CHUNK, (c + 1) * CHUNK)
            mix_ref[rows, cols] = _dot(ws_ref[h], vn[rows, :]) + bias_ref[:, cols]
    u = _gelu_tanh(_dot(xb, wu_ref[...]))
    ya = _dot((u * mix_ref[...]).astype(BF16), wa_ref[...])
    gate = _sigmoid(_dot(xb, wg_ref[...]))
    o_ref[...] = (gate * ya).astype(BF16)


def _gmlp_branch(x2, w_u, w_v, w_ga, ln_g, ln_b, w_s, bias_full, w_a):
    T = x2.shape[0]
    tm = TOKEN_TILE
    return pl.pallas_call(
        _gmlp_kernel,
        grid=(T // tm,),
        in_specs=[
            pl.BlockSpec((tm, D_MODEL), lambda i: (i, 0)),
            _const_spec((D_MODEL, A_WIDTH)),
            _const_spec((D_MODEL, A_WIDTH)),
            _const_spec((D_MODEL, D_MODEL)),
            _const_spec((1, A_WIDTH)),
            _const_spec((1, A_WIDTH)),
            _const_spec((A_GROUPS, CHUNK, CHUNK)),
            _const_spec((CHUNK, A_WIDTH)),
            _const_spec((A_WIDTH, D_MODEL)),
        ],
        out_specs=pl.BlockSpec((tm, D_MODEL), lambda i: (i, 0)),
        out_shape=jax.ShapeDtypeStruct((T, D_MODEL), BF16),
        scratch_shapes=[pltpu.VMEM((tm, A_WIDTH), F32)],
        compiler_params=_params(("arbitrary",)),
        name="gmlp_branch",
    )(x2, w_u, w_v, w_ga, ln_g, ln_b, w_s, bias_full, w_a)


def _mla_prep_kernel(x_ref, posr_ref, posc_ref, wb_ref, qg_ref, kvg_ref, wuqt_ref, wuk_ref, wuvt_ref,
                     freqc_ref, freqr_ref, signc_ref, signr_ref,
                     qt_ref, k_ref, vt_ref, sgb_ref):
    xb = x_ref[...].astype(BF16)
    pb = _dot(xb, wb_ref[...])
    c_q = pb[:, :Q_LORA]
    c_kv = pb[:, Q_LORA:Q_LORA + KV_LORA]
    o = Q_LORA + KV_LORA
    kr = pb[:, o:o + LANES]
    kr_partner = pb[:, o + LANES:o + 2 * LANES]
    gb = pb[:, o + 2 * LANES:]
    sgb_ref[...] = _sigmoid(gb).astype(BF16)

    cqn = _rms_norm(c_q, qg_ref[...])
    ckvn = _rms_norm(c_kv, kvg_ref[...])
    cqn_t = cqn.T.astype(BF16)
    ckvn_t = ckvn.T.astype(BF16)

    scale = 1.0 / math.sqrt(QK_DIM)
    ang_t = freqc_ref[...] * posr_ref[0]
    table_t = jnp.concatenate(
        [jnp.cos(ang_t[:QK_ROPE]), signc_ref[QK_ROPE:] * jnp.sin(ang_t[QK_ROPE:])], axis=0) * scale
    q_t = _dot(wuqt_ref[...], cqn_t)
    for h in range(N_HEADS):
        blk = q_t[h * QK_PAD:(h + 1) * QK_PAD]
        qt_ref[0, h] = jnp.concatenate(
            [blk[:QK_NOPE] * scale, blk[QK_NOPE:] * table_t], axis=0).astype(BF16)

    v_t = _dot(wuvt_ref[...], ckvn_t)
    for h in range(N_HEADS):
        vt_ref[0, h] = v_t[h * V_HEAD:(h + 1) * V_HEAD].astype(BF16)

    ang = posc_ref[...] * freqr_ref[...]
    k_rope = (kr * jnp.cos(ang) + kr_partner * (signr_ref[...] * jnp.sin(ang))).astype(BF16)
    k_nope = _dot(ckvn.astype(BF16), wuk_ref[...])
    for h in range(N_HEADS):
        k_ref[0, h] = jnp.concatenate(
            [k_nope[:, h * QK_NOPE:(h + 1) * QK_NOPE].astype(BF16), k_rope], axis=1)


def _mla_prep(x2, pos_row, pos_col, w_b, q_g, kv_g, w_uqt, w_uk, w_uvt, freq_col, freq_row, sign_col, sign_row,
              B, S):
    T = x2.shape[0]
    tm = TOKEN_TILE
    spb = S // tm
    n_b = w_b.shape[1]
    head_map = lambda i: (i // spb, 0, 0, i % spb)
    return pl.pallas_call(
        _mla_prep_kernel,
        grid=(T // tm,),
        in_specs=[
            pl.BlockSpec((tm, D_MODEL), lambda i: (i, 0)),
            pl.BlockSpec((1, 1, tm), lambda i: (i, 0, 0)),
            pl.BlockSpec((tm, 1), lambda i: (i, 0)),
            _const_spec((D_MODEL, n_b)),
            _const_spec((1, Q_LORA)),
            _const_spec((1, KV_LORA)),
            _const_spec((N_HEADS * QK_PAD, Q_LORA)),
            _const_spec((KV_LORA, N_HEADS * QK_NOPE)),
            _const_spec((N_HEADS * V_HEAD, KV_LORA)),
            _const_spec((LANES, 1)),
            _const_spec((1, LANES)),
            _const_spec((LANES, 1)),
            _const_spec((1, LANES)),
        ],
        out_specs=[
            pl.BlockSpec((1, N_HEADS, QK_PAD, tm), head_map),
            pl.BlockSpec((1, N_HEADS, tm, QK_PAD), lambda i: (i // spb, 0, i % spb, 0)),
            pl.BlockSpec((1, N_HEADS, V_HEAD, tm), head_map),
            pl.BlockSpec((tm, D_MODEL), lambda i: (i, 0)),
        ],
        out_shape=[
            jax.ShapeDtypeStruct((B, N_HEADS, QK_PAD, S), BF16),
            jax.ShapeDtypeStruct((B, N_HEADS, S, QK_PAD), BF16),
            jax.ShapeDtypeStruct((B, N_HEADS, V_HEAD, S), BF16),
            jax.ShapeDtypeStruct((T, D_MODEL), BF16),
        ],
        compiler_params=_params(("arbitrary",)),
        name="mla_prep",
    )(x2, pos_row, pos_col, w_b, q_g, kv_g, w_uqt, w_uk, w_uvt, freq_col, freq_row, sign_col, sign_row)


def _attention_kernel(qt_ref, k_ref, vt_ref, o_ref):
    S = k_ref.shape[2]
    tq = qt_ref.shape[3]
    q_t = qt_ref[0, 0]

    def step(j, carry):
        m, l, acc = carry
        start = pl.multiple_of(j * KV_TILE, KV_TILE)
        s_t = _dot(k_ref[0, 0, pl.ds(start, KV_TILE), :], q_t)
        m_new = jnp.maximum(m, jnp.max(s_t, axis=0, keepdims=True))
        p_t = jnp.exp(s_t - m_new)
        alpha = jnp.exp(m - m_new)
        l = alpha * l + jnp.sum(p_t, axis=0, keepdims=True)
        acc = alpha * acc + _dot(vt_ref[0, 0, :, pl.ds(start, KV_TILE)], p_t.astype(BF16))
        return m_new, l, acc

    init = (jnp.full((1, tq), NEG_INF, F32), jnp.zeros((1, tq), F32), jnp.zeros((V_HEAD, tq), F32))
    _, l, acc = lax.fori_loop(0, S // KV_TILE, step, init)
    o_ref[0] = (acc / l).T.astype(BF16)


def _attention(q_t, k, v_t):
    B, H, _, S = q_t.shape
    tq = Q_TILE
    return pl.pallas_call(
        _attention_kernel,
        grid=(B, H, S // tq),
        in_specs=[
            pl.BlockSpec((1, 1, QK_PAD, tq), lambda b, h, i: (b, h, 0, i)),
            pl.BlockSpec((1, 1, S, QK_PAD), lambda b, h, i: (b, h, 0, 0)),
            pl.BlockSpec((1, 1, V_HEAD, S), lambda b, h, i: (b, h, 0, 0)),
        ],
        out_specs=pl.BlockSpec((1, tq, V_HEAD), lambda b, h, i: (b, i, h)),
        out_shape=jax.ShapeDtypeStruct((B, S, H * V_HEAD), BF16),
        compiler_params=_params(("arbitrary", "arbitrary", "arbitrary")),
        name="attention",
    )(q_t, k, v_t)


def _mix_route_kernel(x_ref, ga_ref, sgb_ref, o_ref, wbp_ref, wout_ref, g1_ref, b1_ref, wr_ref, rb_ref,
                      h1_ref, h1p_ref, idx_ref, rank_ref, gw_ref, cnt_ref, base_ref):
    tm = x_ref.shape[0]

    @pl.when(pl.program_id(0) == 0)
    def _():
        base_ref[...] = jnp.zeros_like(base_ref)

    yb = _dot(o_ref[...], wbp_ref[...])
    merged = ga_ref[...].astype(F32) + sgb_ref[...].astype(F32) * yb
    mixed = _dot(merged.astype(BF16), wout_ref[...])
    h1 = _layer_norm(DN_ALPHA * x_ref[...] + mixed, g1_ref[...], b1_ref[...])
    h1_ref[...] = h1
    half = D_MODEL // 2
    h1p_ref[...] = _pack_bf16_pair(h1[:, :half], h1[:, half:])

    scores_t = _sigmoid(_dot(h1.astype(BF16), wr_ref[...])).T
    biased = scores_t + rb_ref[...]

    riota = lax.broadcasted_iota(jnp.int32, (GROUP_SIZE, tm), 0)
    blocks, gscore = [], []
    for g in range(N_GROUPS):
        blk = biased[g * GROUP_SIZE:(g + 1) * GROUP_SIZE]
        m1 = jnp.max(blk, axis=0, keepdims=True)
        i1 = jnp.min(jnp.where(blk == m1, riota, GROUP_SIZE), axis=0, keepdims=True)
        m2 = jnp.max(jnp.where(riota == i1, NEG_INF, blk), axis=0, keepdims=True)
        blocks.append(blk)
        gscore.append(m1 + m2)

    selected = [jnp.zeros((1, tm), F32) for _ in range(N_GROUPS)]
    for _ in range(TOPK_GROUPS):
        best = functools.reduce(jnp.maximum, gscore)
        first = functools.reduce(
            jnp.minimum, [jnp.where(gscore[g] == best, g, N_GROUPS) for g in range(N_GROUPS)])
        for g in range(N_GROUPS):
            hit = first == g
            selected[g] = jnp.where(hit, 1.0, selected[g])
            gscore[g] = jnp.where(hit, NEG_INF, gscore[g])

    masked = jnp.concatenate(
        [jnp.where(jnp.broadcast_to(selected[g], blocks[g].shape) > 0.5, blocks[g], NEG_INF)
         for g in range(N_GROUPS)], axis=0)

    eiota = lax.broadcasted_iota(jnp.int32, (N_EXPERTS, tm), 0)
    picks, weights = [], []
    for _ in range(TOP_K):
        best = jnp.max(masked, axis=0, keepdims=True)
        e = jnp.min(jnp.where(masked == best, eiota, N_EXPERTS), axis=0, keepdims=True)
        hit = eiota == e
        masked = jnp.where(hit, NEG_INF, masked)
        picks.append(e)
        weights.append(jnp.sum(jnp.where(hit, scores_t, 0.0), axis=0, keepdims=True))
    wsum = functools.reduce(lambda a, b: a + b, weights)

    chosen = functools.reduce(
        lambda a, b: a + b, [jnp.where(eiota == e, 1.0, 0.0) for e in picks])
    before = (lax.broadcasted_iota(jnp.int32, (tm, tm), 0)
              < lax.broadcasted_iota(jnp.int32, (tm, tm), 1)).astype(BF16)
    rank_full = _dot(chosen.astype(BF16), before) + base_ref[:, :1]
    new_base = base_ref[:, :1] + jnp.sum(chosen, axis=1, keepdims=True)
    base_ref[...] = jnp.broadcast_to(new_base, base_ref.shape)
    cnt_ref[...] = jnp.broadcast_to(new_base, cnt_ref.shape)

    kiota = lax.broadcasted_iota(jnp.int32, (TOP_K, tm), 0)
    wiota = lax.broadcasted_iota(jnp.int32, (LANES, tm), 0)
    idx_out = jnp.zeros((TOP_K, tm), jnp.int32)
    rank_out = jnp.zeros((TOP_K, tm), jnp.int32)
    gw_t = jnp.zeros((LANES, tm), F32)
    for k in range(TOP_K):
        r = jnp.sum(jnp.where(eiota == picks[k], rank_full, 0.0), axis=0, keepdims=True)
        idx_out = jnp.where(kiota == k, picks[k], idx_out)
        rank_out = jnp.where(kiota == k, r.astype(jnp.int32), rank_out)
        gw_t = jnp.where(wiota == k, weights[k] / wsum * ROUTED_SCALE, gw_t)
    idx_ref[...] = idx_out
    rank_ref[...] = rank_out
    gw_ref[...] = gw_t.T


def _mix_route(x2, ga, sgb, o, w_bp, w_out, g1, b1, w_r, r_bias):
    T = x2.shape[0]
    tm = TOKEN_TILE
    row = lambda i: (i, 0)
    col = lambda i: (0, i)
    return pl.pallas_call(
        _mix_route_kernel,
        grid=(T // tm,),
        in_specs=[
            pl.BlockSpec((tm, D_MODEL), row),
            pl.BlockSpec((tm, D_MODEL), row),
            pl.BlockSpec((tm, D_MODEL), row),
            pl.BlockSpec((tm, D_MODEL), row),
            _const_spec((D_MODEL, D_MODEL)),
            _const_spec((D_MODEL, D_MODEL)),
            _const_spec((1, D_MODEL)),
            _const_spec((1, D_MODEL)),
            _const_spec((D_MODEL, N_EXPERTS)),
            _const_spec((N_EXPERTS, 1)),
        ],
        out_specs=[
            pl.BlockSpec((tm, D_MODEL), row),
            pl.BlockSpec((tm, D_MODEL // 2), row),
            pl.BlockSpec((TOP_K, tm), col),
            pl.BlockSpec((TOP_K, tm), col),
            pl.BlockSpec((tm, LANES), row),
            _const_spec((N_EXPERTS, LANES)),
        ],
        out_shape=[
            jax.ShapeDtypeStruct((T, D_MODEL), F32),
            jax.ShapeDtypeStruct((T, D_MODEL // 2), jnp.uint32),
            jax.ShapeDtypeStruct((TOP_K, T), jnp.int32),
            jax.ShapeDtypeStruct((TOP_K, T), jnp.int32),
            jax.ShapeDtypeStruct((T, LANES), F32),
            jax.ShapeDtypeStruct((N_EXPERTS, LANES), F32),
        ],
        scratch_shapes=[pltpu.VMEM((N_EXPERTS, LANES), F32)],
        compiler_params=_params(("arbitrary",)),
        name="mix_route",
    )(x2, ga, sgb, o, w_bp, w_out, g1, b1, w_r, r_bias)


def _row_copy_dispatch(src_ref, dst_ref, sem, t, dst_row):
    return pltpu.make_async_copy(src_ref.at[pl.ds(t, 1), :], dst_ref.at[pl.ds(dst_row, 1), :], sem)


def _dispatch_kernel(starts_ref, idx_ref, rank_ref, h1p_ref, xs_ref, sem):
    tm = h1p_ref.shape[0]

    def issue(t, carry):
        for k in range(TOP_K):
            dst_row = starts_ref[idx_ref[k, t]] + rank_ref[k, t]
            _row_copy_dispatch(h1p_ref, xs_ref, sem, t, dst_row).start()
        return carry

    lax.fori_loop(0, tm, issue, 0)

    def drain(t, carry):
        for k in range(TOP_K):
            _row_copy_dispatch(h1p_ref, xs_ref, sem, t, 0).wait()
        return carry

    lax.fori_loop(0, tm, drain, 0)


def _dispatch(starts, idx_t, rank_t, h1p):
    T, half = h1p.shape
    tm = TOKEN_TILE
    col = lambda i: (0, i)
    return pl.pallas_call(
        _dispatch_kernel,
        grid=(T // tm,),
        in_specs=[
            pl.BlockSpec(memory_space=pltpu.SMEM),
            pl.BlockSpec((TOP_K, tm), col, memory_space=pltpu.SMEM),
            pl.BlockSpec((TOP_K, tm), col, memory_space=pltpu.SMEM),
            pl.BlockSpec((tm, half), lambda i: (i, 0)),
        ],
        out_specs=pl.BlockSpec(memory_space=pl.ANY),
        out_shape=jax.ShapeDtypeStruct((T * TOP_K, half), jnp.uint32),
        scratch_shapes=[pltpu.SemaphoreType.DMA(())],
        compiler_params=_params(("arbitrary",)),
        name="dispatch",
    )(starts, idx_t, rank_t, h1p)


def _experts_kernel(blk_ref, exp_ref, lo_ref, hi_ref, xs_ref, wg_ref, wu_ref, wd_ref, ys_ref):
    i = pl.program_id(0)
    blk = blk_ref[i]
    lo = lo_ref[i]
    hi = hi_ref[i]
    prev_blk = blk_ref[jnp.maximum(i - 1, 0)]

    @pl.when((i == 0) | (prev_blk != blk))
    def _():
        ys_ref[...] = jnp.zeros_like(ys_ref)

    @pl.when(hi > lo)
    def _():
        x_lo, x_hi = _unpack_bf16_pair(xs_ref[...])
        x = jnp.concatenate([x_lo, x_hi], axis=1).astype(BF16)
        gate = _dot(x, wg_ref[0].astype(BF16))
        up = _dot(x, wu_ref[0].astype(BF16))
        hidden = (gate * _sigmoid(gate) * up).astype(BF16)
        y = _dot(hidden, wd_ref[0].astype(BF16))
        half = D_MODEL // 2
        packed = _pack_bf16_pair(y[:, :half], y[:, half:])
        rows = blk * ROW_BLOCK + lax.broadcasted_iota(jnp.int32, (ROW_BLOCK, 1), 0)
        ys_ref[...] = jnp.where((rows >= lo) & (rows < hi), packed, ys_ref[...])


def _experts(item_blk, item_exp, item_lo, item_hi, xs, w_gate, w_up, w_down):
    n_rows, half = xs.shape
    n_items = item_blk.shape[0]
    grid_spec = pltpu.PrefetchScalarGridSpec(
        num_scalar_prefetch=4,
        grid=(n_items,),
        in_specs=[
            pl.BlockSpec((ROW_BLOCK, half), lambda i, b, e, lo, hi: (b[i], 0)),
            pl.BlockSpec((1, D_MODEL, D_EXPERT), lambda i, b, e, lo, hi: (e[i], 0, 0)),
            pl.BlockSpec((1, D_MODEL, D_EXPERT), lambda i, b, e, lo, hi: (e[i], 0, 0)),
            pl.BlockSpec((1, D_EXPERT, D_MODEL), lambda i, b, e, lo, hi: (e[i], 0, 0)),
        ],
        out_specs=pl.BlockSpec((ROW_BLOCK, half), lambda i, b, e, lo, hi: (b[i], 0)),
    )
    return pl.pallas_call(
        _experts_kernel,
        grid_spec=grid_spec,
        out_shape=jax.ShapeDtypeStruct((n_rows, half), jnp.uint32),
        compiler_params=_params(("arbitrary",)),
        name="experts",
    )(item_blk, item_exp, item_lo, item_hi, xs, w_gate, w_up, w_down)


def _row_copy_combine(src_ref, dst_ref, sem, src_row, dst_row):
    return pltpu.make_async_copy(src_ref.at[pl.ds(src_row, 1), :], dst_ref.at[pl.ds(dst_row, 1), :], sem)


def _combine_kernel(starts_ref, idx_ref, rank_ref, gw_ref, h1_ref, ys_ref, wsg_ref, wsu_ref, wsd_ref,
                    g2_ref, b2_ref, out_ref, buf_ref, sem):
    tm = h1_ref.shape[0]

    def issue(t, carry):
        for k in range(TOP_K):
            src_row = starts_ref[idx_ref[k, t]] + rank_ref[k, t]
            _row_copy_combine(ys_ref, buf_ref, sem, src_row, k * tm + t).start()
        return carry

    lax.fori_loop(0, tm, issue, 0)

    h1 = h1_ref[...]
    hb = h1.astype(BF16)
    gate = _dot(hb, wsg_ref[...])
    hidden = (gate * _sigmoid(gate) * _dot(hb, wsu_ref[...])).astype(BF16)
    moe = _dot(hidden, wsd_ref[...])

    def drain(t, carry):
        for k in range(TOP_K):
            _row_copy_combine(ys_ref, buf_ref, sem, 0, k * tm + t).wait()
        return carry

    lax.fori_loop(0, tm, drain, 0)

    gw = gw_ref[...]
    for k in range(TOP_K):
        y_lo, y_hi = _unpack_bf16_pair(buf_ref[k * tm:(k + 1) * tm, :])
        moe = moe + gw[:, k:k + 1] * jnp.concatenate([y_lo, y_hi], axis=1)
    out_ref[...] = _layer_norm(DN_ALPHA * h1 + moe, g2_ref[...], b2_ref[...])


def _combine(starts, idx_t, rank_t, gw, h1, ys, w_sg, w_su, w_sd, g2, b2):
    T = h1.shape[0]
    half = ys.shape[1]
    tm = TOKEN_TILE
    row = lambda i: (i, 0)
    col = lambda i: (0, i)
    return pl.pallas_call(
        _combine_kernel,
        grid=(T // tm,),
        in_specs=[
            pl.BlockSpec(memory_space=pltpu.SMEM),
            pl.BlockSpec((TOP_K, tm), col, memory_space=pltpu.SMEM),
            pl.BlockSpec((TOP_K, tm), col, memory_space=pltpu.SMEM),
            pl.BlockSpec((tm, LANES), row),
            pl.BlockSpec((tm, D_MODEL), row),
            pl.BlockSpec(memory_space=pl.ANY),
            _const_spec((D_MODEL, D_EXPERT)),
            _const_spec((D_MODEL, D_EXPERT)),
            _const_spec((D_EXPERT, D_MODEL)),
            _const_spec((1, D_MODEL)),
            _const_spec((1, D_MODEL)),
        ],
        out_specs=pl.BlockSpec((tm, D_MODEL), row),
        out_shape=jax.ShapeDtypeStruct((T, D_MODEL), F32),
        scratch_shapes=[pltpu.VMEM((TOP_K * tm, half), jnp.uint32), pltpu.SemaphoreType.DMA(())],
        compiler_params=_params(("arbitrary",)),
        name="combine",
    )(starts, idx_t, rank_t, gw, h1, ys, w_sg, w_su, w_sd, g2, b2)


def _expert_work_items(counts, n_rows):
    n_blocks = n_rows // ROW_BLOCK
    n_items = n_blocks + N_EXPERTS
    ends = jnp.cumsum(counts)
    starts = ends - counts
    first_blk = starts // ROW_BLOCK
    last_blk = jnp.maximum(ends - 1, 0) // ROW_BLOCK
    per_expert = jnp.where(counts > 0, last_blk - first_blk + 1, 0)
    item_end = jnp.cumsum(per_expert)
    item_start = item_end - per_expert
    total = item_end[-1]
    i = jnp.arange(n_items, dtype=jnp.int32)
    live = i < total
    j = jnp.minimum(i, total - 1)
    exp = jnp.minimum(jnp.searchsorted(item_end, j, side="right"), N_EXPERTS - 1).astype(jnp.int32)
    blk = (first_blk[exp] + (j - item_start[exp])).astype(jnp.int32)
    lo = jnp.maximum(starts[exp], blk * ROW_BLOCK)
    hi = jnp.minimum(ends[exp], (blk + 1) * ROW_BLOCK)
    lo = jnp.where(live, lo, 0).astype(jnp.int32)
    hi = jnp.where(live, hi, 0).astype(jnp.int32)
    return starts.astype(jnp.int32), blk, exp, lo, hi


def kernel(x, positions, w_in, gmlp_ln_g, gmlp_ln_b, w_spatial, b_spatial, w_a_proj, q_norm_g, w_uq, kv_norm_g, w_uk, w_uv, w_b_proj, w_out, ln1_g, ln1_b, w_router, router_bias, w_gate, w_up, w_down, w_sh_gate, w_sh_up, w_sh_down, ln2_g, ln2_b):
    B, S, D = x.shape
    T = B * S
    x2 = x.reshape(T, D)

    w = w_in[0]
    o_u, o_v, o_cq = 0, A_WIDTH, 2 * A_WIDTH
    o_ckv = o_cq + Q_LORA
    o_kr = o_ckv + KV_LORA
    o_ga = o_kr + QK_ROPE
    o_gb = o_ga + D_MODEL
    half_r = QK_ROPE // 2
    w_kr = w[:, o_kr:o_ga]
    w_kr_partner = jnp.concatenate([w_kr[:, half_r:], w_kr[:, :half_r]], axis=1)
    w_b = jnp.concatenate(
        [w[:, o_cq:o_kr], w_kr, w_kr, w_kr_partner, w_kr_partner, w[:, o_gb:]], axis=1).astype(BF16)
    uq = w_uq[0]
    uq_rope = uq[:, :, QK_NOPE:]
    uq_partner = jnp.concatenate([uq_rope[:, :, half_r:], uq_rope[:, :, :half_r]], axis=2)
    w_uqt = jnp.concatenate([uq, uq_partner], axis=2).reshape(Q_LORA, N_HEADS * QK_PAD).T.astype(BF16)
    w_ukm = w_uk[0].reshape(KV_LORA, N_HEADS * QK_NOPE).astype(BF16)
    w_uvt = w_uv[0].reshape(KV_LORA, N_HEADS * V_HEAD).T.astype(BF16)

    freq = ROPE_THETA ** (-jnp.arange(0, half_r, dtype=F32) * 2.0 / QK_ROPE)
    freq128 = jnp.tile(freq, LANES // half_r)
    sign128 = jnp.tile(jnp.concatenate([-jnp.ones((half_r,), F32), jnp.ones((half_r,), F32)]), LANES // QK_ROPE)
    pos_f = positions.astype(F32).reshape(T)
    bias_full = jnp.repeat(b_spatial[0].T, A_GROUP_DIM, axis=1)

    ga = _gmlp_branch(
        x2, w[:, o_u:o_v].astype(BF16), w[:, o_v:o_cq].astype(BF16), w[:, o_ga:o_gb].astype(BF16),
        gmlp_ln_g[0].reshape(1, A_WIDTH), gmlp_ln_b[0].reshape(1, A_WIDTH),
        w_spatial[0].astype(BF16), bias_full, w_a_proj[0].astype(BF16))

    q_t, k, v_t, sgb = _mla_prep(
        x2, pos_f.reshape(T // TOKEN_TILE, 1, TOKEN_TILE), pos_f.reshape(T, 1), w_b,
        q_norm_g[0].reshape(1, Q_LORA), kv_norm_g[0].reshape(1, KV_LORA), w_uqt, w_ukm, w_uvt,
        freq128.reshape(LANES, 1), freq128.reshape(1, LANES), sign128.reshape(LANES, 1), sign128.reshape(1, LANES),
        B, S)

    o = _attention(q_t, k, v_t).reshape(T, N_HEADS * V_HEAD)

    h1, h1p, idx_t, rank_t, gw, cnt = _mix_route(
        x2, ga, sgb, o, w_b_proj[0].astype(BF16), w_out[0].astype(BF16),
        ln1_g[0].reshape(1, D), ln1_b[0].reshape(1, D), w_router[0].astype(BF16),
        router_bias[0].reshape(N_EXPERTS, 1))

    counts = cnt[:, 0].astype(jnp.int32)
    starts, item_blk, item_exp, item_lo, item_hi = _expert_work_items(counts, T * TOP_K)

    xs = _dispatch(starts, idx_t, rank_t, h1p)
    ys = _experts(item_blk, item_exp, item_lo, item_hi, xs, w_gate[0], w_up[0], w_down[0])
    out = _combine(
        starts, idx_t, rank_t, gw, h1, ys, w_sh_gate[0].astype(BF16), w_sh_up[0].astype(BF16),
        w_sh_down[0].astype(BF16), ln2_g[0].reshape(1, D), ln2_b[0].reshape(1, D))
    return out.reshape(B, S, D)
```

```python
import functools
import math

import jax
import jax.numpy as jnp
from jax import lax
from jax.experimental import pallas as pl
from jax.experimental.pallas import tpu as pltpu

D_MODEL = 1024
CHUNK = 128
A_GROUPS = 8
A_GROUP_DIM = 128
A_WIDTH = A_GROUPS * A_GROUP_DIM
N_HEADS = 8
QK_NOPE = 128
QK_ROPE = 64
QK_DIM = QK_NOPE + QK_ROPE
V_HEAD = 128
Q_LORA = 384
KV_LORA = 256
ROPE_THETA = 10000.0
N_EXPERTS = 256
TOP_K = 8
N_GROUPS = 8
TOPK_GROUPS = 4
GROUP_SIZE = N_EXPERTS // N_GROUPS
D_EXPERT = 256
ROUTED_SCALE = 2.5
DN_ALPHA = 2.0 ** 0.25
LN_EPS = 1e-5
RMS_EPS = 1e-6

QK_PAD = 256
LANES = 128
TOKEN_TILE = 256
Q_TILE = 1024
KV_TILE = 512
KV_GROUP = 4
SLOT_LANES = 2048
ISSUE_UNROLL = 4
ROW_BLOCK = 256
VMEM_LIMIT = 56 * 1024 * 1024

F32 = jnp.float32
BF16 = jnp.bfloat16
NEG_INF = float("-inf")


def _dot(a, b):
    return jnp.dot(a, b, preferred_element_type=F32)


def _sigmoid(x):
    return 1.0 / (1.0 + jnp.exp(-x))


def _gelu_tanh(x):
    return 0.5 * x * (1.0 + jnp.tanh(math.sqrt(2.0 / math.pi) * (x + 0.044715 * (x * x * x))))


def _layer_norm(x, g, b):
    mu = jnp.mean(x, axis=-1, keepdims=True)
    d = x - mu
    var = jnp.mean(d * d, axis=-1, keepdims=True)
    return d * lax.rsqrt(var + LN_EPS) * g + b


def _rms_norm(x, g):
    return x * lax.rsqrt(jnp.mean(x * x, axis=-1, keepdims=True) + RMS_EPS) * g


def _const_spec(shape):
    zeros = (0,) * len(shape)
    return pl.BlockSpec(shape, lambda *_: zeros)


def _params(semantics):
    return pltpu.CompilerParams(dimension_semantics=semantics, vmem_limit_bytes=VMEM_LIMIT)


def _gmlp_kernel(x_ref, wu_ref, wv_ref, wg_ref, lng_ref, lnb_ref, ws_ref, bias_ref, wa_ref, o_ref, mix_ref):
    tm = x_ref.shape[0]
    xb = x_ref[...].astype(BF16)
    v = _gelu_tanh(_dot(xb, wv_ref[...]))
    for h in range(A_GROUPS):
        cols = slice(h * A_GROUP_DIM, (h + 1) * A_GROUP_DIM)
        vn = _layer_norm(v[:, cols], lng_ref[:, cols], lnb_ref[:, cols]).astype(BF16)
        for c in range(tm // CHUNK):
            rows = slice(c * CHUNK, (c + 1) * CHUNK)
            mix_ref[rows, cols] = _dot(ws_ref[h], vn[rows, :]) + bias_ref[:, cols]
    u = _gelu_tanh(_dot(xb, wu_ref[...]))
    ya = _dot((u * mix_ref[...]).astype(BF16), wa_ref[...])
    gate = _sigmoid(_dot(xb, wg_ref[...]))
    o_ref[...] = (gate * ya).astype(BF16)


def _gmlp_branch(x2, w_u, w_v, w_ga, ln_g, ln_b, w_s, bias_full, w_a):
    T = x2.shape[0]
    tm = TOKEN_TILE
    return pl.pallas_call(
        _gmlp_kernel,
        grid=(T // tm,),
        in_specs=[
            pl.BlockSpec((tm, D_MODEL), lambda i: (i, 0)),
            _const_spec((D_MODEL, A_WIDTH)),
            _const_spec((D_MODEL, A_WIDTH)),
            _const_spec((D_MODEL, D_MODEL)),
            _const_spec((1, A_WIDTH)),
            _const_spec((1, A_WIDTH)),
            _const_spec((A_GROUPS, CHUNK, CHUNK)),
            _const_spec((CHUNK, A_WIDTH)),
            _const_spec((A_WIDTH, D_MODEL)),
        ],
        out_specs=pl.BlockSpec((tm, D_MODEL), lambda i: (i, 0)),
        out_shape=jax.ShapeDtypeStruct((T, D_MODEL), BF16),
        scratch_shapes=[pltpu.VMEM((tm, A_WIDTH), F32)],
        compiler_params=_params(("arbitrary",)),
        name="gmlp_branch",
    )(x2, w_u, w_v, w_ga, ln_g, ln_b, w_s, bias_full, w_a)


def _mla_prep_kernel(x_ref, posr_ref, posc_ref, wb_ref, qg_ref, kvg_ref, wuqt_ref, wuk_ref, wuvt_ref,
                     freqc_ref, freqr_ref, signc_ref, signr_ref,
                     qt_ref, k_ref, vt_ref, sgb_ref):
    xb = x_ref[...].astype(BF16)
    pb = _dot(xb, wb_ref[...])
    c_q = pb[:, :Q_LORA]
    c_kv = pb[:, Q_LORA:Q_LORA + KV_LORA]
    o = Q_LORA + KV_LORA
    kr = pb[:, o:o + LANES]
    kr_partner = pb[:, o + LANES:o + 2 * LANES]
    gb = pb[:, o + 2 * LANES:]
    sgb_ref[...] = _sigmoid(gb).astype(BF16)

    cqn = _rms_norm(c_q, qg_ref[...])
    ckvn = _rms_norm(c_kv, kvg_ref[...])
    cqn_t = cqn.T.astype(BF16)
    ckvn_t = ckvn.T.astype(BF16)

    scale = math.log2(math.e) / math.sqrt(QK_DIM)
    ang_t = freqc_ref[...] * posr_ref[0]
    table_t = jnp.concatenate(
        [jnp.cos(ang_t[:QK_ROPE]), signc_ref[QK_ROPE:] * jnp.sin(ang_t[QK_ROPE:])], axis=0) * scale
    q_t = _dot(wuqt_ref[...], cqn_t)
    for h in range(N_HEADS):
        blk = q_t[h * QK_PAD:(h + 1) * QK_PAD]
        qt_ref[0, h] = jnp.concatenate(
            [blk[:QK_NOPE] * scale, blk[QK_NOPE:] * table_t], axis=0).astype(BF16)

    v_t = _dot(wuvt_ref[...], ckvn_t)
    for h in range(N_HEADS):
        vt_ref[0, h] = v_t[h * V_HEAD:(h + 1) * V_HEAD].astype(BF16)

    ang = posc_ref[...] * freqr_ref[...]
    k_rope = (kr * jnp.cos(ang) + kr_partner * (signr_ref[...] * jnp.sin(ang))).astype(BF16)
    k_nope = _dot(ckvn.astype(BF16), wuk_ref[...])
    for h in range(N_HEADS):
        k_ref[0, h] = jnp.concatenate(
            [k_nope[:, h * QK_NOPE:(h + 1) * QK_NOPE].astype(BF16), k_rope], axis=1)


def _mla_prep(x2, pos_row, pos_col, w_b, q_g, kv_g, w_uqt, w_uk, w_uvt, freq_col, freq_row, sign_col, sign_row,
              B, S):
    T = x2.shape[0]
    tm = TOKEN_TILE
    spb = S // tm
    n_b = w_b.shape[1]
    head_map = lambda i: (i // spb, 0, 0, i % spb)
    return pl.pallas_call(
        _mla_prep_kernel,
        grid=(T // tm,),
        in_specs=[
            pl.BlockSpec((tm, D_MODEL), lambda i: (i, 0)),
            pl.BlockSpec((1, 1, tm), lambda i: (i, 0, 0)),
            pl.BlockSpec((tm, 1), lambda i: (i, 0)),
            _const_spec((D_MODEL, n_b)),
            _const_spec((1, Q_LORA)),
            _const_spec((1, KV_LORA)),
            _const_spec((N_HEADS * QK_PAD, Q_LORA)),
            _const_spec((KV_LORA, N_HEADS * QK_NOPE)),
            _const_spec((N_HEADS * V_HEAD, KV_LORA)),
            _const_spec((LANES, 1)),
            _const_spec((1, LANES)),
            _const_spec((LANES, 1)),
            _const_spec((1, LANES)),
        ],
        out_specs=[
            pl.BlockSpec((1, N_HEADS, QK_PAD, tm), head_map),
            pl.BlockSpec((1, N_HEADS, tm, QK_PAD), lambda i: (i // spb, 0, i % spb, 0)),
            pl.BlockSpec((1, N_HEADS, V_HEAD, tm), head_map),
            pl.BlockSpec((tm, D_MODEL), lambda i: (i, 0)),
        ],
        out_shape=[
            jax.ShapeDtypeStruct((B, N_HEADS, QK_PAD, S), BF16),
            jax.ShapeDtypeStruct((B, N_HEADS, S, QK_PAD), BF16),
            jax.ShapeDtypeStruct((B, N_HEADS, V_HEAD, S), BF16),
            jax.ShapeDtypeStruct((T, D_MODEL), BF16),
        ],
        compiler_params=_params(("arbitrary",)),
        name="mla_prep",
    )(x2, pos_row, pos_col, w_b, q_g, kv_g, w_uqt, w_uk, w_uvt, freq_col, freq_row, sign_col, sign_row)


def _attention_kernel(qt_ref, k_ref, vt_ref, o_ref, s0_ref, s1_ref, m_ref, l_ref, acc_ref):
    S = k_ref.shape[2]
    n_chunks = S // KV_TILE
    q_t = qt_ref[0, 0]

    def scores(c, dst_ref):
        start = pl.multiple_of(c * KV_TILE, KV_TILE)
        dst_ref[...] = _dot(k_ref[0, 0, pl.ds(start, KV_TILE), :], q_t)

    def accumulate(c, src_ref):
        start = pl.multiple_of(c * KV_TILE, KV_TILE)
        s_t = src_ref[...]
        m = m_ref[...]
        m_new = jnp.maximum(m, jnp.max(s_t, axis=0, keepdims=True))
        p_t = jnp.exp2(s_t - m_new)
        alpha = jnp.exp2(m - m_new)
        m_ref[...] = m_new
        l_ref[...] = alpha * l_ref[...] + jnp.sum(p_t, axis=0, keepdims=True)
        acc_ref[...] = alpha * acc_ref[...] + _dot(vt_ref[0, 0, :, pl.ds(start, KV_TILE)], p_t.astype(BF16))

    m_ref[...] = jnp.full(m_ref.shape, NEG_INF, F32)
    l_ref[...] = jnp.zeros(l_ref.shape, F32)
    acc_ref[...] = jnp.zeros(acc_ref.shape, F32)
    scores(0, s0_ref)

    bufs = (s0_ref, s1_ref)

    def group(c0, prefetch_last):
        for i in range(KV_GROUP):
            if i + 1 < KV_GROUP or prefetch_last:
                scores(c0 + i + 1, bufs[(i + 1) % 2])
            accumulate(c0 + i, bufs[i % 2])

    def body(g, carry):
        group(g * KV_GROUP, True)
        return carry

    lax.fori_loop(0, n_chunks // KV_GROUP - 1, body, 0)
    group(n_chunks - KV_GROUP, False)
    o_ref[0] = (acc_ref[...] / l_ref[...]).T.astype(BF16)


def _attention(q_t, k, v_t):
    B, H, _, S = q_t.shape
    tq = min(Q_TILE, S)
    assert S % tq == 0 and S % (KV_TILE * KV_GROUP) == 0 and KV_GROUP % 2 == 0
    return pl.pallas_call(
        _attention_kernel,
        grid=(B, H, S // tq),
        in_specs=[
            pl.BlockSpec((1, 1, QK_PAD, tq), lambda b, h, i: (b, h, 0, i)),
            pl.BlockSpec((1, 1, S, QK_PAD), lambda b, h, i: (b, h, 0, 0)),
            pl.BlockSpec((1, 1, V_HEAD, S), lambda b, h, i: (b, h, 0, 0)),
        ],
        out_specs=pl.BlockSpec((1, tq, V_HEAD), lambda b, h, i: (b, i, h)),
        out_shape=jax.ShapeDtypeStruct((B, S, H * V_HEAD), BF16),
        scratch_shapes=[
            pltpu.VMEM((KV_TILE, tq), F32),
            pltpu.VMEM((KV_TILE, tq), F32),
            pltpu.VMEM((1, tq), F32),
            pltpu.VMEM((1, tq), F32),
            pltpu.VMEM((V_HEAD, tq), F32),
        ],
        compiler_params=_params(("arbitrary", "arbitrary", "arbitrary")),
        name="attention",
    )(q_t, k, v_t)


def _mix_route_kernel(x_ref, ga_ref, sgb_ref, o_ref, wbp_ref, wout_ref, g1_ref, b1_ref, wr_ref, rb_ref,
                      h1_ref, idx_ref, rank_ref, gw_ref, cnt_ref, base_ref):
    tm = x_ref.shape[0]

    @pl.when(pl.program_id(0) == 0)
    def _():
        base_ref[...] = jnp.zeros_like(base_ref)

    yb = _dot(o_ref[...], wbp_ref[...])
    merged = ga_ref[...].astype(F32) + sgb_ref[...].astype(F32) * yb
    mixed = _dot(merged.astype(BF16), wout_ref[...])
    h1 = _layer_norm(DN_ALPHA * x_ref[...] + mixed, g1_ref[...], b1_ref[...])
    h1_ref[...] = h1

    scores_t = _sigmoid(_dot(h1.astype(BF16), wr_ref[...])).T
    biased = scores_t + rb_ref[...]

    riota = lax.broadcasted_iota(jnp.int32, (GROUP_SIZE, tm), 0)
    blocks, gscore = [], []
    for g in range(N_GROUPS):
        blk = biased[g * GROUP_SIZE:(g + 1) * GROUP_SIZE]
        m1 = jnp.max(blk, axis=0, keepdims=True)
        i1 = jnp.min(jnp.where(blk == m1, riota, GROUP_SIZE), axis=0, keepdims=True)
        m2 = jnp.max(jnp.where(riota == i1, NEG_INF, blk), axis=0, keepdims=True)
        blocks.append(blk)
        gscore.append(m1 + m2)

    selected = [jnp.zeros((1, tm), F32) for _ in range(N_GROUPS)]
    for _ in range(TOPK_GROUPS):
        best = functools.reduce(jnp.maximum, gscore)
        first = functools.reduce(
            jnp.minimum, [jnp.where(gscore[g] == best, g, N_GROUPS) for g in range(N_GROUPS)])
        for g in range(N_GROUPS):
            hit = first == g
            selected[g] = jnp.where(hit, 1.0, selected[g])
            gscore[g] = jnp.where(hit, NEG_INF, gscore[g])

    masked = jnp.concatenate(
        [jnp.where(jnp.broadcast_to(selected[g], blocks[g].shape) > 0.5, blocks[g], NEG_INF)
         for g in range(N_GROUPS)], axis=0)

    eiota = lax.broadcasted_iota(jnp.int32, (N_EXPERTS, tm), 0)
    picks, weights = [], []
    for _ in range(TOP_K):
        best = jnp.max(masked, axis=0, keepdims=True)
        e = jnp.min(jnp.where(masked == best, eiota, N_EXPERTS), axis=0, keepdims=True)
        hit = eiota == e
        masked = jnp.where(hit, NEG_INF, masked)
        picks.append(e)
        weights.append(jnp.sum(jnp.where(hit, scores_t, 0.0), axis=0, keepdims=True))
    wsum = functools.reduce(lambda a, b: a + b, weights)

    chosen = functools.reduce(
        lambda a, b: a + b, [jnp.where(eiota == e, 1.0, 0.0) for e in picks])
    before = (lax.broadcasted_iota(jnp.int32, (tm, tm), 0)
              < lax.broadcasted_iota(jnp.int32, (tm, tm), 1)).astype(BF16)
    rank_full = _dot(chosen.astype(BF16), before) + base_ref[:, :1]
    new_base = base_ref[:, :1] + jnp.sum(chosen, axis=1, keepdims=True)
    base_ref[...] = jnp.broadcast_to(new_base, base_ref.shape)
    cnt_ref[...] = jnp.broadcast_to(new_base, cnt_ref.shape)

    kiota = lax.broadcasted_iota(jnp.int32, (TOP_K, tm), 0)
    wiota = lax.broadcasted_iota(jnp.int32, (LANES, tm), 0)
    idx_out = jnp.zeros((TOP_K, tm), jnp.int32)
    rank_out = jnp.zeros((TOP_K, tm), jnp.int32)
    gw_t = jnp.zeros((LANES, tm), F32)
    for k in range(TOP_K):
        r = jnp.sum(jnp.where(eiota == picks[k], rank_full, 0.0), axis=0, keepdims=True)
        idx_out = jnp.where(kiota == k, picks[k], idx_out)
        rank_out = jnp.where(kiota == k, r.astype(jnp.int32), rank_out)
        gw_t = jnp.where(wiota == k, weights[k] / wsum * ROUTED_SCALE, gw_t)
    idx_ref[...] = idx_out
    rank_ref[...] = rank_out
    gw_ref[...] = gw_t.T


def _mix_route(x2, ga, sgb, o, w_bp, w_out, g1, b1, w_r, r_bias):
    T = x2.shape[0]
    tm = TOKEN_TILE
    row = lambda i: (i, 0)
    col = lambda i: (0, i)
    return pl.pallas_call(
        _mix_route_kernel,
        grid=(T // tm,),
        in_specs=[
            pl.BlockSpec((tm, D_MODEL), row),
            pl.BlockSpec((tm, D_MODEL), row),
            pl.BlockSpec((tm, D_MODEL), row),
            pl.BlockSpec((tm, D_MODEL), row),
            _const_spec((D_MODEL, D_MODEL)),
            _const_spec((D_MODEL, D_MODEL)),
            _const_spec((1, D_MODEL)),
            _const_spec((1, D_MODEL)),
            _const_spec((D_MODEL, N_EXPERTS)),
            _const_spec((N_EXPERTS, 1)),
        ],
        out_specs=[
            pl.BlockSpec((tm, D_MODEL), row),
            pl.BlockSpec((TOP_K, tm), col),
            pl.BlockSpec((TOP_K, tm), col),
            pl.BlockSpec((tm, LANES), row),
            _const_spec((N_EXPERTS, LANES)),
        ],
        out_shape=[
            jax.ShapeDtypeStruct((T, D_MODEL), F32),
            jax.ShapeDtypeStruct((TOP_K, T), jnp.int32),
            jax.ShapeDtypeStruct((TOP_K, T), jnp.int32),
            jax.ShapeDtypeStruct((T, LANES), F32),
            jax.ShapeDtypeStruct((N_EXPERTS, LANES), F32),
        ],
        scratch_shapes=[pltpu.VMEM((N_EXPERTS, LANES), F32)],
        compiler_params=_params(("arbitrary",)),
        name="mix_route",
    )(x2, ga, sgb, o, w_bp, w_out, g1, b1, w_r, r_bias)


def _slot_kernel(starts_ref, idx_ref, rank_ref, slot_ref):
    idx = idx_ref[...]

    def lookup(e, acc):
        return jnp.where(idx == e, starts_ref[e], acc)

    slot_ref[...] = lax.fori_loop(0, N_EXPERTS, lookup, jnp.zeros_like(idx), unroll=8) + rank_ref[...]


def _slot_index(starts, idx_t, rank_t):
    T = idx_t.shape[1]
    tl = min(SLOT_LANES, T)
    col = lambda i: (0, i)
    return pl.pallas_call(
        _slot_kernel,
        grid=(T // tl,),
        in_specs=[
            pl.BlockSpec(memory_space=pltpu.SMEM),
            pl.BlockSpec((TOP_K, tl), col),
            pl.BlockSpec((TOP_K, tl), col),
        ],
        out_specs=pl.BlockSpec((TOP_K, tl), col),
        out_shape=jax.ShapeDtypeStruct((TOP_K, T), jnp.int32),
        compiler_params=_params(("arbitrary",)),
        name="slot_index",
    )(starts, idx_t, rank_t)


def _dispatch_kernel(slot_ref, h1_ref, xs_ref, sem):
    tm = h1_ref.shape[0]

    def issue(t, carry):
        for k in range(TOP_K):
            pltpu.make_async_copy(
                h1_ref.at[pl.ds(t, 1), :], xs_ref.at[pl.ds(slot_ref[k * tm + t], 1), :], sem).start()
        return carry

    lax.fori_loop(0, tm, issue, 0, unroll=ISSUE_UNROLL)
    for _ in range(TOP_K):
        pltpu.make_async_copy(h1_ref, xs_ref.at[pl.ds(0, tm), :], sem).wait()


def _dispatch(slots_tiled, h1):
    T, width = h1.shape
    tm = TOKEN_TILE
    return pl.pallas_call(
        _dispatch_kernel,
        grid=(T // tm,),
        in_specs=[
            pl.BlockSpec((TOP_K * tm,), lambda i: (i,), memory_space=pltpu.SMEM),
            pl.BlockSpec((tm, width), lambda i: (i, 0)),
        ],
        out_specs=pl.BlockSpec(memory_space=pl.ANY),
        out_shape=jax.ShapeDtypeStruct((T * TOP_K, width), F32),
        scratch_shapes=[pltpu.SemaphoreType.DMA(())],
        compiler_params=_params(("arbitrary",)),
        name="dispatch",
    )(slots_tiled, h1)


def _experts_kernel(blk_ref, exp_ref, lo_ref, hi_ref, xs_ref, wg_ref, wu_ref, wd_ref, ys_ref,
                    wg_bf_ref, wu_bf_ref, wd_bf_ref):
    i = pl.program_id(0)
    blk = blk_ref[i]
    lo = lo_ref[i]
    hi = hi_ref[i]
    prev = jnp.maximum(i - 1, 0)

    @pl.when((i == 0) | (blk_ref[prev] != blk))
    def _():
        ys_ref[...] = jnp.zeros_like(ys_ref)

    @pl.when((i == 0) | (exp_ref[prev] != exp_ref[i]))
    def _():
        wg_bf_ref[...] = wg_ref[0].astype(BF16)
        wu_bf_ref[...] = wu_ref[0].astype(BF16)
        wd_bf_ref[...] = wd_ref[0].astype(BF16)

    @pl.when(hi > lo)
    def _():
        x = xs_ref[...].astype(BF16)
        gate = _dot(x, wg_bf_ref[...])
        up = _dot(x, wu_bf_ref[...])
        hidden = (gate * _sigmoid(gate) * up).astype(BF16)
        y = _dot(hidden, wd_bf_ref[...])
        rows = blk * ROW_BLOCK + lax.broadcasted_iota(jnp.int32, (ROW_BLOCK, 1), 0)
        ys_ref[...] = jnp.where((rows >= lo) & (rows < hi), y, ys_ref[...])


def _experts(item_blk, item_exp, item_lo, item_hi, xs, w_gate, w_up, w_down):
    n_rows, width = xs.shape
    n_items = item_blk.shape[0]
    grid_spec = pltpu.PrefetchScalarGridSpec(
        num_scalar_prefetch=4,
        grid=(n_items,),
        in_specs=[
            pl.BlockSpec((ROW_BLOCK, width), lambda i, b, e, lo, hi: (b[i], 0)),
            pl.BlockSpec((1, D_MODEL, D_EXPERT), lambda i, b, e, lo, hi: (e[i], 0, 0)),
            pl.BlockSpec((1, D_MODEL, D_EXPERT), lambda i, b, e, lo, hi: (e[i], 0, 0)),
            pl.BlockSpec((1, D_EXPERT, D_MODEL), lambda i, b, e, lo, hi: (e[i], 0, 0)),
        ],
        out_specs=pl.BlockSpec((ROW_BLOCK, width), lambda i, b, e, lo, hi: (b[i], 0)),
        scratch_shapes=[
            pltpu.VMEM((D_MODEL, D_EXPERT), BF16),
            pltpu.VMEM((D_MODEL, D_EXPERT), BF16),
            pltpu.VMEM((D_EXPERT, D_MODEL), BF16),
        ],
    )
    return pl.pallas_call(
        _experts_kernel,
        grid_spec=grid_spec,
        out_shape=jax.ShapeDtypeStruct((n_rows, width), F32),
        compiler_params=_params(("arbitrary",)),
        name="experts",
    )(item_blk, item_exp, item_lo, item_hi, xs, w_gate, w_up, w_down)


def _combine_kernel(slot_ref, gw_ref, h1_ref, ys_ref, wsg_ref, wsu_ref, wsd_ref,
                    g2_ref, b2_ref, out_ref, buf_ref, sem):
    tm = h1_ref.shape[0]

    def issue(t, carry):
        for k in range(TOP_K):
            pltpu.make_async_copy(
                ys_ref.at[pl.ds(slot_ref[k * tm + t], 1), :], buf_ref.at[k, pl.ds(t, 1), :], sem).start()
        return carry

    lax.fori_loop(0, tm, issue, 0, unroll=ISSUE_UNROLL)

    h1 = h1_ref[...]
    hb = h1.astype(BF16)
    gate = _dot(hb, wsg_ref[...])
    hidden = (gate * _sigmoid(gate) * _dot(hb, wsu_ref[...])).astype(BF16)
    moe = _dot(hidden, wsd_ref[...])

    for k in range(TOP_K):
        pltpu.make_async_copy(ys_ref.at[pl.ds(0, tm), :], buf_ref.at[k], sem).wait()

    gw = gw_ref[...]
    for k in range(TOP_K):
        moe = moe + gw[:, k:k + 1] * buf_ref[k]
    out_ref[...] = _layer_norm(DN_ALPHA * h1 + moe, g2_ref[...], b2_ref[...])


def _combine(slots_tiled, gw, h1, ys, w_sg, w_su, w_sd, g2, b2):
    T = h1.shape[0]
    width = ys.shape[1]
    tm = TOKEN_TILE
    row = lambda i: (i, 0)
    return pl.pallas_call(
        _combine_kernel,
        grid=(T // tm,),
        in_specs=[
            pl.BlockSpec((TOP_K * tm,), lambda i: (i,), memory_space=pltpu.SMEM),
            pl.BlockSpec((tm, LANES), row),
            pl.BlockSpec((tm, D_MODEL), row),
            pl.BlockSpec(memory_space=pl.ANY),
            _const_spec((D_MODEL, D_EXPERT)),
            _const_spec((D_MODEL, D_EXPERT)),
            _const_spec((D_EXPERT, D_MODEL)),
            _const_spec((1, D_MODEL)),
            _const_spec((1, D_MODEL)),
        ],
        out_specs=pl.BlockSpec((tm, D_MODEL), row),
        out_shape=jax.ShapeDtypeStruct((T, D_MODEL), F32),
        scratch_shapes=[pltpu.VMEM((TOP_K, tm, width), F32), pltpu.SemaphoreType.DMA(())],
        compiler_params=_params(("arbitrary",)),
        name="combine",
    )(slots_tiled, gw, h1, ys, w_sg, w_su, w_sd, g2, b2)


def _expert_work_items(counts, n_rows):
    n_blocks = n_rows // ROW_BLOCK
    n_items = n_blocks + N_EXPERTS
    ends = jnp.cumsum(counts)
    starts = ends - counts
    first_blk = starts // ROW_BLOCK
    last_blk = jnp.maximum(ends - 1, 0) // ROW_BLOCK
    per_expert = jnp.where(counts > 0, last_blk - first_blk + 1, 0)
    item_end = jnp.cumsum(per_expert)
    item_start = item_end - per_expert
    total = item_end[-1]
    i = jnp.arange(n_items, dtype=jnp.int32)
    live = i < total
    j = jnp.minimum(i, total - 1)[:, None]
    owner = (item_start[None, :] <= j) & (j < item_end[None, :])

    def pick(per_expert_values):
        return jnp.sum(jnp.where(owner, per_expert_values[None, :], 0), axis=1)

    exp = pick(jnp.arange(N_EXPERTS, dtype=jnp.int32))
    blk = pick(first_blk - item_start) + j[:, 0]
    lo = jnp.maximum(pick(starts), blk * ROW_BLOCK)
    hi = jnp.minimum(pick(ends), (blk + 1) * ROW_BLOCK)
    lo = jnp.where(live, lo, 0)
    hi = jnp.where(live, hi, 0)
    as_i32 = lambda a: a.astype(jnp.int32)
    return as_i32(starts), as_i32(blk), as_i32(exp), as_i32(lo), as_i32(hi)


def kernel(x, positions, w_in, gmlp_ln_g, gmlp_ln_b, w_spatial, b_spatial, w_a_proj, q_norm_g, w_uq, kv_norm_g, w_uk, w_uv, w_b_proj, w_out, ln1_g, ln1_b, w_router, router_bias, w_gate, w_up, w_down, w_sh_gate, w_sh_up, w_sh_down, ln2_g, ln2_b):
    B, S, D = x.shape
    T = B * S
    x2 = x.reshape(T, D)

    w = w_in[0]
    o_u, o_v, o_cq = 0, A_WIDTH, 2 * A_WIDTH
    o_ckv = o_cq + Q_LORA
    o_kr = o_ckv + KV_LORA
    o_ga = o_kr + QK_ROPE
    o_gb = o_ga + D_MODEL
    half_r = QK_ROPE // 2
    w_kr = w[:, o_kr:o_ga]
    w_kr_partner = jnp.concatenate([w_kr[:, half_r:], w_kr[:, :half_r]], axis=1)
    w_b = jnp.concatenate(
        [w[:, o_cq:o_kr], w_kr, w_kr, w_kr_partner, w_kr_partner, w[:, o_gb:]], axis=1).astype(BF16)
    uq = w_uq[0]
    uq_rope = uq[:, :, QK_NOPE:]
    uq_partner = jnp.concatenate([uq_rope[:, :, half_r:], uq_rope[:, :, :half_r]], axis=2)
    w_uqt = jnp.concatenate([uq, uq_partner], axis=2).reshape(Q_LORA, N_HEADS * QK_PAD).T.astype(BF16)
    w_ukm = w_uk[0].reshape(KV_LORA, N_HEADS * QK_NOPE).astype(BF16)
    w_uvt = w_uv[0].reshape(KV_LORA, N_HEADS * V_HEAD).T.astype(BF16)

    freq = ROPE_THETA ** (-jnp.arange(0, half_r, dtype=F32) * 2.0 / QK_ROPE)
    freq128 = jnp.tile(freq, LANES // half_r)
    sign128 = jnp.tile(jnp.concatenate([-jnp.ones((half_r,), F32), jnp.ones((half_r,), F32)]), LANES // QK_ROPE)
    pos_f = positions.astype(F32).reshape(T)
    bias_full = jnp.repeat(b_spatial[0].T, A_GROUP_DIM, axis=1)

    ga = _gmlp_branch(
        x2, w[:, o_u:o_v].astype(BF16), w[:, o_v:o_cq].astype(BF16), w[:, o_ga:o_gb].astype(BF16),
        gmlp_ln_g[0].reshape(1, A_WIDTH), gmlp_ln_b[0].reshape(1, A_WIDTH),
        w_spatial[0].astype(BF16), bias_full, w_a_proj[0].astype(BF16))

    q_t, k, v_t, sgb = _mla_prep(
        x2, pos_f.reshape(T // TOKEN_TILE, 1, TOKEN_TILE), pos_f.reshape(T, 1), w_b,
        q_norm_g[0].reshape(1, Q_LORA), kv_norm_g[0].reshape(1, KV_LORA), w_uqt, w_ukm, w_uvt,
        freq128.reshape(LANES, 1), freq128.reshape(1, LANES), sign128.reshape(LANES, 1), sign128.reshape(1, LANES),
        B, S)

    o = _attention(q_t, k, v_t).reshape(T, N_HEADS * V_HEAD)

    h1, idx_t, rank_t, gw, cnt = _mix_route(
        x2, ga, sgb, o, w_b_proj[0].astype(BF16), w_out[0].astype(BF16),
        ln1_g[0].reshape(1, D), ln1_b[0].reshape(1, D), w_router[0].astype(BF16),
        router_bias[0].reshape(N_EXPERTS, 1))

    counts = cnt[:, 0].astype(jnp.int32)
    starts, item_blk, item_exp, item_lo, item_hi = _expert_work_items(counts, T * TOP_K)

    slots = _slot_index(starts, idx_t, rank_t)
    slots_tiled = slots.reshape(TOP_K, T // TOKEN_TILE, TOKEN_TILE).transpose(1, 0, 2).reshape(T * TOP_K)

    xs = _dispatch(slots_tiled, h1)
    ys = _experts(item_blk, item_exp, item_lo, item_hi, xs, w_gate[0], w_up[0], w_down[0])
    out = _combine(
        slots_tiled, gw, h1, ys, w_sh_gate[0].astype(BF16), w_sh_up[0].astype(BF16),
        w_sh_down[0].astype(BF16), ln2_g[0].reshape(1, D), ln2_b[0].reshape(1, D))
    return out.reshape(B, S, D)
```

```python
import functools
import math

import jax
import jax.numpy as jnp
from jax import lax
from jax.experimental import pallas as pl
from jax.experimental.pallas import tpu as pltpu

D_MODEL = 1024
CHUNK = 128
A_GROUPS = 8
A_GROUP_DIM = 128
A_WIDTH = A_GROUPS * A_GROUP_DIM
N_HEADS = 8
QK_NOPE = 128
QK_ROPE = 64
QK_DIM = QK_NOPE + QK_ROPE
V_HEAD = 128
Q_LORA = 384
KV_LORA = 256
ROPE_THETA = 10000.0
N_EXPERTS = 256
TOP_K = 8
N_GROUPS = 8
TOPK_GROUPS = 4
GROUP_SIZE = N_EXPERTS // N_GROUPS
D_EXPERT = 256
ROUTED_SCALE = 2.5
DN_ALPHA = 2.0 ** 0.25
LN_EPS = 1e-5
RMS_EPS = 1e-6

QK_PAD = 256
LANES = 128
TOKEN_TILE = 256
Q_TILE = 1024
KV_TILE = 512
KV_GROUP = 4
SLOT_LANES = 2048
ISSUE_UNROLL = 4
ROW_BLOCK = 256
SUBLANES = 8
VMEM_LIMIT = 56 * 1024 * 1024

F32 = jnp.float32
BF16 = jnp.bfloat16
NEG_INF = float("-inf")


def _dot(a, b):
    return jnp.dot(a, b, preferred_element_type=F32)


def _sigmoid(x):
    return 1.0 / (1.0 + jnp.exp(-x))


def _gelu_tanh(x):
    return 0.5 * x * (1.0 + jnp.tanh(math.sqrt(2.0 / math.pi) * (x + 0.044715 * (x * x * x))))


def _layer_norm(x, g, b):
    mu = jnp.mean(x, axis=-1, keepdims=True)
    d = x - mu
    var = jnp.mean(d * d, axis=-1, keepdims=True)
    return d * lax.rsqrt(var + LN_EPS) * g + b


def _rms_norm(x, g):
    return x * lax.rsqrt(jnp.mean(x * x, axis=-1, keepdims=True) + RMS_EPS) * g


def _tile_rows_load(ref, first, n):
    return jnp.concatenate(
        [ref[pl.ds(first * SUBLANES + j, n, stride=SUBLANES), :] for j in range(SUBLANES)], axis=1)


def _tile_rows_store(ref, first, n, value):
    for j in range(SUBLANES):
        ref[pl.ds(first * SUBLANES + j, n, stride=SUBLANES), :] = value[:, j * LANES:(j + 1) * LANES]


def _const_spec(shape):
    zeros = (0,) * len(shape)
    return pl.BlockSpec(shape, lambda *_: zeros)


def _params(semantics):
    return pltpu.CompilerParams(dimension_semantics=semantics, vmem_limit_bytes=VMEM_LIMIT)


def _gmlp_kernel(x_ref, wu_ref, wv_ref, wg_ref, lng_ref, lnb_ref, ws_ref, bias_ref, wa_ref, o_ref, mix_ref):
    tm = x_ref.shape[0]
    xb = x_ref[...].astype(BF16)
    v = _gelu_tanh(_dot(xb, wv_ref[...]))
    for h in range(A_GROUPS):
        cols = slice(h * A_GROUP_DIM, (h + 1) * A_GROUP_DIM)
        vn = _layer_norm(v[:, cols], lng_ref[:, cols], lnb_ref[:, cols]).astype(BF16)
        for c in range(tm // CHUNK):
            rows = slice(c * CHUNK, (c + 1) * CHUNK)
            mix_ref[rows, cols] = _dot(ws_ref[h], vn[rows, :]) + bias_ref[:, cols]
    u = _gelu_tanh(_dot(xb, wu_ref[...]))
    ya = _dot((u * mix_ref[...]).astype(BF16), wa_ref[...])
    gate = _sigmoid(_dot(xb, wg_ref[...]))
    o_ref[...] = (gate * ya).astype(BF16)


def _gmlp_branch(x2, w_u, w_v, w_ga, ln_g, ln_b, w_s, bias_full, w_a):
    T = x2.shape[0]
    tm = TOKEN_TILE
    return pl.pallas_call(
        _gmlp_kernel,
        grid=(T // tm,),
        in_specs=[
            pl.BlockSpec((tm, D_MODEL), lambda i: (i, 0)),
            _const_spec((D_MODEL, A_WIDTH)),
            _const_spec((D_MODEL, A_WIDTH)),
            _const_spec((D_MODEL, D_MODEL)),
            _const_spec((1, A_WIDTH)),
            _const_spec((1, A_WIDTH)),
            _const_spec((A_GROUPS, CHUNK, CHUNK)),
            _const_spec((CHUNK, A_WIDTH)),
            _const_spec((A_WIDTH, D_MODEL)),
        ],
        out_specs=pl.BlockSpec((tm, D_MODEL), lambda i: (i, 0)),
        out_shape=jax.ShapeDtypeStruct((T, D_MODEL), BF16),
        scratch_shapes=[pltpu.VMEM((tm, A_WIDTH), F32)],
        compiler_params=_params(("arbitrary",)),
        name="gmlp_branch",
    )(x2, w_u, w_v, w_ga, ln_g, ln_b, w_s, bias_full, w_a)


def _mla_prep_kernel(x_ref, posr_ref, posc_ref, wb_ref, qg_ref, kvg_ref, wuqt_ref, wuk_ref, wuvt_ref,
                     freqc_ref, freqr_ref, signc_ref, signr_ref,
                     qt_ref, k_ref, vt_ref, sgb_ref):
    xb = x_ref[...].astype(BF16)
    pb = _dot(xb, wb_ref[...])
    c_q = pb[:, :Q_LORA]
    c_kv = pb[:, Q_LORA:Q_LORA + KV_LORA]
    o = Q_LORA + KV_LORA
    kr = pb[:, o:o + LANES]
    kr_partner = pb[:, o + LANES:o + 2 * LANES]
    gb = pb[:, o + 2 * LANES:]
    sgb_ref[...] = _sigmoid(gb).astype(BF16)

    cqn = _rms_norm(c_q, qg_ref[...])
    ckvn = _rms_norm(c_kv, kvg_ref[...])
    cqn_t = cqn.T.astype(BF16)
    ckvn_t = ckvn.T.astype(BF16)

    scale = math.log2(math.e) / math.sqrt(QK_DIM)
    ang_t = freqc_ref[...] * posr_ref[0]
    table_t = jnp.concatenate(
        [jnp.cos(ang_t[:QK_ROPE]), signc_ref[QK_ROPE:] * jnp.sin(ang_t[QK_ROPE:])], axis=0) * scale
    q_t = _dot(wuqt_ref[...], cqn_t)
    for h in range(N_HEADS):
        blk = q_t[h * QK_PAD:(h + 1) * QK_PAD]
        qt_ref[0, h] = jnp.concatenate(
            [blk[:QK_NOPE] * scale, blk[QK_NOPE:] * table_t], axis=0).astype(BF16)

    v_t = _dot(wuvt_ref[...], ckvn_t)
    for h in range(N_HEADS):
        vt_ref[0, h] = v_t[h * V_HEAD:(h + 1) * V_HEAD].astype(BF16)

    ang = posc_ref[...] * freqr_ref[...]
    k_rope = (kr * jnp.cos(ang) + kr_partner * (signr_ref[...] * jnp.sin(ang))).astype(BF16)
    k_nope = _dot(ckvn.astype(BF16), wuk_ref[...])
    for h in range(N_HEADS):
        k_ref[0, h] = jnp.concatenate(
            [k_nope[:, h * QK_NOPE:(h + 1) * QK_NOPE].astype(BF16), k_rope], axis=1)


def _mla_prep(x2, pos_row, pos_col, w_b, q_g, kv_g, w_uqt, w_uk, w_uvt, freq_col, freq_row, sign_col, sign_row,
              B, S):
    T = x2.shape[0]
    tm = TOKEN_TILE
    spb = S // tm
    n_b = w_b.shape[1]
    head_map = lambda i: (i // spb, 0, 0, i % spb)
    return pl.pallas_call(
        _mla_prep_kernel,
        grid=(T // tm,),
        in_specs=[
            pl.BlockSpec((tm, D_MODEL), lambda i: (i, 0)),
            pl.BlockSpec((1, 1, tm), lambda i: (i, 0, 0)),
            pl.BlockSpec((tm, 1), lambda i: (i, 0)),
            _const_spec((D_MODEL, n_b)),
            _const_spec((1, Q_LORA)),
            _const_spec((1, KV_LORA)),
            _const_spec((N_HEADS * QK_PAD, Q_LORA)),
            _const_spec((KV_LORA, N_HEADS * QK_NOPE)),
            _const_spec((N_HEADS * V_HEAD, KV_LORA)),
            _const_spec((LANES, 1)),
            _const_spec((1, LANES)),
            _const_spec((LANES, 1)),
            _const_spec((1, LANES)),
        ],
        out_specs=[
            pl.BlockSpec((1, N_HEADS, QK_PAD, tm), head_map),
            pl.BlockSpec((1, N_HEADS, tm, QK_PAD), lambda i: (i // spb, 0, i % spb, 0)),
            pl.BlockSpec((1, N_HEADS, V_HEAD, tm), head_map),
            pl.BlockSpec((tm, D_MODEL), lambda i: (i, 0)),
        ],
        out_shape=[
            jax.ShapeDtypeStruct((B, N_HEADS, QK_PAD, S), BF16),
            jax.ShapeDtypeStruct((B, N_HEADS, S, QK_PAD), BF16),
            jax.ShapeDtypeStruct((B, N_HEADS, V_HEAD, S), BF16),
            jax.ShapeDtypeStruct((T, D_MODEL), BF16),
        ],
        compiler_params=_params(("arbitrary",)),
        name="mla_prep",
    )(x2, pos_row, pos_col, w_b, q_g, kv_g, w_uqt, w_uk, w_uvt, freq_col, freq_row, sign_col, sign_row)


def _attention_kernel(qt_ref, k_ref, vt_ref, o_ref, s0_ref, s1_ref, m_ref, l_ref, acc_ref):
    S = k_ref.shape[2]
    n_chunks = S // KV_TILE
    q_t = qt_ref[0, 0]

    def scores(c, dst_ref):
        start = pl.multiple_of(c * KV_TILE, KV_TILE)
        dst_ref[...] = _dot(k_ref[0, 0, pl.ds(start, KV_TILE), :], q_t)

    def accumulate(c, src_ref):
        start = pl.multiple_of(c * KV_TILE, KV_TILE)
        s_t = src_ref[...]
        m = m_ref[...]
        m_new = jnp.maximum(m, jnp.max(s_t, axis=0, keepdims=True))
        p_t = jnp.exp2(s_t - m_new)
        alpha = jnp.exp2(m - m_new)
        m_ref[...] = m_new
        l_ref[...] = alpha * l_ref[...] + jnp.sum(p_t, axis=0, keepdims=True)
        acc_ref[...] = alpha * acc_ref[...] + _dot(vt_ref[0, 0, :, pl.ds(start, KV_TILE)], p_t.astype(BF16))

    m_ref[...] = jnp.full(m_ref.shape, NEG_INF, F32)
    l_ref[...] = jnp.zeros(l_ref.shape, F32)
    acc_ref[...] = jnp.zeros(acc_ref.shape, F32)
    scores(0, s0_ref)

    bufs = (s0_ref, s1_ref)

    def group(c0, prefetch_last):
        for i in range(KV_GROUP):
            if i + 1 < KV_GROUP or prefetch_last:
                scores(c0 + i + 1, bufs[(i + 1) % 2])
            accumulate(c0 + i, bufs[i % 2])

    def body(g, carry):
        group(g * KV_GROUP, True)
        return carry

    lax.fori_loop(0, n_chunks // KV_GROUP - 1, body, 0)
    group(n_chunks - KV_GROUP, False)
    o_ref[0] = (acc_ref[...] / l_ref[...]).T.astype(BF16)


def _attention(q_t, k, v_t):
    B, H, _, S = q_t.shape
    tq = min(Q_TILE, S)
    assert S % tq == 0 and S % (KV_TILE * KV_GROUP) == 0 and KV_GROUP % 2 == 0
    return pl.pallas_call(
        _attention_kernel,
        grid=(B, H, S // tq),
        in_specs=[
            pl.BlockSpec((1, 1, QK_PAD, tq), lambda b, h, i: (b, h, 0, i)),
            pl.BlockSpec((1, 1, S, QK_PAD), lambda b, h, i: (b, h, 0, 0)),
            pl.BlockSpec((1, 1, V_HEAD, S), lambda b, h, i: (b, h, 0, 0)),
        ],
        out_specs=pl.BlockSpec((1, tq, V_HEAD), lambda b, h, i: (b, i, h)),
        out_shape=jax.ShapeDtypeStruct((B, S, H * V_HEAD), BF16),
        scratch_shapes=[
            pltpu.VMEM((KV_TILE, tq), F32),
            pltpu.VMEM((KV_TILE, tq), F32),
            pltpu.VMEM((1, tq), F32),
            pltpu.VMEM((1, tq), F32),
            pltpu.VMEM((V_HEAD, tq), F32),
        ],
        compiler_params=_params(("arbitrary", "arbitrary", "arbitrary")),
        name="attention",
    )(q_t, k, v_t)


def _mix_route_kernel(x_ref, ga_ref, sgb_ref, o_ref, wbp_ref, wout_ref, g1_ref, b1_ref, wr_ref, rb_ref,
                      h1_ref, h1t_ref, idx_ref, rank_ref, gw_ref, cnt_ref, base_ref):
    tm = x_ref.shape[0]

    @pl.when(pl.program_id(0) == 0)
    def _():
        base_ref[...] = jnp.zeros_like(base_ref)

    yb = _dot(o_ref[...], wbp_ref[...])
    merged = ga_ref[...].astype(F32) + sgb_ref[...].astype(F32) * yb
    mixed = _dot(merged.astype(BF16), wout_ref[...])
    h1 = _layer_norm(DN_ALPHA * x_ref[...] + mixed, g1_ref[...], b1_ref[...])
    h1_ref[...] = h1
    _tile_rows_store(h1t_ref, 0, tm, h1)

    scores_t = _sigmoid(_dot(h1.astype(BF16), wr_ref[...])).T
    biased = scores_t + rb_ref[...]

    riota = lax.broadcasted_iota(jnp.int32, (GROUP_SIZE, tm), 0)
    blocks, gscore = [], []
    for g in range(N_GROUPS):
        blk = biased[g * GROUP_SIZE:(g + 1) * GROUP_SIZE]
        m1 = jnp.max(blk, axis=0, keepdims=True)
        i1 = jnp.min(jnp.where(blk == m1, riota, GROUP_SIZE), axis=0, keepdims=True)
        m2 = jnp.max(jnp.where(riota == i1, NEG_INF, blk), axis=0, keepdims=True)
        blocks.append(blk)
        gscore.append(m1 + m2)

    selected = [jnp.zeros((1, tm), F32) for _ in range(N_GROUPS)]
    for _ in range(TOPK_GROUPS):
        best = functools.reduce(jnp.maximum, gscore)
        first = functools.reduce(
            jnp.minimum, [jnp.where(gscore[g] == best, g, N_GROUPS) for g in range(N_GROUPS)])
        for g in range(N_GROUPS):
            hit = first == g
            selected[g] = jnp.where(hit, 1.0, selected[g])
            gscore[g] = jnp.where(hit, NEG_INF, gscore[g])

    masked = jnp.concatenate(
        [jnp.where(jnp.broadcast_to(selected[g], blocks[g].shape) > 0.5, blocks[g], NEG_INF)
         for g in range(N_GROUPS)], axis=0)

    eiota = lax.broadcasted_iota(jnp.int32, (N_EXPERTS, tm), 0)
    picks, weights = [], []
    for _ in range(TOP_K):
        best = jnp.max(masked, axis=0, keepdims=True)
        e = jnp.min(jnp.where(masked == best, eiota, N_EXPERTS), axis=0, keepdims=True)
        hit = eiota == e
        masked = jnp.where(hit, NEG_INF, masked)
        picks.append(e)
        weights.append(jnp.sum(jnp.where(hit, scores_t, 0.0), axis=0, keepdims=True))
    wsum = functools.reduce(lambda a, b: a + b, weights)

    chosen = functools.reduce(
        lambda a, b: a + b, [jnp.where(eiota == e, 1.0, 0.0) for e in picks])
    before = (lax.broadcasted_iota(jnp.int32, (tm, tm), 0)
              < lax.broadcasted_iota(jnp.int32, (tm, tm), 1)).astype(BF16)
    rank_full = _dot(chosen.astype(BF16), before) + base_ref[:, :1]
    new_base = base_ref[:, :1] + jnp.sum(chosen, axis=1, keepdims=True)
    base_ref[...] = jnp.broadcast_to(new_base, base_ref.shape)
    cnt_ref[...] = jnp.broadcast_to(new_base, cnt_ref.shape)

    kiota = lax.broadcasted_iota(jnp.int32, (TOP_K, tm), 0)
    wiota = lax.broadcasted_iota(jnp.int32, (LANES, tm), 0)
    idx_out = jnp.zeros((TOP_K, tm), jnp.int32)
    rank_out = jnp.zeros((TOP_K, tm), jnp.int32)
    gw_t = jnp.zeros((LANES, tm), F32)
    for k in range(TOP_K):
        r = jnp.sum(jnp.where(eiota == picks[k], rank_full, 0.0), axis=0, keepdims=True)
        idx_out = jnp.where(kiota == k, picks[k], idx_out)
        rank_out = jnp.where(kiota == k, r.astype(jnp.int32), rank_out)
        gw_t = jnp.where(wiota == k, weights[k] / wsum * ROUTED_SCALE, gw_t)
    idx_ref[...] = idx_out
    rank_ref[...] = rank_out
    gw_ref[...] = gw_t.T


def _mix_route(x2, ga, sgb, o, w_bp, w_out, g1, b1, w_r, r_bias):
    T = x2.shape[0]
    tm = TOKEN_TILE
    row = lambda i: (i, 0)
    col = lambda i: (0, i)
    return pl.pallas_call(
        _mix_route_kernel,
        grid=(T // tm,),
        in_specs=[
            pl.BlockSpec((tm, D_MODEL), row),
            pl.BlockSpec((tm, D_MODEL), row),
            pl.BlockSpec((tm, D_MODEL), row),
            pl.BlockSpec((tm, D_MODEL), row),
            _const_spec((D_MODEL, D_MODEL)),
            _const_spec((D_MODEL, D_MODEL)),
            _const_spec((1, D_MODEL)),
            _const_spec((1, D_MODEL)),
            _const_spec((D_MODEL, N_EXPERTS)),
            _const_spec((N_EXPERTS, 1)),
        ],
        out_specs=[
            pl.BlockSpec((tm, D_MODEL), row),
            pl.BlockSpec((tm * SUBLANES, LANES), row),
            pl.BlockSpec((TOP_K, tm), col),
            pl.BlockSpec((TOP_K, tm), col),
            pl.BlockSpec((tm, LANES), row),
            _const_spec((N_EXPERTS, LANES)),
        ],
        out_shape=[
            jax.ShapeDtypeStruct((T, D_MODEL), F32),
            jax.ShapeDtypeStruct((T * SUBLANES, LANES), F32),
            jax.ShapeDtypeStruct((TOP_K, T), jnp.int32),
            jax.ShapeDtypeStruct((TOP_K, T), jnp.int32),
            jax.ShapeDtypeStruct((T, LANES), F32),
            jax.ShapeDtypeStruct((N_EXPERTS, LANES), F32),
        ],
        scratch_shapes=[pltpu.VMEM((N_EXPERTS, LANES), F32)],
        compiler_params=_params(("arbitrary",)),
        name="mix_route",
    )(x2, ga, sgb, o, w_bp, w_out, g1, b1, w_r, r_bias)


def _slot_kernel(starts_ref, idx_ref, rank_ref, slot_ref):
    idx = idx_ref[...]

    def lookup(e, acc):
        return jnp.where(idx == e, starts_ref[e], acc)

    slot_ref[...] = lax.fori_loop(0, N_EXPERTS, lookup, jnp.zeros_like(idx), unroll=8) + rank_ref[...]


def _slot_index(starts, idx_t, rank_t):
    T = idx_t.shape[1]
    tl = min(SLOT_LANES, T)
    col = lambda i: (0, i)
    return pl.pallas_call(
        _slot_kernel,
        grid=(T // tl,),
        in_specs=[
            pl.BlockSpec(memory_space=pltpu.SMEM),
            pl.BlockSpec((TOP_K, tl), col),
            pl.BlockSpec((TOP_K, tl), col),
        ],
        out_specs=pl.BlockSpec((TOP_K, tl), col),
        out_shape=jax.ShapeDtypeStruct((TOP_K, T), jnp.int32),
        compiler_params=_params(("arbitrary",)),
        name="slot_index",
    )(starts, idx_t, rank_t)


def _token_tile(ref, token):
    return ref.at[pl.ds(pl.multiple_of(token * SUBLANES, SUBLANES), SUBLANES), :]


def _dispatch_kernel(slot_ref, h1t_ref, xs_ref, sem):
    tm = h1t_ref.shape[0] // SUBLANES

    def issue(t, carry):
        src = _token_tile(h1t_ref, t)
        for k in range(TOP_K):
            pltpu.make_async_copy(src, _token_tile(xs_ref, slot_ref[k * tm + t]), sem).start()
        return carry

    lax.fori_loop(0, tm, issue, 0, unroll=ISSUE_UNROLL)
    for _ in range(TOP_K):
        pltpu.make_async_copy(h1t_ref, xs_ref.at[pl.ds(0, tm * SUBLANES), :], sem).wait()


def _dispatch(slots_tiled, h1t):
    T = h1t.shape[0] // SUBLANES
    tm = TOKEN_TILE
    return pl.pallas_call(
        _dispatch_kernel,
        grid=(T // tm,),
        in_specs=[
            pl.BlockSpec((TOP_K * tm,), lambda i: (i,), memory_space=pltpu.SMEM),
            pl.BlockSpec((tm * SUBLANES, LANES), lambda i: (i, 0)),
        ],
        out_specs=pl.BlockSpec(memory_space=pl.ANY),
        out_shape=jax.ShapeDtypeStruct((T * TOP_K * SUBLANES, LANES), F32),
        scratch_shapes=[pltpu.SemaphoreType.DMA(())],
        compiler_params=_params(("arbitrary",)),
        name="dispatch",
    )(slots_tiled, h1t)


def _weight_copies(expert, slot, w_hbm_refs, w_buf_refs, sems):
    return [pltpu.make_async_copy(w_hbm.at[expert], w_buf.at[slot], sems.at[slot, n])
            for n, (w_hbm, w_buf) in enumerate(zip(w_hbm_refs, w_buf_refs))]


def _experts_kernel(blk_ref, exp_ref, lo_ref, hi_ref, next_ref, slot_ref,
                    xs_ref, wg_hbm, wu_hbm, wd_hbm, ys_ref,
                    wg_buf, wu_buf, wd_buf, wg_bf_ref, wu_bf_ref, wd_bf_ref, sems):
    i = pl.program_id(0)
    blk = blk_ref[i]
    lo = lo_ref[i]
    hi = hi_ref[i]
    expert = exp_ref[i]
    slot = slot_ref[i]
    prev = jnp.maximum(i - 1, 0)
    w_hbm_refs = (wg_hbm, wu_hbm, wd_hbm)
    w_buf_refs = (wg_buf, wu_buf, wd_buf)

    @pl.when(i == 0)
    def _():
        for copy in _weight_copies(expert, slot, w_hbm_refs, w_buf_refs, sems):
            copy.start()

    @pl.when((i == 0) | (blk_ref[prev] != blk))
    def _():
        ys_ref[...] = jnp.zeros_like(ys_ref)

    @pl.when((i == 0) | (exp_ref[prev] != expert))
    def _():
        following = next_ref[i]

        @pl.when(following < N_EXPERTS)
        def _():
            for copy in _weight_copies(following, 1 - slot, w_hbm_refs, w_buf_refs, sems):
                copy.start()

        for copy in _weight_copies(expert, slot, w_hbm_refs, w_buf_refs, sems):
            copy.wait()
        wg_bf_ref[...] = wg_buf[slot].astype(BF16)
        wu_bf_ref[...] = wu_buf[slot].astype(BF16)
        wd_bf_ref[...] = wd_buf[slot].astype(BF16)

    @pl.when(hi > lo)
    def _():
        x = _tile_rows_load(xs_ref, 0, ROW_BLOCK).astype(BF16)
        gate = _dot(x, wg_bf_ref[...])
        up = _dot(x, wu_bf_ref[...])
        hidden = (gate * _sigmoid(gate) * up).astype(BF16)
        y = _dot(hidden, wd_bf_ref[...])
        rows = blk * ROW_BLOCK + lax.broadcasted_iota(jnp.int32, (ROW_BLOCK, 1), 0)
        keep = (rows >= lo) & (rows < hi)
        _tile_rows_store(ys_ref, 0, ROW_BLOCK, jnp.where(keep, y, _tile_rows_load(ys_ref, 0, ROW_BLOCK)))


def _experts(item_blk, item_exp, item_lo, item_hi, item_next, item_slot, xs, w_gate, w_up, w_down):
    n_rows = xs.shape[0] // SUBLANES
    n_items = item_blk.shape[0]
    block = (ROW_BLOCK * SUBLANES, LANES)
    grid_spec = pltpu.PrefetchScalarGridSpec(
        num_scalar_prefetch=6,
        grid=(n_items,),
        in_specs=[
            pl.BlockSpec(block, lambda i, b, *_: (b[i], 0)),
            pl.BlockSpec(memory_space=pl.ANY),
            pl.BlockSpec(memory_space=pl.ANY),
            pl.BlockSpec(memory_space=pl.ANY),
        ],
        out_specs=pl.BlockSpec(block, lambda i, b, *_: (b[i], 0)),
        scratch_shapes=[
            pltpu.VMEM((2, D_MODEL, D_EXPERT), F32),
            pltpu.VMEM((2, D_MODEL, D_EXPERT), F32),
            pltpu.VMEM((2, D_EXPERT, D_MODEL), F32),
            pltpu.VMEM((D_MODEL, D_EXPERT), BF16),
            pltpu.VMEM((D_MODEL, D_EXPERT), BF16),
            pltpu.VMEM((D_EXPERT, D_MODEL), BF16),
            pltpu.SemaphoreType.DMA((2, 3)),
        ],
    )
    return pl.pallas_call(
        _experts_kernel,
        grid_spec=grid_spec,
        out_shape=jax.ShapeDtypeStruct((n_rows * SUBLANES, LANES), F32),
        compiler_params=_params(("arbitrary",)),
        name="experts",
    )(item_blk, item_exp, item_lo, item_hi, item_next, item_slot, xs, w_gate, w_up, w_down)


def _combine_kernel(slot_ref, gw_ref, h1_ref, ys_ref, wsg_ref, wsu_ref, wsd_ref,
                    g2_ref, b2_ref, out_ref, buf_ref, sem):
    tm = h1_ref.shape[0]

    def issue(t, carry):
        for k in range(TOP_K):
            pltpu.make_async_copy(
                _token_tile(ys_ref, slot_ref[k * tm + t]), _token_tile(buf_ref, k * tm + t), sem).start()
        return carry

    lax.fori_loop(0, tm, issue, 0, unroll=ISSUE_UNROLL)

    h1 = h1_ref[...]
    hb = h1.astype(BF16)
    gate = _dot(hb, wsg_ref[...])
    hidden = (gate * _sigmoid(gate) * _dot(hb, wsu_ref[...])).astype(BF16)
    moe = _dot(hidden, wsd_ref[...])

    for k in range(TOP_K):
        pltpu.make_async_copy(
            ys_ref.at[pl.ds(0, tm * SUBLANES), :], buf_ref.at[pl.ds(k * tm * SUBLANES, tm * SUBLANES), :], sem).wait()

    gw = gw_ref[...]
    for k in range(TOP_K):
        moe = moe + gw[:, k:k + 1] * _tile_rows_load(buf_ref, k * tm, tm)
    out_ref[...] = _layer_norm(DN_ALPHA * h1 + moe, g2_ref[...], b2_ref[...])


def _combine(slots_tiled, gw, h1, ys, w_sg, w_su, w_sd, g2, b2):
    T = h1.shape[0]
    tm = TOKEN_TILE
    row = lambda i: (i, 0)
    return pl.pallas_call(
        _combine_kernel,
        grid=(T // tm,),
        in_specs=[
            pl.BlockSpec((TOP_K * tm,), lambda i: (i,), memory_space=pltpu.SMEM),
            pl.BlockSpec((tm, LANES), row),
            pl.BlockSpec((tm, D_MODEL), row),
            pl.BlockSpec(memory_space=pl.ANY),
            _const_spec((D_MODEL, D_EXPERT)),
            _const_spec((D_MODEL, D_EXPERT)),
            _const_spec((D_EXPERT, D_MODEL)),
            _const_spec((1, D_MODEL)),
            _const_spec((1, D_MODEL)),
        ],
        out_specs=pl.BlockSpec((tm, D_MODEL), row),
        out_shape=jax.ShapeDtypeStruct((T, D_MODEL), F32),
        scratch_shapes=[pltpu.VMEM((TOP_K * tm * SUBLANES, LANES), F32), pltpu.SemaphoreType.DMA(())],
        compiler_params=_params(("arbitrary",)),
        name="combine",
    )(slots_tiled, gw, h1, ys, w_sg, w_su, w_sd, g2, b2)


def _expert_work_items(counts, n_rows):
    n_blocks = n_rows // ROW_BLOCK
    n_items = n_blocks + N_EXPERTS
    ends = jnp.cumsum(counts)
    starts = ends - counts
    first_blk = starts // ROW_BLOCK
    last_blk = jnp.maximum(ends - 1, 0) // ROW_BLOCK
    per_expert = jnp.where(counts > 0, last_blk - first_blk + 1, 0)
    item_end = jnp.cumsum(per_expert)
    item_start = item_end - per_expert
    total = item_end[-1]
    i = jnp.arange(n_items, dtype=jnp.int32)
    live = i < total
    j = jnp.minimum(i, total - 1)[:, None]
    owner = (item_start[None, :] <= j) & (j < item_end[None, :])

    def pick(per_expert_values):
        return jnp.sum(jnp.where(owner, per_expert_values[None, :], 0), axis=1)

    expert_ids = jnp.arange(N_EXPERTS, dtype=jnp.int32)
    nonempty = counts > 0
    later = (expert_ids[None, :] > expert_ids[:, None]) & nonempty[None, :]
    following = jnp.min(jnp.where(later, expert_ids[None, :], N_EXPERTS), axis=1)
    buffer_slot = (jnp.cumsum(nonempty) - 1) % 2
    exp = pick(expert_ids)
    blk = pick(first_blk - item_start) + j[:, 0]
    lo = jnp.maximum(pick(starts), blk * ROW_BLOCK)
    hi = jnp.minimum(pick(ends), (blk + 1) * ROW_BLOCK)
    lo = jnp.where(live, lo, 0)
    hi = jnp.where(live, hi, 0)
    as_i32 = lambda a: a.astype(jnp.int32)
    return (as_i32(starts), as_i32(blk), as_i32(exp), as_i32(lo), as_i32(hi),
            as_i32(pick(following)), as_i32(pick(buffer_slot)))


def kernel(x, positions, w_in, gmlp_ln_g, gmlp_ln_b, w_spatial, b_spatial, w_a_proj, q_norm_g, w_uq, kv_norm_g, w_uk, w_uv, w_b_proj, w_out, ln1_g, ln1_b, w_router, router_bias, w_gate, w_up, w_down, w_sh_gate, w_sh_up, w_sh_down, ln2_g, ln2_b):
    B, S, D = x.shape
    T = B * S
    x2 = x.reshape(T, D)

    w = w_in[0]
    o_u, o_v, o_cq = 0, A_WIDTH, 2 * A_WIDTH
    o_ckv = o_cq + Q_LORA
    o_kr = o_ckv + KV_LORA
    o_ga = o_kr + QK_ROPE
    o_gb = o_ga + D_MODEL
    half_r = QK_ROPE // 2
    w_kr = w[:, o_kr:o_ga]
    w_kr_partner = jnp.concatenate([w_kr[:, half_r:], w_kr[:, :half_r]], axis=1)
    w_b = jnp.concatenate(
        [w[:, o_cq:o_kr], w_kr, w_kr, w_kr_partner, w_kr_partner, w[:, o_gb:]], axis=1).astype(BF16)
    uq = w_uq[0]
    uq_rope = uq[:, :, QK_NOPE:]
    uq_partner = jnp.concatenate([uq_rope[:, :, half_r:], uq_rope[:, :, :half_r]], axis=2)
    w_uqt = jnp.concatenate([uq, uq_partner], axis=2).reshape(Q_LORA, N_HEADS * QK_PAD).T.astype(BF16)
    w_ukm = w_uk[0].reshape(KV_LORA, N_HEADS * QK_NOPE).astype(BF16)
    w_uvt = w_uv[0].reshape(KV_LORA, N_HEADS * V_HEAD).T.astype(BF16)

    freq = ROPE_THETA ** (-jnp.arange(0, half_r, dtype=F32) * 2.0 / QK_ROPE)
    freq128 = jnp.tile(freq, LANES // half_r)
    sign128 = jnp.tile(jnp.concatenate([-jnp.ones((half_r,), F32), jnp.ones((half_r,), F32)]), LANES // QK_ROPE)
    pos_f = positions.astype(F32).reshape(T)
    bias_full = jnp.repeat(b_spatial[0].T, A_GROUP_DIM, axis=1)

    ga = _gmlp_branch(
        x2, w[:, o_u:o_v].astype(BF16), w[:, o_v:o_cq].astype(BF16), w[:, o_ga:o_gb].astype(BF16),
        gmlp_ln_g[0].reshape(1, A_WIDTH), gmlp_ln_b[0].reshape(1, A_WIDTH),
        w_spatial[0].astype(BF16), bias_full, w_a_proj[0].astype(BF16))

    q_t, k, v_t, sgb = _mla_prep(
        x2, pos_f.reshape(T // TOKEN_TILE, 1, TOKEN_TILE), pos_f.reshape(T, 1), w_b,
        q_norm_g[0].reshape(1, Q_LORA), kv_norm_g[0].reshape(1, KV_LORA), w_uqt, w_ukm, w_uvt,
        freq128.reshape(LANES, 1), freq128.reshape(1, LANES), sign128.reshape(LANES, 1), sign128.reshape(1, LANES),
        B, S)

    o = _attention(q_t, k, v_t).reshape(T, N_HEADS * V_HEAD)

    h1, h1t, idx_t, rank_t, gw, cnt = _mix_route(
        x2, ga, sgb, o, w_b_proj[0].astype(BF16), w_out[0].astype(BF16),
        ln1_g[0].reshape(1, D), ln1_b[0].reshape(1, D), w_router[0].astype(BF16),
        router_bias[0].reshape(N_EXPERTS, 1))

    counts = cnt[:, 0].astype(jnp.int32)
    starts, item_blk, item_exp, item_lo, item_hi, item_next, item_slot = _expert_work_items(counts, T * TOP_K)

    slots = _slot_index(starts, idx_t, rank_t)
    slots_tiled = slots.reshape(TOP_K, T // TOKEN_TILE, TOKEN_TILE).transpose(1, 0, 2).reshape(T * TOP_K)

    xs = _dispatch(slots_tiled, h1t)
    ys = _experts(item_blk, item_exp, item_lo, item_hi, item_next, item_slot, xs, w_gate[0], w_up[0], w_down[0])
    out = _combine(
        slots_tiled, gw, h1, ys, w_sh_gate[0].astype(BF16), w_sh_up[0].astype(BF16),
        w_sh_down[0].astype(BF16), ln2_g[0].reshape(1, D), ln2_b[0].reshape(1, D))
    return out.reshape(B, S, D)
```

```python
import functools
import math

import jax
import jax.numpy as jnp
from jax import lax
from jax.experimental import pallas as pl
from jax.experimental.pallas import tpu as pltpu

D_MODEL = 1024
CHUNK = 128
A_GROUPS = 8
A_GROUP_DIM = 128
A_WIDTH = A_GROUPS * A_GROUP_DIM
N_HEADS = 8
QK_NOPE = 128
QK_ROPE = 64
QK_DIM = QK_NOPE + QK_ROPE
V_HEAD = 128
V_ROWS = 144
Q_LORA = 384
KV_LORA = 256
ROPE_THETA = 10000.0
N_EXPERTS = 256
TOP_K = 8
N_GROUPS = 8
TOPK_GROUPS = 4
GROUP_SIZE = N_EXPERTS // N_GROUPS
D_EXPERT = 256
ROUTED_SCALE = 2.5
DN_ALPHA = 2.0 ** 0.25
LN_EPS = 1e-5
RMS_EPS = 1e-6

QK_PAD = 256
LANES = 128
TOKEN_TILE = 256
Q_TILE = 1024
KV_TILE = 512
KV_GROUP = 4
SLOT_LANES = 2048
ISSUE_UNROLL = 4
ROW_BLOCK = 256
SUBLANES = 8
VMEM_LIMIT = 56 * 1024 * 1024

F32 = jnp.float32
BF16 = jnp.bfloat16
NEG_INF = float("-inf")


def _dot(a, b):
    return jnp.dot(a, b, preferred_element_type=F32)


def _sigmoid(x):
    return 1.0 / (1.0 + jnp.exp(-x))


def _gelu_tanh(x):
    return 0.5 * x * (1.0 + jnp.tanh(math.sqrt(2.0 / math.pi) * (x + 0.044715 * (x * x * x))))


def _layer_norm(x, g, b):
    mu = jnp.mean(x, axis=-1, keepdims=True)
    d = x - mu
    var = jnp.mean(d * d, axis=-1, keepdims=True)
    return d * lax.rsqrt(var + LN_EPS) * g + b


def _rms_norm(x, g):
    return x * lax.rsqrt(jnp.mean(x * x, axis=-1, keepdims=True) + RMS_EPS) * g


def _tile_rows_load(ref, first, n):
    return jnp.concatenate(
        [ref[pl.ds(first * SUBLANES + j, n, stride=SUBLANES), :] for j in range(SUBLANES)], axis=1)


def _tile_rows_store(ref, first, n, value):
    for j in range(SUBLANES):
        ref[pl.ds(first * SUBLANES + j, n, stride=SUBLANES), :] = value[:, j * LANES:(j + 1) * LANES]


def _const_spec(shape):
    zeros = (0,) * len(shape)
    return pl.BlockSpec(shape, lambda *_: zeros)


def _params(semantics):
    return pltpu.CompilerParams(dimension_semantics=semantics, vmem_limit_bytes=VMEM_LIMIT)


def _gmlp_kernel(x_ref, wu_ref, wv_ref, wg_ref, lng_ref, lnb_ref, ws_ref, bias_ref, wa_ref, o_ref, mix_ref):
    tm = x_ref.shape[0]
    xb = x_ref[...].astype(BF16)
    v = _gelu_tanh(_dot(xb, wv_ref[...]))
    for h in range(A_GROUPS):
        cols = slice(h * A_GROUP_DIM, (h + 1) * A_GROUP_DIM)
        vn = _layer_norm(v[:, cols], lng_ref[:, cols], lnb_ref[:, cols]).astype(BF16)
        for c in range(tm // CHUNK):
            rows = slice(c * CHUNK, (c + 1) * CHUNK)
            mix_ref[rows, cols] = _dot(ws_ref[h], vn[rows, :]) + bias_ref[:, cols]
    u = _gelu_tanh(_dot(xb, wu_ref[...]))
    ya = _dot((u * mix_ref[...]).astype(BF16), wa_ref[...])
    gate = _sigmoid(_dot(xb, wg_ref[...]))
    o_ref[...] = (gate * ya).astype(BF16)


def _gmlp_branch(x2, w_u, w_v, w_ga, ln_g, ln_b, w_s, bias_full, w_a):
    T = x2.shape[0]
    tm = TOKEN_TILE
    return pl.pallas_call(
        _gmlp_kernel,
        grid=(T // tm,),
        in_specs=[
            pl.BlockSpec((tm, D_MODEL), lambda i: (i, 0)),
            _const_spec((D_MODEL, A_WIDTH)),
            _const_spec((D_MODEL, A_WIDTH)),
            _const_spec((D_MODEL, D_MODEL)),
            _const_spec((1, A_WIDTH)),
            _const_spec((1, A_WIDTH)),
            _const_spec((A_GROUPS, CHUNK, CHUNK)),
            _const_spec((CHUNK, A_WIDTH)),
            _const_spec((A_WIDTH, D_MODEL)),
        ],
        out_specs=pl.BlockSpec((tm, D_MODEL), lambda i: (i, 0)),
        out_shape=jax.ShapeDtypeStruct((T, D_MODEL), BF16),
        scratch_shapes=[pltpu.VMEM((tm, A_WIDTH), F32)],
        compiler_params=_params(("arbitrary",)),
        name="gmlp_branch",
    )(x2, w_u, w_v, w_ga, ln_g, ln_b, w_s, bias_full, w_a)


def _mla_prep_kernel(x_ref, posr_ref, posc_ref, wb_ref, qg_ref, kvg_ref, wuqt_ref, wuk_ref, wuvt_ref,
                     freqc_ref, freqr_ref, signc_ref, signr_ref,
                     qt_ref, k_ref, vt_ref, sgb_ref):
    xb = x_ref[...].astype(BF16)
    pb = _dot(xb, wb_ref[...])
    c_q = pb[:, :Q_LORA]
    c_kv = pb[:, Q_LORA:Q_LORA + KV_LORA]
    o = Q_LORA + KV_LORA
    kr = pb[:, o:o + LANES]
    kr_partner = pb[:, o + LANES:o + 2 * LANES]
    gb = pb[:, o + 2 * LANES:]
    sgb_ref[...] = _sigmoid(gb).astype(BF16)

    cqn = _rms_norm(c_q, qg_ref[...])
    ckvn = _rms_norm(c_kv, kvg_ref[...])
    cqn_t = cqn.T.astype(BF16)
    ckvn_t = ckvn.T.astype(BF16)

    scale = math.log2(math.e) / math.sqrt(QK_DIM)
    ang_t = freqc_ref[...] * posr_ref[0]
    table_t = jnp.concatenate(
        [jnp.cos(ang_t[:QK_ROPE]), signc_ref[QK_ROPE:] * jnp.sin(ang_t[QK_ROPE:])], axis=0) * scale
    q_t = _dot(wuqt_ref[...], cqn_t)
    for h in range(N_HEADS):
        blk = q_t[h * QK_PAD:(h + 1) * QK_PAD]
        qt_ref[0, h] = jnp.concatenate(
            [blk[:QK_NOPE] * scale, blk[QK_NOPE:] * table_t], axis=0).astype(BF16)

    v_t = _dot(wuvt_ref[...], ckvn_t)
    ones_rows = jnp.ones((V_ROWS - V_HEAD, v_t.shape[1]), F32)
    for h in range(N_HEADS):
        vt_ref[0, h] = jnp.concatenate([v_t[h * V_HEAD:(h + 1) * V_HEAD], ones_rows], axis=0).astype(BF16)

    ang = posc_ref[...] * freqr_ref[...]
    k_rope = (kr * jnp.cos(ang) + kr_partner * (signr_ref[...] * jnp.sin(ang))).astype(BF16)
    k_nope = _dot(ckvn.astype(BF16), wuk_ref[...])
    for h in range(N_HEADS):
        k_ref[0, h] = jnp.concatenate(
            [k_nope[:, h * QK_NOPE:(h + 1) * QK_NOPE].astype(BF16), k_rope], axis=1)


def _mla_prep(x2, pos_row, pos_col, w_b, q_g, kv_g, w_uqt, w_uk, w_uvt, freq_col, freq_row, sign_col, sign_row,
              B, S):
    T = x2.shape[0]
    tm = TOKEN_TILE
    spb = S // tm
    n_b = w_b.shape[1]
    head_map = lambda i: (i // spb, 0, 0, i % spb)
    return pl.pallas_call(
        _mla_prep_kernel,
        grid=(T // tm,),
        in_specs=[
            pl.BlockSpec((tm, D_MODEL), lambda i: (i, 0)),
            pl.BlockSpec((1, 1, tm), lambda i: (i, 0, 0)),
            pl.BlockSpec((tm, 1), lambda i: (i, 0)),
            _const_spec((D_MODEL, n_b)),
            _const_spec((1, Q_LORA)),
            _const_spec((1, KV_LORA)),
            _const_spec((N_HEADS * QK_PAD, Q_LORA)),
            _const_spec((KV_LORA, N_HEADS * QK_NOPE)),
            _const_spec((N_HEADS * V_HEAD, KV_LORA)),
            _const_spec((LANES, 1)),
            _const_spec((1, LANES)),
            _const_spec((LANES, 1)),
            _const_spec((1, LANES)),
        ],
        out_specs=[
            pl.BlockSpec((1, N_HEADS, QK_PAD, tm), head_map),
            pl.BlockSpec((1, N_HEADS, tm, QK_PAD), lambda i: (i // spb, 0, i % spb, 0)),
            pl.BlockSpec((1, N_HEADS, V_ROWS, tm), head_map),
            pl.BlockSpec((tm, D_MODEL), lambda i: (i, 0)),
        ],
        out_shape=[
            jax.ShapeDtypeStruct((B, N_HEADS, QK_PAD, S), BF16),
            jax.ShapeDtypeStruct((B, N_HEADS, S, QK_PAD), BF16),
            jax.ShapeDtypeStruct((B, N_HEADS, V_ROWS, S), BF16),
            jax.ShapeDtypeStruct((T, D_MODEL), BF16),
        ],
        compiler_params=_params(("arbitrary",)),
        name="mla_prep",
    )(x2, pos_row, pos_col, w_b, q_g, kv_g, w_uqt, w_uk, w_uvt, freq_col, freq_row, sign_col, sign_row)


def _attention_kernel(qt_ref, k_ref, vt_ref, o_ref, s0_ref, s1_ref, m_ref, acc_ref):
    S = k_ref.shape[2]
    n_chunks = S // KV_TILE
    q_t = qt_ref[0, 0]

    def scores(c, dst_ref):
        start = pl.multiple_of(c * KV_TILE, KV_TILE)
        dst_ref[...] = _dot(k_ref[0, 0, pl.ds(start, KV_TILE), :], q_t)

    def accumulate(c, src_ref):
        start = pl.multiple_of(c * KV_TILE, KV_TILE)
        s_t = src_ref[...]
        m = m_ref[...]
        m_new = jnp.maximum(m, jnp.max(s_t, axis=0, keepdims=True))
        p_t = jnp.exp2(s_t - m_new)
        alpha = jnp.exp2(m - m_new)
        m_ref[...] = m_new
        acc_ref[...] = alpha * acc_ref[...] + _dot(vt_ref[0, 0, :, pl.ds(start, KV_TILE)], p_t.astype(BF16))

    m_ref[...] = jnp.full(m_ref.shape, NEG_INF, F32)
    acc_ref[...] = jnp.zeros(acc_ref.shape, F32)
    scores(0, s0_ref)

    bufs = (s0_ref, s1_ref)

    def group(c0, prefetch_last):
        for i in range(KV_GROUP):
            if i + 1 < KV_GROUP or prefetch_last:
                scores(c0 + i + 1, bufs[(i + 1) % 2])
            accumulate(c0 + i, bufs[i % 2])

    def body(g, carry):
        group(g * KV_GROUP, True)
        return carry

    lax.fori_loop(0, n_chunks // KV_GROUP - 1, body, 0)
    group(n_chunks - KV_GROUP, False)
    o_ref[0] = (acc_ref[:V_HEAD] / acc_ref[V_HEAD:V_HEAD + 1]).T.astype(BF16)


def _attention(q_t, k, v_t):
    B, H, _, S = q_t.shape
    tq = min(Q_TILE, S)
    assert S % tq == 0 and S % (KV_TILE * KV_GROUP) == 0 and KV_GROUP % 2 == 0
    return pl.pallas_call(
        _attention_kernel,
        grid=(B, H, S // tq),
        in_specs=[
            pl.BlockSpec((1, 1, QK_PAD, tq), lambda b, h, i: (b, h, 0, i)),
            pl.BlockSpec((1, 1, S, QK_PAD), lambda b, h, i: (b, h, 0, 0)),
            pl.BlockSpec((1, 1, V_ROWS, S), lambda b, h, i: (b, h, 0, 0)),
        ],
        out_specs=pl.BlockSpec((1, tq, V_HEAD), lambda b, h, i: (b, i, h)),
        out_shape=jax.ShapeDtypeStruct((B, S, H * V_HEAD), BF16),
        scratch_shapes=[
            pltpu.VMEM((KV_TILE, tq), F32),
            pltpu.VMEM((KV_TILE, tq), F32),
            pltpu.VMEM((1, tq), F32),
            pltpu.VMEM((V_ROWS, tq), F32),
        ],
        compiler_params=_params(("arbitrary", "arbitrary", "arbitrary")),
        name="attention",
    )(q_t, k, v_t)


def _mix_route_kernel(x_ref, ga_ref, sgb_ref, o_ref, wbp_ref, wout_ref, g1_ref, b1_ref, wr_ref, rb_ref,
                      h1_ref, h1t_ref, idx_ref, rank_ref, gw_ref, cnt_ref, base_ref):
    tm = x_ref.shape[0]

    @pl.when(pl.program_id(0) == 0)
    def _():
        base_ref[...] = jnp.zeros_like(base_ref)

    yb = _dot(o_ref[...], wbp_ref[...])
    merged = ga_ref[...].astype(F32) + sgb_ref[...].astype(F32) * yb
    mixed = _dot(merged.astype(BF16), wout_ref[...])
    h1 = _layer_norm(DN_ALPHA * x_ref[...] + mixed, g1_ref[...], b1_ref[...])
    h1_ref[...] = h1
    _tile_rows_store(h1t_ref, 0, tm, h1)

    scores_t = _sigmoid(_dot(h1.astype(BF16), wr_ref[...])).T
    biased = scores_t + rb_ref[...]

    riota = lax.broadcasted_iota(jnp.int32, (GROUP_SIZE, tm), 0)
    blocks, gscore = [], []
    for g in range(N_GROUPS):
        blk = biased[g * GROUP_SIZE:(g + 1) * GROUP_SIZE]
        m1 = jnp.max(blk, axis=0, keepdims=True)
        i1 = jnp.min(jnp.where(blk == m1, riota, GROUP_SIZE), axis=0, keepdims=True)
        m2 = jnp.max(jnp.where(riota == i1, NEG_INF, blk), axis=0, keepdims=True)
        blocks.append(blk)
        gscore.append(m1 + m2)

    selected = [jnp.zeros((1, tm), F32) for _ in range(N_GROUPS)]
    for _ in range(TOPK_GROUPS):
        best = functools.reduce(jnp.maximum, gscore)
        first = functools.reduce(
            jnp.minimum, [jnp.where(gscore[g] == best, g, N_GROUPS) for g in range(N_GROUPS)])
        for g in range(N_GROUPS):
            hit = first == g
            selected[g] = jnp.where(hit, 1.0, selected[g])
            gscore[g] = jnp.where(hit, NEG_INF, gscore[g])

    masked = jnp.concatenate(
        [jnp.where(jnp.broadcast_to(selected[g], blocks[g].shape) > 0.5, blocks[g], NEG_INF)
         for g in range(N_GROUPS)], axis=0)

    eiota = lax.broadcasted_iota(jnp.int32, (N_EXPERTS, tm), 0)
    picks, weights = [], []
    for _ in range(TOP_K):
        best = jnp.max(masked, axis=0, keepdims=True)
        e = jnp.min(jnp.where(masked == best, eiota, N_EXPERTS), axis=0, keepdims=True)
        hit = eiota == e
        masked = jnp.where(hit, NEG_INF, masked)
        picks.append(e)
        weights.append(jnp.sum(jnp.where(hit, scores_t, 0.0), axis=0, keepdims=True))
    wsum = functools.reduce(lambda a, b: a + b, weights)

    chosen = functools.reduce(
        lambda a, b: a + b, [jnp.where(eiota == e, 1.0, 0.0) for e in picks])
    before = (lax.broadcasted_iota(jnp.int32, (tm, tm), 0)
              < lax.broadcasted_iota(jnp.int32, (tm, tm), 1)).astype(BF16)
    rank_full = _dot(chosen.astype(BF16), before) + base_ref[:, :1]
    new_base = base_ref[:, :1] + jnp.sum(chosen, axis=1, keepdims=True)
    base_ref[...] = jnp.broadcast_to(new_base, base_ref.shape)
    cnt_ref[...] = jnp.broadcast_to(new_base, cnt_ref.shape)

    kiota = lax.broadcasted_iota(jnp.int32, (TOP_K, tm), 0)
    wiota = lax.broadcasted_iota(jnp.int32, (LANES, tm), 0)
    idx_out = jnp.zeros((TOP_K, tm), jnp.int32)
    rank_out = jnp.zeros((TOP_K, tm), jnp.int32)
    gw_t = jnp.zeros((LANES, tm), F32)
    for k in range(TOP_K):
        r = jnp.sum(jnp.where(eiota == picks[k], rank_full, 0.0), axis=0, keepdims=True)
        idx_out = jnp.where(kiota == k, picks[k], idx_out)
        rank_out = jnp.where(kiota == k, r.astype(jnp.int32), rank_out)
        gw_t = jnp.where(wiota == k, weights[k] / wsum * ROUTED_SCALE, gw_t)
    idx_ref[...] = idx_out
    rank_ref[...] = rank_out
    gw_ref[...] = gw_t.T


def _mix_route(x2, ga, sgb, o, w_bp, w_out, g1, b1, w_r, r_bias):
    T = x2.shape[0]
    tm = TOKEN_TILE
    row = lambda i: (i, 0)
    col = lambda i: (0, i)
    return pl.pallas_call(
        _mix_route_kernel,
        grid=(T // tm,),
        in_specs=[
            pl.BlockSpec((tm, D_MODEL), row),
            pl.BlockSpec((tm, D_MODEL), row),
            pl.BlockSpec((tm, D_MODEL), row),
            pl.BlockSpec((tm, D_MODEL), row),
            _const_spec((D_MODEL, D_MODEL)),
            _const_spec((D_MODEL, D_MODEL)),
            _const_spec((1, D_MODEL)),
            _const_spec((1, D_MODEL)),
            _const_spec((D_MODEL, N_EXPERTS)),
            _const_spec((N_EXPERTS, 1)),
        ],
        out_specs=[
            pl.BlockSpec((tm, D_MODEL), row),
            pl.BlockSpec((tm * SUBLANES, LANES), row),
            pl.BlockSpec((TOP_K, tm), col),
            pl.BlockSpec((TOP_K, tm), col),
            pl.BlockSpec((tm, LANES), row),
            _const_spec((N_EXPERTS, LANES)),
        ],
        out_shape=[
            jax.ShapeDtypeStruct((T, D_MODEL), F32),
            jax.ShapeDtypeStruct((T * SUBLANES, LANES), F32),
            jax.ShapeDtypeStruct((TOP_K, T), jnp.int32),
            jax.ShapeDtypeStruct((TOP_K, T), jnp.int32),
            jax.ShapeDtypeStruct((T, LANES), F32),
            jax.ShapeDtypeStruct((N_EXPERTS, LANES), F32),
        ],
        scratch_shapes=[pltpu.VMEM((N_EXPERTS, LANES), F32)],
        compiler_params=_params(("arbitrary",)),
        name="mix_route",
    )(x2, ga, sgb, o, w_bp, w_out, g1, b1, w_r, r_bias)


def _slot_kernel(starts_ref, idx_ref, rank_ref, slot_ref):
    idx = idx_ref[...]

    def lookup(e, acc):
        return jnp.where(idx == e, starts_ref[e], acc)

    slot_ref[...] = lax.fori_loop(0, N_EXPERTS, lookup, jnp.zeros_like(idx), unroll=8) + rank_ref[...]


def _slot_index(starts, idx_t, rank_t):
    T = idx_t.shape[1]
    tl = min(SLOT_LANES, T)
    col = lambda i: (0, i)
    return pl.pallas_call(
        _slot_kernel,
        grid=(T // tl,),
        in_specs=[
            pl.BlockSpec(memory_space=pltpu.SMEM),
            pl.BlockSpec((TOP_K, tl), col),
            pl.BlockSpec((TOP_K, tl), col),
        ],
        out_specs=pl.BlockSpec((TOP_K, tl), col),
        out_shape=jax.ShapeDtypeStruct((TOP_K, T), jnp.int32),
        compiler_params=_params(("arbitrary",)),
        name="slot_index",
    )(starts, idx_t, rank_t)


def _token_tile(ref, token):
    return ref.at[pl.ds(pl.multiple_of(token * SUBLANES, SUBLANES), SUBLANES), :]


def _dispatch_kernel(slot_ref, h1t_ref, xs_ref, sem):
    tm = h1t_ref.shape[0] // SUBLANES

    def issue(t, carry):
        src = _token_tile(h1t_ref, t)
        for k in range(TOP_K):
            pltpu.make_async_copy(src, _token_tile(xs_ref, slot_ref[k * tm + t]), sem).start(priority=k % 2)
        return carry

    lax.fori_loop(0, tm, issue, 0, unroll=ISSUE_UNROLL)
    for _ in range(TOP_K):
        pltpu.make_async_copy(h1t_ref, xs_ref.at[pl.ds(0, tm * SUBLANES), :], sem).wait()


def _dispatch(slots_tiled, h1t):
    T = h1t.shape[0] // SUBLANES
    tm = TOKEN_TILE
    return pl.pallas_call(
        _dispatch_kernel,
        grid=(T // tm,),
        in_specs=[
            pl.BlockSpec((TOP_K * tm,), lambda i: (i,), memory_space=pltpu.SMEM),
            pl.BlockSpec((tm * SUBLANES, LANES), lambda i: (i, 0)),
        ],
        out_specs=pl.BlockSpec(memory_space=pl.ANY),
        out_shape=jax.ShapeDtypeStruct((T * TOP_K * SUBLANES, LANES), F32),
        scratch_shapes=[pltpu.SemaphoreType.DMA(())],
        compiler_params=_params(("arbitrary",)),
        name="dispatch",
    )(slots_tiled, h1t)


def _weight_copies(expert, slot, w_hbm_refs, w_buf_refs, sems):
    return [pltpu.make_async_copy(w_hbm.at[expert], w_buf.at[slot], sems.at[slot, n])
            for n, (w_hbm, w_buf) in enumerate(zip(w_hbm_refs, w_buf_refs))]


def _experts_kernel(blk_ref, exp_ref, lo_ref, hi_ref, next_ref, slot_ref,
                    xs_ref, wg_hbm, wu_hbm, wd_hbm, ys_ref,
                    wg_buf, wu_buf, wd_buf, wg_bf_ref, wu_bf_ref, wd_bf_ref, sems):
    i = pl.program_id(0)
    blk = blk_ref[i]
    lo = lo_ref[i]
    hi = hi_ref[i]
    expert = exp_ref[i]
    slot = slot_ref[i]
    prev = jnp.maximum(i - 1, 0)
    w_hbm_refs = (wg_hbm, wu_hbm, wd_hbm)
    w_buf_refs = (wg_buf, wu_buf, wd_buf)

    @pl.when(i == 0)
    def _():
        for copy in _weight_copies(expert, slot, w_hbm_refs, w_buf_refs, sems):
            copy.start()

    @pl.when((i == 0) | (blk_ref[prev] != blk))
    def _():
        ys_ref[...] = jnp.zeros_like(ys_ref)

    @pl.when((i == 0) | (exp_ref[prev] != expert))
    def _():
        following = next_ref[i]

        @pl.when(following < N_EXPERTS)
        def _():
            for copy in _weight_copies(following, 1 - slot, w_hbm_refs, w_buf_refs, sems):
                copy.start()

        for copy in _weight_copies(expert, slot, w_hbm_refs, w_buf_refs, sems):
            copy.wait()
        wg_bf_ref[...] = wg_buf[slot].astype(BF16)
        wu_bf_ref[...] = wu_buf[slot].astype(BF16)
        wd_bf_ref[...] = wd_buf[slot].astype(BF16)

    @pl.when(hi > lo)
    def _():
        x = _tile_rows_load(xs_ref, 0, ROW_BLOCK).astype(BF16)
        gate = _dot(x, wg_bf_ref[...])
        up = _dot(x, wu_bf_ref[...])
        hidden = (gate * _sigmoid(gate) * up).astype(BF16)
        y = _dot(hidden, wd_bf_ref[...])
        rows = blk * ROW_BLOCK + lax.broadcasted_iota(jnp.int32, (ROW_BLOCK, 1), 0)
        keep = (rows >= lo) & (rows < hi)
        _tile_rows_store(ys_ref, 0, ROW_BLOCK, jnp.where(keep, y, _tile_rows_load(ys_ref, 0, ROW_BLOCK)))


def _experts(item_blk, item_exp, item_lo, item_hi, item_next, item_slot, xs, w_gate, w_up, w_down):
    n_rows = xs.shape[0] // SUBLANES
    n_items = item_blk.shape[0]
    block = (ROW_BLOCK * SUBLANES, LANES)
    grid_spec = pltpu.PrefetchScalarGridSpec(
        num_scalar_prefetch=6,
        grid=(n_items,),
        in_specs=[
            pl.BlockSpec(block, lambda i, b, *_: (b[i], 0)),
            pl.BlockSpec(memory_space=pl.ANY),
            pl.BlockSpec(memory_space=pl.ANY),
            pl.BlockSpec(memory_space=pl.ANY),
        ],
        out_specs=pl.BlockSpec(block, lambda i, b, *_: (b[i], 0)),
        scratch_shapes=[
            pltpu.VMEM((2, D_MODEL, D_EXPERT), F32),
            pltpu.VMEM((2, D_MODEL, D_EXPERT), F32),
            pltpu.VMEM((2, D_EXPERT, D_MODEL), F32),
            pltpu.VMEM((D_MODEL, D_EXPERT), BF16),
            pltpu.VMEM((D_MODEL, D_EXPERT), BF16),
            pltpu.VMEM((D_EXPERT, D_MODEL), BF16),
            pltpu.SemaphoreType.DMA((2, 3)),
        ],
    )
    return pl.pallas_call(
        _experts_kernel,
        grid_spec=grid_spec,
        out_shape=jax.ShapeDtypeStruct((n_rows * SUBLANES, LANES), F32),
        compiler_params=_params(("arbitrary",)),
        name="experts",
    )(item_blk, item_exp, item_lo, item_hi, item_next, item_slot, xs, w_gate, w_up, w_down)


def _combine_kernel(slot_ref, gw_ref, h1_ref, ys_ref, wsg_ref, wsu_ref, wsd_ref,
                    g2_ref, b2_ref, out_ref, buf_ref, sem):
    tm = h1_ref.shape[0]

    def issue(t, carry):
        for k in range(TOP_K):
            pltpu.make_async_copy(
                _token_tile(ys_ref, slot_ref[k * tm + t]), _token_tile(buf_ref, k * tm + t), sem).start(priority=k % 2)
        return carry

    lax.fori_loop(0, tm, issue, 0, unroll=ISSUE_UNROLL)

    h1 = h1_ref[...]
    hb = h1.astype(BF16)
    gate = _dot(hb, wsg_ref[...])
    hidden = (gate * _sigmoid(gate) * _dot(hb, wsu_ref[...])).astype(BF16)
    moe = _dot(hidden, wsd_ref[...])

    for k in range(TOP_K):
        pltpu.make_async_copy(
            ys_ref.at[pl.ds(0, tm * SUBLANES), :], buf_ref.at[pl.ds(k * tm * SUBLANES, tm * SUBLANES), :], sem).wait()

    gw = gw_ref[...]
    for k in range(TOP_K):
        moe = moe + gw[:, k:k + 1] * _tile_rows_load(buf_ref, k * tm, tm)
    out_ref[...] = _layer_norm(DN_ALPHA * h1 + moe, g2_ref[...], b2_ref[...])


def _combine(slots_tiled, gw, h1, ys, w_sg, w_su, w_sd, g2, b2):
    T = h1.shape[0]
    tm = TOKEN_TILE
    row = lambda i: (i, 0)
    return pl.pallas_call(
        _combine_kernel,
        grid=(T // tm,),
        in_specs=[
            pl.BlockSpec((TOP_K * tm,), lambda i: (i,), memory_space=pltpu.SMEM),
            pl.BlockSpec((tm, LANES), row),
            pl.BlockSpec((tm, D_MODEL), row),
            pl.BlockSpec(memory_space=pl.ANY),
            _const_spec((D_MODEL, D_EXPERT)),
            _const_spec((D_MODEL, D_EXPERT)),
            _const_spec((D_EXPERT, D_MODEL)),
            _const_spec((1, D_MODEL)),
            _const_spec((1, D_MODEL)),
        ],
        out_specs=pl.BlockSpec((tm, D_MODEL), row),
        out_shape=jax.ShapeDtypeStruct((T, D_MODEL), F32),
        scratch_shapes=[pltpu.VMEM((TOP_K * tm * SUBLANES, LANES), F32), pltpu.SemaphoreType.DMA(())],
        compiler_params=_params(("arbitrary",)),
        name="combine",
    )(slots_tiled, gw, h1, ys, w_sg, w_su, w_sd, g2, b2)


def _expert_work_items(counts, n_rows):
    n_blocks = n_rows // ROW_BLOCK
    n_items = n_blocks + N_EXPERTS
    ends = jnp.cumsum(counts)
    starts = ends - counts
    first_blk = starts // ROW_BLOCK
    last_blk = jnp.maximum(ends - 1, 0) // ROW_BLOCK
    per_expert = jnp.where(counts > 0, last_blk - first_blk + 1, 0)
    item_end = jnp.cumsum(per_expert)
    item_start = item_end - per_expert
    total = item_end[-1]
    i = jnp.arange(n_items, dtype=jnp.int32)
    live = i < total
    j = jnp.minimum(i, total - 1)[:, None]
    owner = (item_start[None, :] <= j) & (j < item_end[None, :])

    def pick(per_expert_values):
        return jnp.sum(jnp.where(owner, per_expert_values[None, :], 0), axis=1)

    expert_ids = jnp.arange(N_EXPERTS, dtype=jnp.int32)
    nonempty = counts > 0
    later = (expert_ids[None, :] > expert_ids[:, None]) & nonempty[None, :]
    following = jnp.min(jnp.where(later, expert_ids[None, :], N_EXPERTS), axis=1)
    buffer_slot = (jnp.cumsum(nonempty) - 1) % 2
    exp = pick(expert_ids)
    blk = pick(first_blk - item_start) + j[:, 0]
    lo = jnp.maximum(pick(starts), blk * ROW_BLOCK)
    hi = jnp.minimum(pick(ends), (blk + 1) * ROW_BLOCK)
    lo = jnp.where(live, lo, 0)
    hi = jnp.where(live, hi, 0)
    as_i32 = lambda a: a.astype(jnp.int32)
    return (as_i32(starts), as_i32(blk), as_i32(exp), as_i32(lo), as_i32(hi),
            as_i32(pick(following)), as_i32(pick(buffer_slot)))


def kernel(x, positions, w_in, gmlp_ln_g, gmlp_ln_b, w_spatial, b_spatial, w_a_proj, q_norm_g, w_uq, kv_norm_g, w_uk, w_uv, w_b_proj, w_out, ln1_g, ln1_b, w_router, router_bias, w_gate, w_up, w_down, w_sh_gate, w_sh_up, w_sh_down, ln2_g, ln2_b):
    B, S, D = x.shape
    T = B * S
    x2 = x.reshape(T, D)

    w = w_in[0]
    o_u, o_v, o_cq = 0, A_WIDTH, 2 * A_WIDTH
    o_ckv = o_cq + Q_LORA
    o_kr = o_ckv + KV_LORA
    o_ga = o_kr + QK_ROPE
    o_gb = o_ga + D_MODEL
    half_r = QK_ROPE // 2
    w_kr = w[:, o_kr:o_ga]
    w_kr_partner = jnp.concatenate([w_kr[:, half_r:], w_kr[:, :half_r]], axis=1)
    w_b = jnp.concatenate(
        [w[:, o_cq:o_kr], w_kr, w_kr, w_kr_partner, w_kr_partner, w[:, o_gb:]], axis=1).astype(BF16)
    uq = w_uq[0]
    uq_rope = uq[:, :, QK_NOPE:]
    uq_partner = jnp.concatenate([uq_rope[:, :, half_r:], uq_rope[:, :, :half_r]], axis=2)
    w_uqt = jnp.concatenate([uq, uq_partner], axis=2).reshape(Q_LORA, N_HEADS * QK_PAD).T.astype(BF16)
    w_ukm = w_uk[0].reshape(KV_LORA, N_HEADS * QK_NOPE).astype(BF16)
    w_uvt = w_uv[0].reshape(KV_LORA, N_HEADS * V_HEAD).T.astype(BF16)

    freq = ROPE_THETA ** (-jnp.arange(0, half_r, dtype=F32) * 2.0 / QK_ROPE)
    freq128 = jnp.tile(freq, LANES // half_r)
    sign128 = jnp.tile(jnp.concatenate([-jnp.ones((half_r,), F32), jnp.ones((half_r,), F32)]), LANES // QK_ROPE)
    pos_f = positions.astype(F32).reshape(T)
    bias_full = jnp.repeat(b_spatial[0].T, A_GROUP_DIM, axis=1)

    ga = _gmlp_branch(
        x2, w[:, o_u:o_v].astype(BF16), w[:, o_v:o_cq].astype(BF16), w[:, o_ga:o_gb].astype(BF16),
        gmlp_ln_g[0].reshape(1, A_WIDTH), gmlp_ln_b[0].reshape(1, A_WIDTH),
        w_spatial[0].astype(BF16), bias_full, w_a_proj[0].astype(BF16))

    q_t, k, v_t, sgb = _mla_prep(
        x2, pos_f.reshape(T // TOKEN_TILE, 1, TOKEN_TILE), pos_f.reshape(T, 1), w_b,
        q_norm_g[0].reshape(1, Q_LORA), kv_norm_g[0].reshape(1, KV_LORA), w_uqt, w_ukm, w_uvt,
        freq128.reshape(LANES, 1), freq128.reshape(1, LANES), sign128.reshape(LANES, 1), sign128.reshape(1, LANES),
        B, S)

    o = _attention(q_t, k, v_t).reshape(T, N_HEADS * V_HEAD)

    h1, h1t, idx_t, rank_t, gw, cnt = _mix_route(
        x2, ga, sgb, o, w_b_proj[0].astype(BF16), w_out[0].astype(BF16),
        ln1_g[0].reshape(1, D), ln1_b[0].reshape(1, D), w_router[0].astype(BF16),
        router_bias[0].reshape(N_EXPERTS, 1))

    counts = cnt[:, 0].astype(jnp.int32)
    starts, item_blk, item_exp, item_lo, item_hi, item_next, item_slot = _expert_work_items(counts, T * TOP_K)

    slots = _slot_index(starts, idx_t, rank_t)
    slots_tiled = slots.reshape(TOP_K, T // TOKEN_TILE, TOKEN_TILE).transpose(1, 0, 2).reshape(T * TOP_K)

    xs = _dispatch(slots_tiled, h1t)
    ys = _experts(item_blk, item_exp, item_lo, item_hi, item_next, item_slot, xs, w_gate[0], w_up[0], w_down[0])
    out = _combine(
        slots_tiled, gw, h1, ys, w_sh_gate[0].astype(BF16), w_sh_up[0].astype(BF16),
        w_sh_down[0].astype(BF16), ln2_g[0].reshape(1, D), ln2_b[0].reshape(1, D))
    return out.reshape(B, S, D)
```

```python
import functools
import math

import jax
import jax.numpy as jnp
from jax import lax
from jax.experimental import pallas as pl
from jax.experimental.pallas import tpu as pltpu

D_MODEL = 1024
CHUNK = 128
A_GROUPS = 8
A_GROUP_DIM = 128
A_WIDTH = A_GROUPS * A_GROUP_DIM
N_HEADS = 8
QK_NOPE = 128
QK_ROPE = 64
QK_DIM = QK_NOPE + QK_ROPE
V_HEAD = 128
V_ROWS = 144
Q_LORA = 384
KV_LORA = 256
ROPE_THETA = 10000.0
N_EXPERTS = 256
TOP_K = 8
N_GROUPS = 8
TOPK_GROUPS = 4
GROUP_SIZE = N_EXPERTS // N_GROUPS
D_EXPERT = 256
ROUTED_SCALE = 2.5
DN_ALPHA = 2.0 ** 0.25
LN_EPS = 1e-5
RMS_EPS = 1e-6

QK_PAD = 256
LANES = 128
TOKEN_TILE = 256
Q_TILE = 1024
KV_TILE = 512
KV_GROUP = 4
SLOT_LANES = 2048
ISSUE_UNROLL = 4
ROW_BLOCK = 256
SUBLANES = 8
VMEM_LIMIT = 56 * 1024 * 1024

F32 = jnp.float32
BF16 = jnp.bfloat16
NEG_INF = float("-inf")


def _dot(a, b):
    return jnp.dot(a, b, preferred_element_type=F32)


def _sigmoid(x):
    return 1.0 / (1.0 + jnp.exp(-x))


def _gelu_tanh(x):
    return 0.5 * x * (1.0 + jnp.tanh(math.sqrt(2.0 / math.pi) * (x + 0.044715 * (x * x * x))))


def _layer_norm(x, g, b):
    mu = jnp.mean(x, axis=-1, keepdims=True)
    d = x - mu
    var = jnp.mean(d * d, axis=-1, keepdims=True)
    return d * lax.rsqrt(var + LN_EPS) * g + b


def _rms_norm(x, g):
    return x * lax.rsqrt(jnp.mean(x * x, axis=-1, keepdims=True) + RMS_EPS) * g


def _tile_rows_load(ref, first, n):
    return jnp.concatenate(
        [ref[pl.ds(first * SUBLANES + j, n, stride=SUBLANES), :] for j in range(SUBLANES)], axis=1)


def _tile_rows_store(ref, first, n, value):
    for j in range(SUBLANES):
        ref[pl.ds(first * SUBLANES + j, n, stride=SUBLANES), :] = value[:, j * LANES:(j + 1) * LANES]


def _const_spec(shape):
    zeros = (0,) * len(shape)
    return pl.BlockSpec(shape, lambda *_: zeros)


def _params(semantics):
    return pltpu.CompilerParams(dimension_semantics=semantics, vmem_limit_bytes=VMEM_LIMIT)


def _gmlp_kernel(x_ref, wu_ref, wv_ref, wg_ref, lng_ref, lnb_ref, ws_ref, bias_ref, wa_ref, o_ref, mix_ref):
    tm = x_ref.shape[0]
    xb = x_ref[...].astype(BF16)
    v = _gelu_tanh(_dot(xb, wv_ref[...]))
    for h in range(A_GROUPS):
        cols = slice(h * A_GROUP_DIM, (h + 1) * A_GROUP_DIM)
        vn = _layer_norm(v[:, cols], lng_ref[:, cols], lnb_ref[:, cols]).astype(BF16)
        for c in range(tm // CHUNK):
            rows = slice(c * CHUNK, (c + 1) * CHUNK)
            mix_ref[rows, cols] = _dot(ws_ref[h], vn[rows, :]) + bias_ref[:, cols]
    u = _gelu_tanh(_dot(xb, wu_ref[...]))
    ya = _dot((u * mix_ref[...]).astype(BF16), wa_ref[...])
    gate = _sigmoid(_dot(xb, wg_ref[...]))
    o_ref[...] = (gate * ya).astype(BF16)


def _gmlp_branch(x2, w_u, w_v, w_ga, ln_g, ln_b, w_s, bias_full, w_a):
    T = x2.shape[0]
    tm = TOKEN_TILE
    return pl.pallas_call(
        _gmlp_kernel,
        grid=(T // tm,),
        in_specs=[
            pl.BlockSpec((tm, D_MODEL), lambda i: (i, 0)),
            _const_spec((D_MODEL, A_WIDTH)),
            _const_spec((D_MODEL, A_WIDTH)),
            _const_spec((D_MODEL, D_MODEL)),
            _const_spec((1, A_WIDTH)),
            _const_spec((1, A_WIDTH)),
            _const_spec((A_GROUPS, CHUNK, CHUNK)),
            _const_spec((CHUNK, A_WIDTH)),
            _const_spec((A_WIDTH, D_MODEL)),
        ],
        out_specs=pl.BlockSpec((tm, D_MODEL), lambda i: (i, 0)),
        out_shape=jax.ShapeDtypeStruct((T, D_MODEL), BF16),
        scratch_shapes=[pltpu.VMEM((tm, A_WIDTH), F32)],
        compiler_params=_params(("arbitrary",)),
        name="gmlp_branch",
    )(x2, w_u, w_v, w_ga, ln_g, ln_b, w_s, bias_full, w_a)


def _mla_prep_kernel(x_ref, posr_ref, posc_ref, wb_ref, qg_ref, kvg_ref, wuqt_ref, wuk_ref, wuvt_ref,
                     freqc_ref, freqr_ref, signc_ref, signr_ref,
                     qt_ref, k_ref, vt_ref, sgb_ref):
    xb = x_ref[...].astype(BF16)
    pb = _dot(xb, wb_ref[...])
    c_q = pb[:, :Q_LORA]
    c_kv = pb[:, Q_LORA:Q_LORA + KV_LORA]
    o = Q_LORA + KV_LORA
    kr = pb[:, o:o + LANES]
    kr_partner = pb[:, o + LANES:o + 2 * LANES]
    gb = pb[:, o + 2 * LANES:]
    sgb_ref[...] = _sigmoid(gb).astype(BF16)

    cqn = _rms_norm(c_q, qg_ref[...])
    ckvn = _rms_norm(c_kv, kvg_ref[...])
    cqn_t = cqn.T.astype(BF16)
    ckvn_t = ckvn.T.astype(BF16)

    scale = math.log2(math.e) / math.sqrt(QK_DIM)
    ang_t = freqc_ref[...] * posr_ref[0]
    table_t = jnp.concatenate(
        [jnp.cos(ang_t[:QK_ROPE]), signc_ref[QK_ROPE:] * jnp.sin(ang_t[QK_ROPE:])], axis=0) * scale
    q_t = _dot(wuqt_ref[...], cqn_t)
    for h in range(N_HEADS):
        blk = q_t[h * QK_PAD:(h + 1) * QK_PAD]
        qt_ref[0, h] = jnp.concatenate(
            [blk[:QK_NOPE] * scale, blk[QK_NOPE:] * table_t], axis=0).astype(BF16)

    v_t = _dot(wuvt_ref[...], ckvn_t)
    ones_rows = jnp.ones((V_ROWS - V_HEAD, v_t.shape[1]), F32)
    for h in range(N_HEADS):
        vt_ref[0, h] = jnp.concatenate([v_t[h * V_HEAD:(h + 1) * V_HEAD], ones_rows], axis=0).astype(BF16)

    ang = posc_ref[...] * freqr_ref[...]
    k_rope = (kr * jnp.cos(ang) + kr_partner * (signr_ref[...] * jnp.sin(ang))).astype(BF16)
    k_nope = _dot(ckvn.astype(BF16), wuk_ref[...])
    for h in range(N_HEADS):
        k_ref[0, h] = jnp.concatenate(
            [k_nope[:, h * QK_NOPE:(h + 1) * QK_NOPE].astype(BF16), k_rope], axis=1)


def _mla_prep(x2, pos_row, pos_col, w_b, q_g, kv_g, w_uqt, w_uk, w_uvt, freq_col, freq_row, sign_col, sign_row,
              B, S):
    T = x2.shape[0]
    tm = TOKEN_TILE
    spb = S // tm
    n_b = w_b.shape[1]
    head_map = lambda i: (i // spb, 0, 0, i % spb)
    return pl.pallas_call(
        _mla_prep_kernel,
        grid=(T // tm,),
        in_specs=[
            pl.BlockSpec((tm, D_MODEL), lambda i: (i, 0)),
            pl.BlockSpec((1, 1, tm), lambda i: (i, 0, 0)),
            pl.BlockSpec((tm, 1), lambda i: (i, 0)),
            _const_spec((D_MODEL, n_b)),
            _const_spec((1, Q_LORA)),
            _const_spec((1, KV_LORA)),
            _const_spec((N_HEADS * QK_PAD, Q_LORA)),
            _const_spec((KV_LORA, N_HEADS * QK_NOPE)),
            _const_spec((N_HEADS * V_HEAD, KV_LORA)),
            _const_spec((LANES, 1)),
            _const_spec((1, LANES)),
            _const_spec((LANES, 1)),
            _const_spec((1, LANES)),
        ],
        out_specs=[
            pl.BlockSpec((1, N_HEADS, QK_PAD, tm), head_map),
            pl.BlockSpec((1, N_HEADS, tm, QK_PAD), lambda i: (i // spb, 0, i % spb, 0)),
            pl.BlockSpec((1, N_HEADS, V_ROWS, tm), head_map),
            pl.BlockSpec((tm, D_MODEL), lambda i: (i, 0)),
        ],
        out_shape=[
            jax.ShapeDtypeStruct((B, N_HEADS, QK_PAD, S), BF16),
            jax.ShapeDtypeStruct((B, N_HEADS, S, QK_PAD), BF16),
            jax.ShapeDtypeStruct((B, N_HEADS, V_ROWS, S), BF16),
            jax.ShapeDtypeStruct((T, D_MODEL), BF16),
        ],
        compiler_params=_params(("arbitrary",)),
        name="mla_prep",
    )(x2, pos_row, pos_col, w_b, q_g, kv_g, w_uqt, w_uk, w_uvt, freq_col, freq_row, sign_col, sign_row)


def _attention_kernel(qt_ref, k_ref, vt_ref, o_ref, s0_ref, s1_ref, m_ref, acc_ref):
    S = k_ref.shape[2]
    n_chunks = S // KV_TILE
    q_t = qt_ref[0, 0]

    def scores(c, dst_ref):
        start = pl.multiple_of(c * KV_TILE, KV_TILE)
        dst_ref[...] = _dot(k_ref[0, 0, pl.ds(start, KV_TILE), :], q_t)

    def accumulate(c, src_ref):
        start = pl.multiple_of(c * KV_TILE, KV_TILE)
        s_t = src_ref[...]
        m = m_ref[...]
        m_new = jnp.maximum(m, jnp.max(s_t, axis=0, keepdims=True))
        p_t = jnp.exp2(s_t - m_new)
        alpha = jnp.exp2(m - m_new)
        m_ref[...] = m_new
        acc_ref[...] = alpha * acc_ref[...] + _dot(vt_ref[0, 0, :, pl.ds(start, KV_TILE)], p_t.astype(BF16))

    m_ref[...] = jnp.full(m_ref.shape, NEG_INF, F32)
    acc_ref[...] = jnp.zeros(acc_ref.shape, F32)
    scores(0, s0_ref)

    bufs = (s0_ref, s1_ref)

    def group(c0, prefetch_last):
        for i in range(KV_GROUP):
            if i + 1 < KV_GROUP or prefetch_last:
                scores(c0 + i + 1, bufs[(i + 1) % 2])
            accumulate(c0 + i, bufs[i % 2])

    def body(g, carry):
        group(g * KV_GROUP, True)
        return carry

    lax.fori_loop(0, n_chunks // KV_GROUP - 1, body, 0)
    group(n_chunks - KV_GROUP, False)
    o_ref[0] = (acc_ref[:V_HEAD] / acc_ref[V_HEAD:V_HEAD + 1]).T.astype(BF16)


def _attention(q_t, k, v_t):
    B, H, _, S = q_t.shape
    tq = min(Q_TILE, S)
    assert S % tq == 0 and S % (KV_TILE * KV_GROUP) == 0 and KV_GROUP % 2 == 0
    return pl.pallas_call(
        _attention_kernel,
        grid=(B, H, S // tq),
        in_specs=[
            pl.BlockSpec((1, 1, QK_PAD, tq), lambda b, h, i: (b, h, 0, i)),
            pl.BlockSpec((1, 1, S, QK_PAD), lambda b, h, i: (b, h, 0, 0)),
            pl.BlockSpec((1, 1, V_ROWS, S), lambda b, h, i: (b, h, 0, 0)),
        ],
        out_specs=pl.BlockSpec((1, tq, V_HEAD), lambda b, h, i: (b, i, h)),
        out_shape=jax.ShapeDtypeStruct((B, S, H * V_HEAD), BF16),
        scratch_shapes=[
            pltpu.VMEM((KV_TILE, tq), F32),
            pltpu.VMEM((KV_TILE, tq), F32),
            pltpu.VMEM((1, tq), F32),
            pltpu.VMEM((V_ROWS, tq), F32),
        ],
        compiler_params=_params(("arbitrary", "arbitrary", "arbitrary")),
        name="attention",
    )(q_t, k, v_t)


def _mix_route_kernel(x_ref, ga_ref, sgb_ref, o_ref, wbp_ref, wout_ref, g1_ref, b1_ref, wr_ref, rb_ref,
                      h1_ref, h1t_ref, idx_ref, rank_ref, gw_ref, cnt_ref, base_ref):
    tm = x_ref.shape[0]

    @pl.when(pl.program_id(0) == 0)
    def _():
        base_ref[...] = jnp.zeros_like(base_ref)

    yb = _dot(o_ref[...], wbp_ref[...])
    merged = ga_ref[...].astype(F32) + sgb_ref[...].astype(F32) * yb
    mixed = _dot(merged.astype(BF16), wout_ref[...])
    h1 = _layer_norm(DN_ALPHA * x_ref[...] + mixed, g1_ref[...], b1_ref[...])
    h1_ref[...] = h1
    _tile_rows_store(h1t_ref, 0, tm, h1)

    scores_t = _sigmoid(_dot(h1.astype(BF16), wr_ref[...])).T
    biased = scores_t + rb_ref[...]

    riota = lax.broadcasted_iota(jnp.int32, (GROUP_SIZE, tm), 0)
    blocks, gscore = [], []
    for g in range(N_GROUPS):
        blk = biased[g * GROUP_SIZE:(g + 1) * GROUP_SIZE]
        m1 = jnp.max(blk, axis=0, keepdims=True)
        i1 = jnp.min(jnp.where(blk == m1, riota, GROUP_SIZE), axis=0, keepdims=True)
        m2 = jnp.max(jnp.where(riota == i1, NEG_INF, blk), axis=0, keepdims=True)
        blocks.append(blk)
        gscore.append(m1 + m2)

    selected = [jnp.zeros((1, tm), F32) for _ in range(N_GROUPS)]
    for _ in range(TOPK_GROUPS):
        best = functools.reduce(jnp.maximum, gscore)
        first = functools.reduce(
            jnp.minimum, [jnp.where(gscore[g] == best, g, N_GROUPS) for g in range(N_GROUPS)])
        for g in range(N_GROUPS):
            hit = first == g
            selected[g] = jnp.where(hit, 1.0, selected[g])
            gscore[g] = jnp.where(hit, NEG_INF, gscore[g])

    masked = jnp.concatenate(
        [jnp.where(jnp.broadcast_to(selected[g], blocks[g].shape) > 0.5, blocks[g], NEG_INF)
         for g in range(N_GROUPS)], axis=0)

    eiota = lax.broadcasted_iota(jnp.int32, (N_EXPERTS, tm), 0)
    picks, weights = [], []
    for _ in range(TOP_K):
        best = jnp.max(masked, axis=0, keepdims=True)
        e = jnp.min(jnp.where(masked == best, eiota, N_EXPERTS), axis=0, keepdims=True)
        hit = eiota == e
        masked = jnp.where(hit, NEG_INF, masked)
        picks.append(e)
        weights.append(jnp.sum(jnp.where(hit, scores_t, 0.0), axis=0, keepdims=True))
    wsum = functools.reduce(lambda a, b: a + b, weights)

    chosen = functools.reduce(
        lambda a, b: a + b, [jnp.where(eiota == e, 1.0, 0.0) for e in picks])
    before = (lax.broadcasted_iota(jnp.int32, (tm, tm), 0)
              < lax.broadcasted_iota(jnp.int32, (tm, tm), 1)).astype(BF16)
    rank_full = _dot(chosen.astype(BF16), before) + base_ref[:, :1]
    new_base = base_ref[:, :1] + jnp.sum(chosen, axis=1, keepdims=True)
    base_ref[...] = jnp.broadcast_to(new_base, base_ref.shape)
    cnt_ref[...] = jnp.broadcast_to(new_base, cnt_ref.shape)

    kiota = lax.broadcasted_iota(jnp.int32, (TOP_K, tm), 0)
    wiota = lax.broadcasted_iota(jnp.int32, (LANES, tm), 0)
    idx_out = jnp.zeros((TOP_K, tm), jnp.int32)
    rank_out = jnp.zeros((TOP_K, tm), jnp.int32)
    gw_t = jnp.zeros((LANES, tm), F32)
    for k in range(TOP_K):
        r = jnp.sum(jnp.where(eiota == picks[k], rank_full, 0.0), axis=0, keepdims=True)
        idx_out = jnp.where(kiota == k, picks[k], idx_out)
        rank_out = jnp.where(kiota == k, r.astype(jnp.int32), rank_out)
        gw_t = jnp.where(wiota == k, weights[k] / wsum * ROUTED_SCALE, gw_t)
    idx_ref[...] = idx_out
    rank_ref[...] = rank_out
    gw_ref[...] = gw_t.T


def _mix_route(x2, ga, sgb, o, w_bp, w_out, g1, b1, w_r, r_bias):
    T = x2.shape[0]
    tm = TOKEN_TILE
    row = lambda i: (i, 0)
    col = lambda i: (0, i)
    return pl.pallas_call(
        _mix_route_kernel,
        grid=(T // tm,),
        in_specs=[
            pl.BlockSpec((tm, D_MODEL), row),
            pl.BlockSpec((tm, D_MODEL), row),
            pl.BlockSpec((tm, D_MODEL), row),
            pl.BlockSpec((tm, D_MODEL), row),
            _const_spec((D_MODEL, D_MODEL)),
            _const_spec((D_MODEL, D_MODEL)),
            _const_spec((1, D_MODEL)),
            _const_spec((1, D_MODEL)),
            _const_spec((D_MODEL, N_EXPERTS)),
            _const_spec((N_EXPERTS, 1)),
        ],
        out_specs=[
            pl.BlockSpec((tm, D_MODEL), row),
            pl.BlockSpec((tm * SUBLANES, LANES), row),
            pl.BlockSpec((TOP_K, tm), col),
            pl.BlockSpec((TOP_K, tm), col),
            pl.BlockSpec((tm, LANES), row),
            _const_spec((N_EXPERTS, LANES)),
        ],
        out_shape=[
            jax.ShapeDtypeStruct((T, D_MODEL), F32),
            jax.ShapeDtypeStruct((T * SUBLANES, LANES), F32),
            jax.ShapeDtypeStruct((TOP_K, T), jnp.int32),
            jax.ShapeDtypeStruct((TOP_K, T), jnp.int32),
            jax.ShapeDtypeStruct((T, LANES), F32),
            jax.ShapeDtypeStruct((N_EXPERTS, LANES), F32),
        ],
        scratch_shapes=[pltpu.VMEM((N_EXPERTS, LANES), F32)],
        compiler_params=_params(("arbitrary",)),
        name="mix_route",
    )(x2, ga, sgb, o, w_bp, w_out, g1, b1, w_r, r_bias)


def _slot_kernel(starts_ref, idx_ref, rank_ref, slot_ref):
    idx = idx_ref[...]

    def lookup(e, acc):
        return jnp.where(idx == e, starts_ref[e], acc)

    slot_ref[...] = lax.fori_loop(0, N_EXPERTS, lookup, jnp.zeros_like(idx), unroll=8) + rank_ref[...]


def _slot_index(starts, idx_t, rank_t):
    T = idx_t.shape[1]
    tl = min(SLOT_LANES, T)
    col = lambda i: (0, i)
    return pl.pallas_call(
        _slot_kernel,
        grid=(T // tl,),
        in_specs=[
            pl.BlockSpec(memory_space=pltpu.SMEM),
            pl.BlockSpec((TOP_K, tl), col),
            pl.BlockSpec((TOP_K, tl), col),
        ],
        out_specs=pl.BlockSpec((TOP_K, tl), col),
        out_shape=jax.ShapeDtypeStruct((TOP_K, T), jnp.int32),
        compiler_params=_params(("arbitrary",)),
        name="slot_index",
    )(starts, idx_t, rank_t)


def _token_tile(ref, token):
    return ref.at[pl.ds(pl.multiple_of(token * SUBLANES, SUBLANES), SUBLANES), :]


def _dispatch_kernel(slot_ref, h1t_ref, xs_ref, sem):
    tm = h1t_ref.shape[0] // SUBLANES

    def issue(t, carry):
        src = _token_tile(h1t_ref, t)
        for k in range(TOP_K):
            pltpu.make_async_copy(src, _token_tile(xs_ref, slot_ref[k * tm + t]), sem).start(priority=k % 2)
        return carry

    lax.fori_loop(0, tm, issue, 0, unroll=ISSUE_UNROLL)
    for _ in range(TOP_K):
        pltpu.make_async_copy(h1t_ref, xs_ref.at[pl.ds(0, tm * SUBLANES), :], sem).wait()


def _dispatch(slots_tiled, h1t):
    T = h1t.shape[0] // SUBLANES
    tm = TOKEN_TILE
    return pl.pallas_call(
        _dispatch_kernel,
        grid=(T // tm,),
        in_specs=[
            pl.BlockSpec((TOP_K * tm,), lambda i: (i,), memory_space=pltpu.SMEM),
            pl.BlockSpec((tm * SUBLANES, LANES), lambda i: (i, 0)),
        ],
        out_specs=pl.BlockSpec(memory_space=pl.ANY),
        out_shape=jax.ShapeDtypeStruct((T * TOP_K * SUBLANES, LANES), F32),
        scratch_shapes=[pltpu.SemaphoreType.DMA(())],
        compiler_params=_params(("arbitrary",)),
        name="dispatch",
    )(slots_tiled, h1t)


def _weight_copies(expert, slot, w_hbm_refs, w_buf_refs, sems):
    return [pltpu.make_async_copy(w_hbm.at[expert], w_buf.at[slot], sems.at[slot, n])
            for n, (w_hbm, w_buf) in enumerate(zip(w_hbm_refs, w_buf_refs))]


def _experts_kernel(blk_ref, exp_ref, lo_ref, hi_ref, next_ref, slot_ref,
                    xs_ref, wg_hbm, wu_hbm, wd_hbm, ys_ref,
                    wg_buf, wu_buf, wd_buf, wg_bf_ref, wu_bf_ref, wd_bf_ref, sems):
    i = pl.program_id(0)
    blk = blk_ref[i]
    lo = lo_ref[i]
    hi = hi_ref[i]
    expert = exp_ref[i]
    slot = slot_ref[i]
    prev = jnp.maximum(i - 1, 0)
    w_hbm_refs = (wg_hbm, wu_hbm, wd_hbm)
    w_buf_refs = (wg_buf, wu_buf, wd_buf)

    @pl.when(i == 0)
    def _():
        for copy in _weight_copies(expert, slot, w_hbm_refs, w_buf_refs, sems):
            copy.start()

    @pl.when((i == 0) | (blk_ref[prev] != blk))
    def _():
        ys_ref[...] = jnp.zeros_like(ys_ref)

    @pl.when((i == 0) | (exp_ref[prev] != expert))
    def _():
        following = next_ref[i]

        @pl.when(following < N_EXPERTS)
        def _():
            for copy in _weight_copies(following, 1 - slot, w_hbm_refs, w_buf_refs, sems):
                copy.start(priority=1)

        for copy in _weight_copies(expert, slot, w_hbm_refs, w_buf_refs, sems):
            copy.wait()
        wg_bf_ref[...] = wg_buf[slot].astype(BF16)
        wu_bf_ref[...] = wu_buf[slot].astype(BF16)
        wd_bf_ref[...] = wd_buf[slot].astype(BF16)

    @pl.when(hi > lo)
    def _():
        x = _tile_rows_load(xs_ref, 0, ROW_BLOCK).astype(BF16)
        gate = _dot(x, wg_bf_ref[...])
        up = _dot(x, wu_bf_ref[...])
        hidden = (gate * _sigmoid(gate) * up).astype(BF16)
        y = _dot(hidden, wd_bf_ref[...])
        rows = blk * ROW_BLOCK + lax.broadcasted_iota(jnp.int32, (ROW_BLOCK, 1), 0)
        keep = (rows >= lo) & (rows < hi)
        _tile_rows_store(ys_ref, 0, ROW_BLOCK, jnp.where(keep, y, _tile_rows_load(ys_ref, 0, ROW_BLOCK)))


def _experts(item_blk, item_exp, item_lo, item_hi, item_next, item_slot, xs, w_gate, w_up, w_down):
    n_rows = xs.shape[0] // SUBLANES
    n_items = item_blk.shape[0]
    block = (ROW_BLOCK * SUBLANES, LANES)
    grid_spec = pltpu.PrefetchScalarGridSpec(
        num_scalar_prefetch=6,
        grid=(n_items,),
        in_specs=[
            pl.BlockSpec(block, lambda i, b, *_: (b[i], 0)),
            pl.BlockSpec(memory_space=pl.ANY),
            pl.BlockSpec(memory_space=pl.ANY),
            pl.BlockSpec(memory_space=pl.ANY),
        ],
        out_specs=pl.BlockSpec(block, lambda i, b, *_: (b[i], 0)),
        scratch_shapes=[
            pltpu.VMEM((2, D_MODEL, D_EXPERT), F32),
            pltpu.VMEM((2, D_MODEL, D_EXPERT), F32),
            pltpu.VMEM((2, D_EXPERT, D_MODEL), F32),
            pltpu.VMEM((D_MODEL, D_EXPERT), BF16),
            pltpu.VMEM((D_MODEL, D_EXPERT), BF16),
            pltpu.VMEM((D_EXPERT, D_MODEL), BF16),
            pltpu.SemaphoreType.DMA((2, 3)),
        ],
    )
    return pl.pallas_call(
        _experts_kernel,
        grid_spec=grid_spec,
        out_shape=jax.ShapeDtypeStruct((n_rows * SUBLANES, LANES), F32),
        compiler_params=_params(("arbitrary",)),
        name="experts",
    )(item_blk, item_exp, item_lo, item_hi, item_next, item_slot, xs, w_gate, w_up, w_down)


def _combine_kernel(slot0_ref, slot_next_ref, gw_ref, h1_ref, ys_ref, wsg_ref, wsu_ref, wsd_ref,
                    g2_ref, b2_ref, out_ref, buf_ref, sems):
    tm = h1_ref.shape[0]
    i = pl.program_id(0)
    cur = i % 2

    def gather(slot_ref, buf, sem):
        def issue(t, carry):
            for k in range(TOP_K):
                pltpu.make_async_copy(
                    _token_tile(ys_ref, slot_ref[k * tm + t]), _token_tile(buf, k * tm + t), sem).start(priority=k % 2)
            return carry

        lax.fori_loop(0, tm, issue, 0, unroll=ISSUE_UNROLL)

    @pl.when(i == 0)
    def _():
        gather(slot0_ref, buf_ref.at[0], sems.at[0])

    @pl.when(i + 1 < pl.num_programs(0))
    def _():
        gather(slot_next_ref, buf_ref.at[1 - cur], sems.at[1 - cur])

    h1 = h1_ref[...]
    hb = h1.astype(BF16)
    gate = _dot(hb, wsg_ref[...])
    hidden = (gate * _sigmoid(gate) * _dot(hb, wsu_ref[...])).astype(BF16)
    moe = _dot(hidden, wsd_ref[...])

    buf = buf_ref.at[cur]
    for k in range(TOP_K):
        pltpu.make_async_copy(
            ys_ref.at[pl.ds(0, tm * SUBLANES), :], buf.at[pl.ds(k * tm * SUBLANES, tm * SUBLANES), :],
            sems.at[cur]).wait()

    gw = gw_ref[...]
    for k in range(TOP_K):
        moe = moe + gw[:, k:k + 1] * _tile_rows_load(buf, k * tm, tm)
    out_ref[...] = _layer_norm(DN_ALPHA * h1 + moe, g2_ref[...], b2_ref[...])


def _combine(slots_tiled, gw, h1, ys, w_sg, w_su, w_sd, g2, b2):
    T = h1.shape[0]
    tm = TOKEN_TILE
    row = lambda i: (i, 0)
    n_steps = T // tm
    return pl.pallas_call(
        _combine_kernel,
        grid=(n_steps,),
        in_specs=[
            pl.BlockSpec((TOP_K * tm,), lambda i: (0,), memory_space=pltpu.SMEM),
            pl.BlockSpec((TOP_K * tm,), lambda i: (jnp.minimum(i + 1, n_steps - 1),), memory_space=pltpu.SMEM),
            pl.BlockSpec((tm, LANES), row),
            pl.BlockSpec((tm, D_MODEL), row),
            pl.BlockSpec(memory_space=pl.ANY),
            _const_spec((D_MODEL, D_EXPERT)),
            _const_spec((D_MODEL, D_EXPERT)),
            _const_spec((D_EXPERT, D_MODEL)),
            _const_spec((1, D_MODEL)),
            _const_spec((1, D_MODEL)),
        ],
        out_specs=pl.BlockSpec((tm, D_MODEL), row),
        out_shape=jax.ShapeDtypeStruct((T, D_MODEL), F32),
        scratch_shapes=[pltpu.VMEM((2, TOP_K * tm * SUBLANES, LANES), F32), pltpu.SemaphoreType.DMA((2,))],
        compiler_params=_params(("arbitrary",)),
        name="combine",
    )(slots_tiled, slots_tiled, gw, h1, ys, w_sg, w_su, w_sd, g2, b2)


def _expert_work_items(counts, n_rows):
    n_blocks = n_rows // ROW_BLOCK
    n_items = n_blocks + N_EXPERTS
    ends = jnp.cumsum(counts)
    starts = ends - counts
    first_blk = starts // ROW_BLOCK
    last_blk = jnp.maximum(ends - 1, 0) // ROW_BLOCK
    per_expert = jnp.where(counts > 0, last_blk - first_blk + 1, 0)
    item_end = jnp.cumsum(per_expert)
    item_start = item_end - per_expert
    total = item_end[-1]
    i = jnp.arange(n_items, dtype=jnp.int32)
    live = i < total
    j = jnp.minimum(i, total - 1)[:, None]
    owner = (item_start[None, :] <= j) & (j < item_end[None, :])

    def pick(per_expert_values):
        return jnp.sum(jnp.where(owner, per_expert_values[None, :], 0), axis=1)

    expert_ids = jnp.arange(N_EXPERTS, dtype=jnp.int32)
    nonempty = counts > 0
    later = (expert_ids[None, :] > expert_ids[:, None]) & nonempty[None, :]
    following = jnp.min(jnp.where(later, expert_ids[None, :], N_EXPERTS), axis=1)
    buffer_slot = (jnp.cumsum(nonempty) - 1) % 2
    exp = pick(expert_ids)
    blk = pick(first_blk - item_start) + j[:, 0]
    lo = jnp.maximum(pick(starts), blk * ROW_BLOCK)
    hi = jnp.minimum(pick(ends), (blk + 1) * ROW_BLOCK)
    lo = jnp.where(live, lo, 0)
    hi = jnp.where(live, hi, 0)
    as_i32 = lambda a: a.astype(jnp.int32)
    return (as_i32(starts), as_i32(blk), as_i32(exp), as_i32(lo), as_i32(hi),
            as_i32(pick(following)), as_i32(pick(buffer_slot)))


def kernel(x, positions, w_in, gmlp_ln_g, gmlp_ln_b, w_spatial, b_spatial, w_a_proj, q_norm_g, w_uq, kv_norm_g, w_uk, w_uv, w_b_proj, w_out, ln1_g, ln1_b, w_router, router_bias, w_gate, w_up, w_down, w_sh_gate, w_sh_up, w_sh_down, ln2_g, ln2_b):
    B, S, D = x.shape
    T = B * S
    x2 = x.reshape(T, D)

    w = w_in[0]
    o_u, o_v, o_cq = 0, A_WIDTH, 2 * A_WIDTH
    o_ckv = o_cq + Q_LORA
    o_kr = o_ckv + KV_LORA
    o_ga = o_kr + QK_ROPE
    o_gb = o_ga + D_MODEL
    half_r = QK_ROPE // 2
    w_kr = w[:, o_kr:o_ga]
    w_kr_partner = jnp.concatenate([w_kr[:, half_r:], w_kr[:, :half_r]], axis=1)
    w_b = jnp.concatenate(
        [w[:, o_cq:o_kr], w_kr, w_kr, w_kr_partner, w_kr_partner, w[:, o_gb:]], axis=1).astype(BF16)
    uq = w_uq[0]
    uq_rope = uq[:, :, QK_NOPE:]
    uq_partner = jnp.concatenate([uq_rope[:, :, half_r:], uq_rope[:, :, :half_r]], axis=2)
    w_uqt = jnp.concatenate([uq, uq_partner], axis=2).reshape(Q_LORA, N_HEADS * QK_PAD).T.astype(BF16)
    w_ukm = w_uk[0].reshape(KV_LORA, N_HEADS * QK_NOPE).astype(BF16)
    w_uvt = w_uv[0].reshape(KV_LORA, N_HEADS * V_HEAD).T.astype(BF16)

    freq = ROPE_THETA ** (-jnp.arange(0, half_r, dtype=F32) * 2.0 / QK_ROPE)
    freq128 = jnp.tile(freq, LANES // half_r)
    sign128 = jnp.tile(jnp.concatenate([-jnp.ones((half_r,), F32), jnp.ones((half_r,), F32)]), LANES // QK_ROPE)
    pos_f = positions.astype(F32).reshape(T)
    bias_full = jnp.repeat(b_spatial[0].T, A_GROUP_DIM, axis=1)

    ga = _gmlp_branch(
        x2, w[:, o_u:o_v].astype(BF16), w[:, o_v:o_cq].astype(BF16), w[:, o_ga:o_gb].astype(BF16),
        gmlp_ln_g[0].reshape(1, A_WIDTH), gmlp_ln_b[0].reshape(1, A_WIDTH),
        w_spatial[0].astype(BF16), bias_full, w_a_proj[0].astype(BF16))

    q_t, k, v_t, sgb = _mla_prep(
        x2, pos_f.reshape(T // TOKEN_TILE, 1, TOKEN_TILE), pos_f.reshape(T, 1), w_b,
        q_norm_g[0].reshape(1, Q_LORA), kv_norm_g[0].reshape(1, KV_LORA), w_uqt, w_ukm, w_uvt,
        freq128.reshape(LANES, 1), freq128.reshape(1, LANES), sign128.reshape(LANES, 1), sign128.reshape(1, LANES),
        B, S)

    o = _attention(q_t, k, v_t).reshape(T, N_HEADS * V_HEAD)

    h1, h1t, idx_t, rank_t, gw, cnt = _mix_route(
        x2, ga, sgb, o, w_b_proj[0].astype(BF16), w_out[0].astype(BF16),
        ln1_g[0].reshape(1, D), ln1_b[0].reshape(1, D), w_router[0].astype(BF16),
        router_bias[0].reshape(N_EXPERTS, 1))

    counts = cnt[:, 0].astype(jnp.int32)
    starts, item_blk, item_exp, item_lo, item_hi, item_next, item_slot = _expert_work_items(counts, T * TOP_K)

    slots = _slot_index(starts, idx_t, rank_t)
    slots_tiled = slots.reshape(TOP_K, T // TOKEN_TILE, TOKEN_TILE).transpose(1, 0, 2).reshape(T * TOP_K)

    xs = _dispatch(slots_tiled, h1t)
    ys = _experts(item_blk, item_exp, item_lo, item_hi, item_next, item_slot, xs, w_gate[0], w_up[0], w_down[0])
    out = _combine(
        slots_tiled, gw, h1, ys, w_sh_gate[0].astype(BF16), w_sh_up[0].astype(BF16),
        w_sh_down[0].astype(BF16), ln2_g[0].reshape(1, D), ln2_b[0].reshape(1, D))
    return out.reshape(B, S, D)
```

```python
import functools
import math

import jax
import jax.numpy as jnp
from jax import lax
from jax.experimental import pallas as pl
from jax.experimental.pallas import tpu as pltpu

D_MODEL = 1024
CHUNK = 128
A_GROUPS = 8
A_GROUP_DIM = 128
A_WIDTH = A_GROUPS * A_GROUP_DIM
N_HEADS = 8
QK_NOPE = 128
QK_ROPE = 64
QK_DIM = QK_NOPE + QK_ROPE
V_HEAD = 128
V_ROWS = 144
Q_LORA = 384
KV_LORA = 256
ROPE_THETA = 10000.0
N_EXPERTS = 256
TOP_K = 8
N_GROUPS = 8
TOPK_GROUPS = 4
GROUP_SIZE = N_EXPERTS // N_GROUPS
D_EXPERT = 256
ROUTED_SCALE = 2.5
DN_ALPHA = 2.0 ** 0.25
LN_EPS = 1e-5
RMS_EPS = 1e-6

QK_PAD = 256
LANES = 128
TOKEN_TILE = 256
Q_TILE = 1024
KV_TILE = 512
KV_GROUP = 4
SLOT_LANES = 2048
ISSUE_UNROLL = 4
ROW_BLOCK = 256
SUBLANES = 8
VMEM_LIMIT = 56 * 1024 * 1024

F32 = jnp.float32
BF16 = jnp.bfloat16
NEG_INF = float("-inf")


def _dot(a, b):
    return jnp.dot(a, b, preferred_element_type=F32)


def _sigmoid(x):
    return 1.0 / (1.0 + jnp.exp(-x))


def _gelu_tanh(x):
    return 0.5 * x * (1.0 + jnp.tanh(math.sqrt(2.0 / math.pi) * (x + 0.044715 * (x * x * x))))


def _layer_norm(x, g, b):
    mu = jnp.mean(x, axis=-1, keepdims=True)
    d = x - mu
    var = jnp.mean(d * d, axis=-1, keepdims=True)
    return d * lax.rsqrt(var + LN_EPS) * g + b


def _rms_norm(x, g):
    return x * lax.rsqrt(jnp.mean(x * x, axis=-1, keepdims=True) + RMS_EPS) * g


def _tile_rows_load(ref, first, n):
    return jnp.concatenate(
        [ref[pl.ds(first * SUBLANES + j, n, stride=SUBLANES), :] for j in range(SUBLANES)], axis=1)


def _tile_rows_store(ref, first, n, value):
    for j in range(SUBLANES):
        ref[pl.ds(first * SUBLANES + j, n, stride=SUBLANES), :] = value[:, j * LANES:(j + 1) * LANES]


def _const_spec(shape):
    zeros = (0,) * len(shape)
    return pl.BlockSpec(shape, lambda *_: zeros)


def _params(semantics):
    return pltpu.CompilerParams(dimension_semantics=semantics, vmem_limit_bytes=VMEM_LIMIT)


def _gmlp_kernel(x_ref, wu_ref, wv_ref, wg_ref, lng_ref, lnb_ref, ws_ref, bias_ref, wa_ref, o_ref, mix_ref):
    tm = x_ref.shape[0]
    xb = x_ref[...].astype(BF16)
    v = _gelu_tanh(_dot(xb, wv_ref[...]))
    for h in range(A_GROUPS):
        cols = slice(h * A_GROUP_DIM, (h + 1) * A_GROUP_DIM)
        vn = _layer_norm(v[:, cols], lng_ref[:, cols], lnb_ref[:, cols]).astype(BF16)
        for c in range(tm // CHUNK):
            rows = slice(c * CHUNK, (c + 1) * CHUNK)
            mix_ref[rows, cols] = _dot(ws_ref[h], vn[rows, :]) + bias_ref[:, cols]
    u = _gelu_tanh(_dot(xb, wu_ref[...]))
    ya = _dot((u * mix_ref[...]).astype(BF16), wa_ref[...])
    gate = _sigmoid(_dot(xb, wg_ref[...]))
    o_ref[...] = (gate * ya).astype(BF16)


def _gmlp_branch(x2, w_u, w_v, w_ga, ln_g, ln_b, w_s, bias_full, w_a):
    T = x2.shape[0]
    tm = TOKEN_TILE
    return pl.pallas_call(
        _gmlp_kernel,
        grid=(T // tm,),
        in_specs=[
            pl.BlockSpec((tm, D_MODEL), lambda i: (i, 0)),
            _const_spec((D_MODEL, A_WIDTH)),
            _const_spec((D_MODEL, A_WIDTH)),
            _const_spec((D_MODEL, D_MODEL)),
            _const_spec((1, A_WIDTH)),
            _const_spec((1, A_WIDTH)),
            _const_spec((A_GROUPS, CHUNK, CHUNK)),
            _const_spec((CHUNK, A_WIDTH)),
            _const_spec((A_WIDTH, D_MODEL)),
        ],
        out_specs=pl.BlockSpec((tm, D_MODEL), lambda i: (i, 0)),
        out_shape=jax.ShapeDtypeStruct((T, D_MODEL), BF16),
        scratch_shapes=[pltpu.VMEM((tm, A_WIDTH), F32)],
        compiler_params=_params(("arbitrary",)),
        name="gmlp_branch",
    )(x2, w_u, w_v, w_ga, ln_g, ln_b, w_s, bias_full, w_a)


def _mla_prep_kernel(x_ref, posr_ref, posc_ref, wb_ref, qg_ref, kvg_ref, wuqt_ref, wuk_ref, wuvt_ref,
                     freqc_ref, freqr_ref, signc_ref, signr_ref,
                     qt_ref, k_ref, vt_ref, sgb_ref):
    xb = x_ref[...].astype(BF16)
    pb = _dot(xb, wb_ref[...])
    c_q = pb[:, :Q_LORA]
    c_kv = pb[:, Q_LORA:Q_LORA + KV_LORA]
    o = Q_LORA + KV_LORA
    kr = pb[:, o:o + LANES]
    kr_partner = pb[:, o + LANES:o + 2 * LANES]
    gb = pb[:, o + 2 * LANES:]
    sgb_ref[...] = _sigmoid(gb).astype(BF16)

    cqn = _rms_norm(c_q, qg_ref[...])
    ckvn = _rms_norm(c_kv, kvg_ref[...])
    cqn_t = cqn.T.astype(BF16)
    ckvn_t = ckvn.T.astype(BF16)

    scale = math.log2(math.e) / math.sqrt(QK_DIM)
    ang_t = freqc_ref[...] * posr_ref[0]
    table_t = jnp.concatenate(
        [jnp.cos(ang_t[:QK_ROPE]), signc_ref[QK_ROPE:] * jnp.sin(ang_t[QK_ROPE:])], axis=0) * scale
    q_t = _dot(wuqt_ref[...], cqn_t)
    for h in range(N_HEADS):
        blk = q_t[h * QK_PAD:(h + 1) * QK_PAD]
        qt_ref[0, h] = jnp.concatenate(
            [blk[:QK_NOPE] * scale, blk[QK_NOPE:] * table_t], axis=0).astype(BF16)

    v_t = _dot(wuvt_ref[...], ckvn_t)
    ones_rows = jnp.ones((V_ROWS - V_HEAD, v_t.shape[1]), F32)
    for h in range(N_HEADS):
        vt_ref[0, h] = jnp.concatenate([v_t[h * V_HEAD:(h + 1) * V_HEAD], ones_rows], axis=0).astype(BF16)

    ang = posc_ref[...] * freqr_ref[...]
    k_rope = (kr * jnp.cos(ang) + kr_partner * (signr_ref[...] * jnp.sin(ang))).astype(BF16)
    k_nope = _dot(ckvn.astype(BF16), wuk_ref[...])
    for h in range(N_HEADS):
        k_ref[0, h] = jnp.concatenate(
            [k_nope[:, h * QK_NOPE:(h + 1) * QK_NOPE].astype(BF16), k_rope], axis=1)


def _mla_prep(x2, pos_row, pos_col, w_b, q_g, kv_g, w_uqt, w_uk, w_uvt, freq_col, freq_row, sign_col, sign_row,
              B, S):
    T = x2.shape[0]
    tm = TOKEN_TILE
    spb = S // tm
    n_b = w_b.shape[1]
    head_map = lambda i: (i // spb, 0, 0, i % spb)
    return pl.pallas_call(
        _mla_prep_kernel,
        grid=(T // tm,),
        in_specs=[
            pl.BlockSpec((tm, D_MODEL), lambda i: (i, 0)),
            pl.BlockSpec((1, 1, tm), lambda i: (i, 0, 0)),
            pl.BlockSpec((tm, 1), lambda i: (i, 0)),
            _const_spec((D_MODEL, n_b)),
            _const_spec((1, Q_LORA)),
            _const_spec((1, KV_LORA)),
            _const_spec((N_HEADS * QK_PAD, Q_LORA)),
            _const_spec((KV_LORA, N_HEADS * QK_NOPE)),
            _const_spec((N_HEADS * V_HEAD, KV_LORA)),
            _const_spec((LANES, 1)),
            _const_spec((1, LANES)),
            _const_spec((LANES, 1)),
            _const_spec((1, LANES)),
        ],
        out_specs=[
            pl.BlockSpec((1, N_HEADS, QK_PAD, tm), head_map),
            pl.BlockSpec((1, N_HEADS, tm, QK_PAD), lambda i: (i // spb, 0, i % spb, 0)),
            pl.BlockSpec((1, N_HEADS, V_ROWS, tm), head_map),
            pl.BlockSpec((tm, D_MODEL), lambda i: (i, 0)),
        ],
        out_shape=[
            jax.ShapeDtypeStruct((B, N_HEADS, QK_PAD, S), BF16),
            jax.ShapeDtypeStruct((B, N_HEADS, S, QK_PAD), BF16),
            jax.ShapeDtypeStruct((B, N_HEADS, V_ROWS, S), BF16),
            jax.ShapeDtypeStruct((T, D_MODEL), BF16),
        ],
        compiler_params=_params(("arbitrary",)),
        name="mla_prep",
    )(x2, pos_row, pos_col, w_b, q_g, kv_g, w_uqt, w_uk, w_uvt, freq_col, freq_row, sign_col, sign_row)


def _attention_kernel(qt_ref, k_ref, vt_ref, o_ref, s0_ref, s1_ref, m_ref, acc_ref):
    S = k_ref.shape[2]
    n_chunks = S // KV_TILE
    q_t = qt_ref[0, 0]

    def scores(c, dst_ref):
        start = pl.multiple_of(c * KV_TILE, KV_TILE)
        dst_ref[...] = _dot(k_ref[0, 0, pl.ds(start, KV_TILE), :], q_t)

    def accumulate(c, src_ref):
        start = pl.multiple_of(c * KV_TILE, KV_TILE)
        s_t = src_ref[...]
        m = m_ref[...]
        m_new = jnp.maximum(m, jnp.max(s_t, axis=0, keepdims=True))
        p_t = jnp.exp2(s_t - m_new)
        alpha = jnp.exp2(m - m_new)
        m_ref[...] = m_new
        acc_ref[...] = alpha * acc_ref[...] + _dot(vt_ref[0, 0, :, pl.ds(start, KV_TILE)], p_t.astype(BF16))

    m_ref[...] = jnp.full(m_ref.shape, NEG_INF, F32)
    acc_ref[...] = jnp.zeros(acc_ref.shape, F32)
    scores(0, s0_ref)

    bufs = (s0_ref, s1_ref)

    def group(c0, prefetch_last):
        for i in range(KV_GROUP):
            if i + 1 < KV_GROUP or prefetch_last:
                scores(c0 + i + 1, bufs[(i + 1) % 2])
            accumulate(c0 + i, bufs[i % 2])

    def body(g, carry):
        group(g * KV_GROUP, True)
        return carry

    lax.fori_loop(0, n_chunks // KV_GROUP - 1, body, 0)
    group(n_chunks - KV_GROUP, False)
    o_ref[0] = (acc_ref[:V_HEAD] / acc_ref[V_HEAD:V_HEAD + 1]).T.astype(BF16)


def _attention(q_t, k, v_t):
    B, H, _, S = q_t.shape
    tq = min(Q_TILE, S)
    assert S % tq == 0 and S % (KV_TILE * KV_GROUP) == 0 and KV_GROUP % 2 == 0
    return pl.pallas_call(
        _attention_kernel,
        grid=(B, H, S // tq),
        in_specs=[
            pl.BlockSpec((1, 1, QK_PAD, tq), lambda b, h, i: (b, h, 0, i)),
            pl.BlockSpec((1, 1, S, QK_PAD), lambda b, h, i: (b, h, 0, 0)),
            pl.BlockSpec((1, 1, V_ROWS, S), lambda b, h, i: (b, h, 0, 0)),
        ],
        out_specs=pl.BlockSpec((1, tq, V_HEAD), lambda b, h, i: (b, i, h)),
        out_shape=jax.ShapeDtypeStruct((B, S, H * V_HEAD), BF16),
        scratch_shapes=[
            pltpu.VMEM((KV_TILE, tq), F32),
            pltpu.VMEM((KV_TILE, tq), F32),
            pltpu.VMEM((1, tq), F32),
            pltpu.VMEM((V_ROWS, tq), F32),
        ],
        compiler_params=_params(("arbitrary", "arbitrary", "arbitrary")),
        name="attention",
    )(q_t, k, v_t)


def _mix_route_kernel(x_ref, ga_ref, sgb_ref, o_ref, wbp_ref, wout_ref, g1_ref, b1_ref, wr_ref, rb_ref,
                      h1_ref, h1t_ref, idx_ref, rank_ref, gw_ref, cnt_ref, base_ref):
    tm = x_ref.shape[0]

    @pl.when(pl.program_id(0) == 0)
    def _():
        base_ref[...] = jnp.zeros_like(base_ref)

    yb = _dot(o_ref[...], wbp_ref[...])
    merged = ga_ref[...].astype(F32) + sgb_ref[...].astype(F32) * yb
    mixed = _dot(merged.astype(BF16), wout_ref[...])
    h1 = _layer_norm(DN_ALPHA * x_ref[...] + mixed, g1_ref[...], b1_ref[...])
    h1_ref[...] = h1
    _tile_rows_store(h1t_ref, 0, tm, h1)

    scores_t = _sigmoid(_dot(h1.astype(BF16), wr_ref[...])).T
    biased = scores_t + rb_ref[...]

    riota = lax.broadcasted_iota(jnp.int32, (GROUP_SIZE, tm), 0)
    blocks, gscore = [], []
    for g in range(N_GROUPS):
        blk = biased[g * GROUP_SIZE:(g + 1) * GROUP_SIZE]
        m1 = jnp.max(blk, axis=0, keepdims=True)
        i1 = jnp.min(jnp.where(blk == m1, riota, GROUP_SIZE), axis=0, keepdims=True)
        m2 = jnp.max(jnp.where(riota == i1, NEG_INF, blk), axis=0, keepdims=True)
        blocks.append(blk)
        gscore.append(m1 + m2)

    selected = [jnp.zeros((1, tm), F32) for _ in range(N_GROUPS)]
    for _ in range(TOPK_GROUPS):
        best = functools.reduce(jnp.maximum, gscore)
        first = functools.reduce(
            jnp.minimum, [jnp.where(gscore[g] == best, g, N_GROUPS) for g in range(N_GROUPS)])
        for g in range(N_GROUPS):
            hit = first == g
            selected[g] = jnp.where(hit, 1.0, selected[g])
            gscore[g] = jnp.where(hit, NEG_INF, gscore[g])

    masked = jnp.concatenate(
        [jnp.where(jnp.broadcast_to(selected[g], blocks[g].shape) > 0.5, blocks[g], NEG_INF)
         for g in range(N_GROUPS)], axis=0)

    eiota = lax.broadcasted_iota(jnp.int32, (N_EXPERTS, tm), 0)
    picks, weights = [], []
    for _ in range(TOP_K):
        best = jnp.max(masked, axis=0, keepdims=True)
        e = jnp.min(jnp.where(masked == best, eiota, N_EXPERTS), axis=0, keepdims=True)
        hit = eiota == e
        masked = jnp.where(hit, NEG_INF, masked)
        picks.append(e)
        weights.append(jnp.sum(jnp.where(hit, scores_t, 0.0), axis=0, keepdims=True))
    wsum = functools.reduce(lambda a, b: a + b, weights)

    chosen = functools.reduce(
        lambda a, b: a + b, [jnp.where(eiota == e, 1.0, 0.0) for e in picks])
    before = (lax.broadcasted_iota(jnp.int32, (tm, tm), 0)
              < lax.broadcasted_iota(jnp.int32, (tm, tm), 1)).astype(BF16)
    rank_full = _dot(chosen.astype(BF16), before) + base_ref[:, :1]
    new_base = base_ref[:, :1] + jnp.sum(chosen, axis=1, keepdims=True)
    base_ref[...] = jnp.broadcast_to(new_base, base_ref.shape)
    cnt_ref[...] = jnp.broadcast_to(new_base, cnt_ref.shape)

    kiota = lax.broadcasted_iota(jnp.int32, (TOP_K, tm), 0)
    wiota = lax.broadcasted_iota(jnp.int32, (LANES, tm), 0)
    idx_out = jnp.zeros((TOP_K, tm), jnp.int32)
    rank_out = jnp.zeros((TOP_K, tm), jnp.int32)
    gw_t = jnp.zeros((LANES, tm), F32)
    for k in range(TOP_K):
        r = jnp.sum(jnp.where(eiota == picks[k], rank_full, 0.0), axis=0, keepdims=True)
        idx_out = jnp.where(kiota == k, picks[k], idx_out)
        rank_out = jnp.where(kiota == k, r.astype(jnp.int32), rank_out)
        gw_t = jnp.where(wiota == k, weights[k] / wsum * ROUTED_SCALE, gw_t)
    idx_ref[...] = idx_out
    rank_ref[...] = rank_out
    gw_ref[...] = gw_t.T


def _mix_route(x2, ga, sgb, o, w_bp, w_out, g1, b1, w_r, r_bias):
    T = x2.shape[0]
    tm = TOKEN_TILE
    row = lambda i: (i, 0)
    col = lambda i: (0, i)
    return pl.pallas_call(
        _mix_route_kernel,
        grid=(T // tm,),
        in_specs=[
            pl.BlockSpec((tm, D_MODEL), row),
            pl.BlockSpec((tm, D_MODEL), row),
            pl.BlockSpec((tm, D_MODEL), row),
            pl.BlockSpec((tm, D_MODEL), row),
            _const_spec((D_MODEL, D_MODEL)),
            _const_spec((D_MODEL, D_MODEL)),
            _const_spec((1, D_MODEL)),
            _const_spec((1, D_MODEL)),
            _const_spec((D_MODEL, N_EXPERTS)),
            _const_spec((N_EXPERTS, 1)),
        ],
        out_specs=[
            pl.BlockSpec((tm, D_MODEL), row),
            pl.BlockSpec((tm * SUBLANES, LANES), row),
            pl.BlockSpec((TOP_K, tm), col),
            pl.BlockSpec((TOP_K, tm), col),
            pl.BlockSpec((tm, LANES), row),
            _const_spec((N_EXPERTS, LANES)),
        ],
        out_shape=[
            jax.ShapeDtypeStruct((T, D_MODEL), F32),
            jax.ShapeDtypeStruct((T * SUBLANES, LANES), F32),
            jax.ShapeDtypeStruct((TOP_K, T), jnp.int32),
            jax.ShapeDtypeStruct((TOP_K, T), jnp.int32),
            jax.ShapeDtypeStruct((T, LANES), F32),
            jax.ShapeDtypeStruct((N_EXPERTS, LANES), F32),
        ],
        scratch_shapes=[pltpu.VMEM((N_EXPERTS, LANES), F32)],
        compiler_params=_params(("arbitrary",)),
        name="mix_route",
    )(x2, ga, sgb, o, w_bp, w_out, g1, b1, w_r, r_bias)


def _slot_kernel(starts_ref, idx_ref, rank_ref, slot_ref):
    idx = idx_ref[...]

    def lookup(e, acc):
        return jnp.where(idx == e, starts_ref[e], acc)

    slot_ref[...] = lax.fori_loop(0, N_EXPERTS, lookup, jnp.zeros_like(idx), unroll=8) + rank_ref[...]


def _slot_index(starts, idx_t, rank_t):
    T = idx_t.shape[1]
    tl = min(SLOT_LANES, T)
    col = lambda i: (0, i)
    return pl.pallas_call(
        _slot_kernel,
        grid=(T // tl,),
        in_specs=[
            pl.BlockSpec(memory_space=pltpu.SMEM),
            pl.BlockSpec((TOP_K, tl), col),
            pl.BlockSpec((TOP_K, tl), col),
        ],
        out_specs=pl.BlockSpec((TOP_K, tl), col),
        out_shape=jax.ShapeDtypeStruct((TOP_K, T), jnp.int32),
        compiler_params=_params(("arbitrary",)),
        name="slot_index",
    )(starts, idx_t, rank_t)


def _token_tile(ref, token):
    return ref.at[pl.ds(pl.multiple_of(token * SUBLANES, SUBLANES), SUBLANES), :]


def _dispatch_kernel(slot_ref, h1t_ref, xs_ref, sem):
    tm = h1t_ref.shape[0] // SUBLANES

    def issue(t, carry):
        src = _token_tile(h1t_ref, t)
        for k in range(TOP_K):
            pltpu.make_async_copy(src, _token_tile(xs_ref, slot_ref[k * tm + t]), sem).start(priority=k % 2)
        return carry

    lax.fori_loop(0, tm, issue, 0, unroll=ISSUE_UNROLL)
    for _ in range(TOP_K):
        pltpu.make_async_copy(h1t_ref, xs_ref.at[pl.ds(0, tm * SUBLANES), :], sem).wait()


def _dispatch(slots_tiled, h1t):
    T = h1t.shape[0] // SUBLANES
    tm = TOKEN_TILE
    return pl.pallas_call(
        _dispatch_kernel,
        grid=(T // tm,),
        in_specs=[
            pl.BlockSpec((TOP_K * tm,), lambda i: (i,), memory_space=pltpu.SMEM),
            pl.BlockSpec((tm * SUBLANES, LANES), lambda i: (i, 0)),
        ],
        out_specs=pl.BlockSpec(memory_space=pl.ANY),
        out_shape=jax.ShapeDtypeStruct((T * TOP_K * SUBLANES, LANES), F32),
        scratch_shapes=[pltpu.SemaphoreType.DMA(())],
        compiler_params=_params(("arbitrary",)),
        name="dispatch",
    )(slots_tiled, h1t)


def _weight_copies(expert, slot, w_hbm_refs, w_buf_refs, sems):
    return [pltpu.make_async_copy(w_hbm.at[expert], w_buf.at[slot], sems.at[slot, n])
            for n, (w_hbm, w_buf) in enumerate(zip(w_hbm_refs, w_buf_refs))]


def _experts_kernel(first_ref, exp_ref, lo_ref, hi_ref, next_ref, slot_ref,
                    xs_ref, wg_hbm, wu_hbm, wd_hbm, ys_ref,
                    x_ref, acc_ref, wg_buf, wu_buf, wd_buf, wg_bf_ref, wu_bf_ref, wd_bf_ref, sems):
    blk = pl.program_id(0)
    w_hbm_refs = (wg_hbm, wu_hbm, wd_hbm)
    w_buf_refs = (wg_buf, wu_buf, wd_buf)
    x_ref[...] = _tile_rows_load(xs_ref, 0, ROW_BLOCK).astype(BF16)
    acc_ref[...] = jnp.zeros_like(acc_ref)
    rows = blk * ROW_BLOCK + lax.broadcasted_iota(jnp.int32, (ROW_BLOCK, 1), 0)

    def item(i, carry):
        expert = exp_ref[i]
        slot = slot_ref[i]

        @pl.when(i == 0)
        def _():
            for copy in _weight_copies(expert, slot, w_hbm_refs, w_buf_refs, sems):
                copy.start()

        @pl.when((i == 0) | (exp_ref[jnp.maximum(i - 1, 0)] != expert))
        def _():
            following = next_ref[i]

            @pl.when(following < N_EXPERTS)
            def _():
                for copy in _weight_copies(following, 1 - slot, w_hbm_refs, w_buf_refs, sems):
                    copy.start(priority=1)

            for copy in _weight_copies(expert, slot, w_hbm_refs, w_buf_refs, sems):
                copy.wait()
            wg_bf_ref[...] = wg_buf[slot].astype(BF16)
            wu_bf_ref[...] = wu_buf[slot].astype(BF16)
            wd_bf_ref[...] = wd_buf[slot].astype(BF16)

        x = x_ref[...]
        gate = _dot(x, wg_bf_ref[...])
        up = _dot(x, wu_bf_ref[...])
        hidden = (gate * _sigmoid(gate) * up).astype(BF16)
        y = _dot(hidden, wd_bf_ref[...])
        keep = (rows >= lo_ref[i]) & (rows < hi_ref[i])
        acc_ref[...] = jnp.where(keep, y, acc_ref[...])
        return carry

    lax.fori_loop(first_ref[blk], first_ref[blk + 1], item, 0)
    _tile_rows_store(ys_ref, 0, ROW_BLOCK, acc_ref[...])


def _experts(block_first, item_exp, item_lo, item_hi, item_next, item_slot, xs, w_gate, w_up, w_down):
    n_rows = xs.shape[0] // SUBLANES
    block = (ROW_BLOCK * SUBLANES, LANES)
    grid_spec = pltpu.PrefetchScalarGridSpec(
        num_scalar_prefetch=6,
        grid=(n_rows // ROW_BLOCK,),
        in_specs=[
            pl.BlockSpec(block, lambda b, *_: (b, 0)),
            pl.BlockSpec(memory_space=pl.ANY),
            pl.BlockSpec(memory_space=pl.ANY),
            pl.BlockSpec(memory_space=pl.ANY),
        ],
        out_specs=pl.BlockSpec(block, lambda b, *_: (b, 0)),
        scratch_shapes=[
            pltpu.VMEM((ROW_BLOCK, D_MODEL), BF16),
            pltpu.VMEM((ROW_BLOCK, D_MODEL), F32),
            pltpu.VMEM((2, D_MODEL, D_EXPERT), F32),
            pltpu.VMEM((2, D_MODEL, D_EXPERT), F32),
            pltpu.VMEM((2, D_EXPERT, D_MODEL), F32),
            pltpu.VMEM((D_MODEL, D_EXPERT), BF16),
            pltpu.VMEM((D_MODEL, D_EXPERT), BF16),
            pltpu.VMEM((D_EXPERT, D_MODEL), BF16),
            pltpu.SemaphoreType.DMA((2, 3)),
        ],
    )
    return pl.pallas_call(
        _experts_kernel,
        grid_spec=grid_spec,
        out_shape=jax.ShapeDtypeStruct((n_rows * SUBLANES, LANES), F32),
        compiler_params=_params(("arbitrary",)),
        name="experts",
    )(block_first, item_exp, item_lo, item_hi, item_next, item_slot, xs, w_gate, w_up, w_down)


def _combine_kernel(slot0_ref, slot_next_ref, gw_ref, h1_ref, ys_ref, wsg_ref, wsu_ref, wsd_ref,
                    g2_ref, b2_ref, out_ref, buf_ref, sems):
    tm = h1_ref.shape[0]
    i = pl.program_id(0)
    cur = i % 2

    def gather(slot_ref, buf, sem):
        def issue(t, carry):
            for k in range(TOP_K):
                pltpu.make_async_copy(
                    _token_tile(ys_ref, slot_ref[k * tm + t]), _token_tile(buf, k * tm + t), sem).start(priority=k % 2)
            return carry

        lax.fori_loop(0, tm, issue, 0, unroll=ISSUE_UNROLL)

    @pl.when(i == 0)
    def _():
        gather(slot0_ref, buf_ref.at[0], sems.at[0])

    @pl.when(i + 1 < pl.num_programs(0))
    def _():
        gather(slot_next_ref, buf_ref.at[1 - cur], sems.at[1 - cur])

    h1 = h1_ref[...]
    hb = h1.astype(BF16)
    gate = _dot(hb, wsg_ref[...])
    hidden = (gate * _sigmoid(gate) * _dot(hb, wsu_ref[...])).astype(BF16)
    moe = _dot(hidden, wsd_ref[...])

    buf = buf_ref.at[cur]
    for k in range(TOP_K):
        pltpu.make_async_copy(
            ys_ref.at[pl.ds(0, tm * SUBLANES), :], buf.at[pl.ds(k * tm * SUBLANES, tm * SUBLANES), :],
            sems.at[cur]).wait()

    gw = gw_ref[...]
    for k in range(TOP_K):
        moe = moe + gw[:, k:k + 1] * _tile_rows_load(buf, k * tm, tm)
    out_ref[...] = _layer_norm(DN_ALPHA * h1 + moe, g2_ref[...], b2_ref[...])


def _combine(slots_tiled, gw, h1, ys, w_sg, w_su, w_sd, g2, b2):
    T = h1.shape[0]
    tm = TOKEN_TILE
    row = lambda i: (i, 0)
    n_steps = T // tm
    return pl.pallas_call(
        _combine_kernel,
        grid=(n_steps,),
        in_specs=[
            pl.BlockSpec((TOP_K * tm,), lambda i: (0,), memory_space=pltpu.SMEM),
            pl.BlockSpec((TOP_K * tm,), lambda i: (jnp.minimum(i + 1, n_steps - 1),), memory_space=pltpu.SMEM),
            pl.BlockSpec((tm, LANES), row),
            pl.BlockSpec((tm, D_MODEL), row),
            pl.BlockSpec(memory_space=pl.ANY),
            _const_spec((D_MODEL, D_EXPERT)),
            _const_spec((D_MODEL, D_EXPERT)),
            _const_spec((D_EXPERT, D_MODEL)),
            _const_spec((1, D_MODEL)),
            _const_spec((1, D_MODEL)),
        ],
        out_specs=pl.BlockSpec((tm, D_MODEL), row),
        out_shape=jax.ShapeDtypeStruct((T, D_MODEL), F32),
        scratch_shapes=[pltpu.VMEM((2, TOP_K * tm * SUBLANES, LANES), F32), pltpu.SemaphoreType.DMA((2,))],
        compiler_params=_params(("arbitrary",)),
        name="combine",
    )(slots_tiled, slots_tiled, gw, h1, ys, w_sg, w_su, w_sd, g2, b2)


def _expert_work_items(counts, n_rows):
    n_blocks = n_rows // ROW_BLOCK
    n_items = n_blocks + N_EXPERTS
    ends = jnp.cumsum(counts)
    starts = ends - counts
    first_blk = starts // ROW_BLOCK
    last_blk = jnp.maximum(ends - 1, 0) // ROW_BLOCK
    per_expert = jnp.where(counts > 0, last_blk - first_blk + 1, 0)
    item_end = jnp.cumsum(per_expert)
    item_start = item_end - per_expert
    total = item_end[-1]
    i = jnp.arange(n_items, dtype=jnp.int32)
    live = i < total
    j = jnp.minimum(i, total - 1)[:, None]
    owner = (item_start[None, :] <= j) & (j < item_end[None, :])

    def pick(per_expert_values):
        return jnp.sum(jnp.where(owner, per_expert_values[None, :], 0), axis=1)

    expert_ids = jnp.arange(N_EXPERTS, dtype=jnp.int32)
    nonempty = counts > 0
    later = (expert_ids[None, :] > expert_ids[:, None]) & nonempty[None, :]
    following = jnp.min(jnp.where(later, expert_ids[None, :], N_EXPERTS), axis=1)
    buffer_slot = (jnp.cumsum(nonempty) - 1) % 2
    exp = pick(expert_ids)
    blk = pick(first_blk - item_start) + j[:, 0]
    lo = jnp.maximum(pick(starts), blk * ROW_BLOCK)
    hi = jnp.minimum(pick(ends), (blk + 1) * ROW_BLOCK)
    lo = jnp.where(live, lo, 0)
    hi = jnp.where(live, hi, 0)
    block_ids = jnp.arange(n_blocks + 1, dtype=jnp.int32)
    block_first = jnp.sum(live[None, :] & (blk[None, :] < block_ids[:, None]), axis=1)
    as_i32 = lambda a: a.astype(jnp.int32)
    return (as_i32(starts), as_i32(block_first), as_i32(exp), as_i32(lo), as_i32(hi),
            as_i32(pick(following)), as_i32(pick(buffer_slot)))


def kernel(x, positions, w_in, gmlp_ln_g, gmlp_ln_b, w_spatial, b_spatial, w_a_proj, q_norm_g, w_uq, kv_norm_g, w_uk, w_uv, w_b_proj, w_out, ln1_g, ln1_b, w_router, router_bias, w_gate, w_up, w_down, w_sh_gate, w_sh_up, w_sh_down, ln2_g, ln2_b):
    B, S, D = x.shape
    T = B * S
    x2 = x.reshape(T, D)

    w = w_in[0]
    o_u, o_v, o_cq = 0, A_WIDTH, 2 * A_WIDTH
    o_ckv = o_cq + Q_LORA
    o_kr = o_ckv + KV_LORA
    o_ga = o_kr + QK_ROPE
    o_gb = o_ga + D_MODEL
    half_r = QK_ROPE // 2
    w_kr = w[:, o_kr:o_ga]
    w_kr_partner = jnp.concatenate([w_kr[:, half_r:], w_kr[:, :half_r]], axis=1)
    w_b = jnp.concatenate(
        [w[:, o_cq:o_kr], w_kr, w_kr, w_kr_partner, w_kr_partner, w[:, o_gb:]], axis=1).astype(BF16)
    uq = w_uq[0]
    uq_rope = uq[:, :, QK_NOPE:]
    uq_partner = jnp.concatenate([uq_rope[:, :, half_r:], uq_rope[:, :, :half_r]], axis=2)
    w_uqt = jnp.concatenate([uq, uq_partner], axis=2).reshape(Q_LORA, N_HEADS * QK_PAD).T.astype(BF16)
    w_ukm = w_uk[0].reshape(KV_LORA, N_HEADS * QK_NOPE).astype(BF16)
    w_uvt = w_uv[0].reshape(KV_LORA, N_HEADS * V_HEAD).T.astype(BF16)

    freq = ROPE_THETA ** (-jnp.arange(0, half_r, dtype=F32) * 2.0 / QK_ROPE)
    freq128 = jnp.tile(freq, LANES // half_r)
    sign128 = jnp.tile(jnp.concatenate([-jnp.ones((half_r,), F32), jnp.ones((half_r,), F32)]), LANES // QK_ROPE)
    pos_f = positions.astype(F32).reshape(T)
    bias_full = jnp.repeat(b_spatial[0].T, A_GROUP_DIM, axis=1)

    ga = _gmlp_branch(
        x2, w[:, o_u:o_v].astype(BF16), w[:, o_v:o_cq].astype(BF16), w[:, o_ga:o_gb].astype(BF16),
        gmlp_ln_g[0].reshape(1, A_WIDTH), gmlp_ln_b[0].reshape(1, A_WIDTH),
        w_spatial[0].astype(BF16), bias_full, w_a_proj[0].astype(BF16))

    q_t, k, v_t, sgb = _mla_prep(
        x2, pos_f.reshape(T // TOKEN_TILE, 1, TOKEN_TILE), pos_f.reshape(T, 1), w_b,
        q_norm_g[0].reshape(1, Q_LORA), kv_norm_g[0].reshape(1, KV_LORA), w_uqt, w_ukm, w_uvt,
        freq128.reshape(LANES, 1), freq128.reshape(1, LANES), sign128.reshape(LANES, 1), sign128.reshape(1, LANES),
        B, S)

    o = _attention(q_t, k, v_t).reshape(T, N_HEADS * V_HEAD)

    h1, h1t, idx_t, rank_t, gw, cnt = _mix_route(
        x2, ga, sgb, o, w_b_proj[0].astype(BF16), w_out[0].astype(BF16),
        ln1_g[0].reshape(1, D), ln1_b[0].reshape(1, D), w_router[0].astype(BF16),
        router_bias[0].reshape(N_EXPERTS, 1))

    counts = cnt[:, 0].astype(jnp.int32)
    starts, block_first, item_exp, item_lo, item_hi, item_next, item_slot = _expert_work_items(counts, T * TOP_K)

    slots = _slot_index(starts, idx_t, rank_t)
    slots_tiled = slots.reshape(TOP_K, T // TOKEN_TILE, TOKEN_TILE).transpose(1, 0, 2).reshape(T * TOP_K)

    xs = _dispatch(slots_tiled, h1t)
    ys = _experts(block_first, item_exp, item_lo, item_hi, item_next, item_slot, xs, w_gate[0], w_up[0], w_down[0])
    out = _combine(
        slots_tiled, gw, h1, ys, w_sh_gate[0].astype(BF16), w_sh_up[0].astype(BF16),
        w_sh_down[0].astype(BF16), ln2_g[0].reshape(1, D), ln2_b[0].reshape(1, D))
    return out.reshape(B, S, D)
```

```python
import functools
import math

import jax
import jax.numpy as jnp
from jax import lax
from jax.experimental import pallas as pl
from jax.experimental.pallas import tpu as pltpu

D_MODEL = 1024
CHUNK = 128
A_GROUPS = 8
A_GROUP_DIM = 128
A_WIDTH = A_GROUPS * A_GROUP_DIM
N_HEADS = 8
QK_NOPE = 128
QK_ROPE = 64
QK_DIM = QK_NOPE + QK_ROPE
V_HEAD = 128
V_ROWS = 144
Q_LORA = 384
KV_LORA = 256
ROPE_THETA = 10000.0
N_EXPERTS = 256
TOP_K = 8
N_GROUPS = 8
TOPK_GROUPS = 4
GROUP_SIZE = N_EXPERTS // N_GROUPS
D_EXPERT = 256
ROUTED_SCALE = 2.5
DN_ALPHA = 2.0 ** 0.25
LN_EPS = 1e-5
RMS_EPS = 1e-6

QK_PAD = 256
LANES = 128
TOKEN_TILE = 512
Q_TILE = 1024
KV_TILE = 512
KV_GROUP = 4
SLOT_LANES = 2048
ISSUE_UNROLL = 4
ROW_BLOCK = 256
X_RING = 3
SUBLANES = 8
VMEM_LIMIT = 56 * 1024 * 1024

F32 = jnp.float32
BF16 = jnp.bfloat16
NEG_INF = float("-inf")


def _dot(a, b):
    return jnp.dot(a, b, preferred_element_type=F32)


def _sigmoid(x):
    return 1.0 / (1.0 + jnp.exp(-x))


def _gelu_tanh(x):
    return 0.5 * x * (1.0 + jnp.tanh(math.sqrt(2.0 / math.pi) * (x + 0.044715 * (x * x * x))))


def _layer_norm(x, g, b):
    mu = jnp.mean(x, axis=-1, keepdims=True)
    d = x - mu
    var = jnp.mean(d * d, axis=-1, keepdims=True)
    return d * lax.rsqrt(var + LN_EPS) * g + b


def _rms_norm(x, g):
    return x * lax.rsqrt(jnp.mean(x * x, axis=-1, keepdims=True) + RMS_EPS) * g


def _tile_rows_load(ref, first, n):
    return jnp.concatenate(
        [ref[pl.ds(first * SUBLANES + j, n, stride=SUBLANES), :] for j in range(SUBLANES)], axis=1)


def _tile_rows_store(ref, first, n, value):
    for j in range(SUBLANES):
        ref[pl.ds(first * SUBLANES + j, n, stride=SUBLANES), :] = value[:, j * LANES:(j + 1) * LANES]


def _const_spec(shape):
    zeros = (0,) * len(shape)
    return pl.BlockSpec(shape, lambda *_: zeros)


def _params(semantics):
    return pltpu.CompilerParams(dimension_semantics=semantics, vmem_limit_bytes=VMEM_LIMIT)


def _gmlp_kernel(x_ref, wu_ref, wv_ref, wg_ref, lng_ref, lnb_ref, ws_ref, bias_ref, wa_ref, o_ref, mix_ref):
    tm = x_ref.shape[0]
    xb = x_ref[...].astype(BF16)
    v = _gelu_tanh(_dot(xb, wv_ref[...]))
    for h in range(A_GROUPS):
        cols = slice(h * A_GROUP_DIM, (h + 1) * A_GROUP_DIM)
        vn = _layer_norm(v[:, cols], lng_ref[:, cols], lnb_ref[:, cols]).astype(BF16)
        n_chunks = tm // CHUNK
        side_by_side = jnp.concatenate([vn[c * CHUNK:(c + 1) * CHUNK, :] for c in range(n_chunks)], axis=1)
        mixed = _dot(ws_ref[h], side_by_side)
        for c in range(n_chunks):
            mix_ref[c * CHUNK:(c + 1) * CHUNK, cols] = (
                mixed[:, c * A_GROUP_DIM:(c + 1) * A_GROUP_DIM] + bias_ref[:, cols])
    u = _gelu_tanh(_dot(xb, wu_ref[...]))
    ya = _dot((u * mix_ref[...]).astype(BF16), wa_ref[...])
    gate = _sigmoid(_dot(xb, wg_ref[...]))
    o_ref[...] = (gate * ya).astype(BF16)


def _gmlp_branch(x2, w_u, w_v, w_ga, ln_g, ln_b, w_s, bias_full, w_a):
    T = x2.shape[0]
    tm = TOKEN_TILE
    return pl.pallas_call(
        _gmlp_kernel,
        grid=(T // tm,),
        in_specs=[
            pl.BlockSpec((tm, D_MODEL), lambda i: (i, 0)),
            _const_spec((D_MODEL, A_WIDTH)),
            _const_spec((D_MODEL, A_WIDTH)),
            _const_spec((D_MODEL, D_MODEL)),
            _const_spec((1, A_WIDTH)),
            _const_spec((1, A_WIDTH)),
            _const_spec((A_GROUPS, CHUNK, CHUNK)),
            _const_spec((CHUNK, A_WIDTH)),
            _const_spec((A_WIDTH, D_MODEL)),
        ],
        out_specs=pl.BlockSpec((tm, D_MODEL), lambda i: (i, 0)),
        out_shape=jax.ShapeDtypeStruct((T, D_MODEL), BF16),
        scratch_shapes=[pltpu.VMEM((tm, A_WIDTH), F32)],
        compiler_params=_params(("arbitrary",)),
        name="gmlp_branch",
    )(x2, w_u, w_v, w_ga, ln_g, ln_b, w_s, bias_full, w_a)


def _mla_prep_kernel(x_ref, posr_ref, posc_ref, wb_ref, qg_ref, kvg_ref, wuqt_ref, wuk_ref, wuvt_ref,
                     freqc_ref, freqr_ref, signc_ref, signr_ref,
                     qt_ref, k_ref, vt_ref, sgb_ref):
    xb = x_ref[...].astype(BF16)
    pb = _dot(xb, wb_ref[...])
    c_q = pb[:, :Q_LORA]
    c_kv = pb[:, Q_LORA:Q_LORA + KV_LORA]
    o = Q_LORA + KV_LORA
    kr = pb[:, o:o + LANES]
    kr_partner = pb[:, o + LANES:o + 2 * LANES]
    gb = pb[:, o + 2 * LANES:]
    sgb_ref[...] = _sigmoid(gb).astype(BF16)

    cqn = _rms_norm(c_q, qg_ref[...])
    ckvn = _rms_norm(c_kv, kvg_ref[...])
    cqn_t = cqn.T.astype(BF16)
    ckvn_t = ckvn.T.astype(BF16)

    scale = math.log2(math.e) / math.sqrt(QK_DIM)
    ang_t = freqc_ref[...] * posr_ref[0]
    table_t = jnp.concatenate(
        [jnp.cos(ang_t[:QK_ROPE]), signc_ref[QK_ROPE:] * jnp.sin(ang_t[QK_ROPE:])], axis=0) * scale
    q_t = _dot(wuqt_ref[...], cqn_t)
    for h in range(N_HEADS):
        blk = q_t[h * QK_PAD:(h + 1) * QK_PAD]
        qt_ref[0, h] = jnp.concatenate(
            [blk[:QK_NOPE] * scale, blk[QK_NOPE:] * table_t], axis=0).astype(BF16)

    v_t = _dot(wuvt_ref[...], ckvn_t)
    ones_rows = jnp.ones((V_ROWS - V_HEAD, v_t.shape[1]), F32)
    for h in range(N_HEADS):
        vt_ref[0, h] = jnp.concatenate([v_t[h * V_HEAD:(h + 1) * V_HEAD], ones_rows], axis=0).astype(BF16)

    ang = posc_ref[...] * freqr_ref[...]
    k_rope = (kr * jnp.cos(ang) + kr_partner * (signr_ref[...] * jnp.sin(ang))).astype(BF16)
    k_nope = _dot(ckvn.astype(BF16), wuk_ref[...])
    for h in range(N_HEADS):
        k_ref[0, h] = jnp.concatenate(
            [k_nope[:, h * QK_NOPE:(h + 1) * QK_NOPE].astype(BF16), k_rope], axis=1)


def _mla_prep(x2, pos_row, pos_col, w_b, q_g, kv_g, w_uqt, w_uk, w_uvt, freq_col, freq_row, sign_col, sign_row,
              B, S):
    T = x2.shape[0]
    tm = TOKEN_TILE
    spb = S // tm
    n_b = w_b.shape[1]
    head_map = lambda i: (i // spb, 0, 0, i % spb)
    return pl.pallas_call(
        _mla_prep_kernel,
        grid=(T // tm,),
        in_specs=[
            pl.BlockSpec((tm, D_MODEL), lambda i: (i, 0)),
            pl.BlockSpec((1, 1, tm), lambda i: (i, 0, 0)),
            pl.BlockSpec((tm, 1), lambda i: (i, 0)),
            _const_spec((D_MODEL, n_b)),
            _const_spec((1, Q_LORA)),
            _const_spec((1, KV_LORA)),
            _const_spec((N_HEADS * QK_PAD, Q_LORA)),
            _const_spec((KV_LORA, N_HEADS * QK_NOPE)),
            _const_spec((N_HEADS * V_HEAD, KV_LORA)),
            _const_spec((LANES, 1)),
            _const_spec((1, LANES)),
            _const_spec((LANES, 1)),
            _const_spec((1, LANES)),
        ],
        out_specs=[
            pl.BlockSpec((1, N_HEADS, QK_PAD, tm), head_map),
            pl.BlockSpec((1, N_HEADS, tm, QK_PAD), lambda i: (i // spb, 0, i % spb, 0)),
            pl.BlockSpec((1, N_HEADS, V_ROWS, tm), head_map),
            pl.BlockSpec((tm, D_MODEL), lambda i: (i, 0)),
        ],
        out_shape=[
            jax.ShapeDtypeStruct((B, N_HEADS, QK_PAD, S), BF16),
            jax.ShapeDtypeStruct((B, N_HEADS, S, QK_PAD), BF16),
            jax.ShapeDtypeStruct((B, N_HEADS, V_ROWS, S), BF16),
            jax.ShapeDtypeStruct((T, D_MODEL), BF16),
        ],
        compiler_params=_params(("arbitrary",)),
        name="mla_prep",
    )(x2, pos_row, pos_col, w_b, q_g, kv_g, w_uqt, w_uk, w_uvt, freq_col, freq_row, sign_col, sign_row)


def _attention_kernel(qt_ref, k_ref, vt_ref, o_ref, s0_ref, s1_ref, m_ref, acc_ref):
    S = k_ref.shape[2]
    n_chunks = S // KV_TILE
    q_t = qt_ref[0, 0]

    def scores(c, dst_ref):
        start = pl.multiple_of(c * KV_TILE, KV_TILE)
        dst_ref[...] = _dot(k_ref[0, 0, pl.ds(start, KV_TILE), :], q_t)

    def accumulate(c, src_ref):
        start = pl.multiple_of(c * KV_TILE, KV_TILE)
        s_t = src_ref[...]
        m = m_ref[...]
        m_new = jnp.maximum(m, jnp.max(s_t, axis=0, keepdims=True))
        p_t = jnp.exp2(s_t - m_new)
        alpha = jnp.exp2(m - m_new)
        m_ref[...] = m_new
        acc_ref[...] = alpha * acc_ref[...] + _dot(vt_ref[0, 0, :, pl.ds(start, KV_TILE)], p_t.astype(BF16))

    m_ref[...] = jnp.full(m_ref.shape, NEG_INF, F32)
    acc_ref[...] = jnp.zeros(acc_ref.shape, F32)
    scores(0, s0_ref)

    bufs = (s0_ref, s1_ref)

    def group(c0, prefetch_last):
        for i in range(KV_GROUP):
            if i + 1 < KV_GROUP or prefetch_last:
                scores(c0 + i + 1, bufs[(i + 1) % 2])
            accumulate(c0 + i, bufs[i % 2])

    def body(g, carry):
        group(g * KV_GROUP, True)
        return carry

    lax.fori_loop(0, n_chunks // KV_GROUP - 1, body, 0)
    group(n_chunks - KV_GROUP, False)
    o_ref[0] = (acc_ref[:V_HEAD] / acc_ref[V_HEAD:V_HEAD + 1]).T.astype(BF16)


def _attention(q_t, k, v_t):
    B, H, _, S = q_t.shape
    tq = min(Q_TILE, S)
    assert S % tq == 0 and S % (KV_TILE * KV_GROUP) == 0 and KV_GROUP % 2 == 0
    return pl.pallas_call(
        _attention_kernel,
        grid=(B, H, S // tq),
        in_specs=[
            pl.BlockSpec((1, 1, QK_PAD, tq), lambda b, h, i: (b, h, 0, i)),
            pl.BlockSpec((1, 1, S, QK_PAD), lambda b, h, i: (b, h, 0, 0)),
            pl.BlockSpec((1, 1, V_ROWS, S), lambda b, h, i: (b, h, 0, 0)),
        ],
        out_specs=pl.BlockSpec((1, tq, V_HEAD), lambda b, h, i: (b, i, h)),
        out_shape=jax.ShapeDtypeStruct((B, S, H * V_HEAD), BF16),
        scratch_shapes=[
            pltpu.VMEM((KV_TILE, tq), F32),
            pltpu.VMEM((KV_TILE, tq), F32),
            pltpu.VMEM((1, tq), F32),
            pltpu.VMEM((V_ROWS, tq), F32),
        ],
        compiler_params=_params(("arbitrary", "arbitrary", "arbitrary")),
        name="attention",
    )(q_t, k, v_t)


def _mix_route_kernel(x_ref, ga_ref, sgb_ref, o_ref, wbp_ref, wout_ref, g1_ref, b1_ref, wr_ref, rb_ref,
                      h1_ref, h1t_ref, idx_ref, rank_ref, gw_ref, cnt_ref, base_ref):
    tm = x_ref.shape[0]

    @pl.when(pl.program_id(0) == 0)
    def _():
        base_ref[...] = jnp.zeros_like(base_ref)

    yb = _dot(o_ref[...], wbp_ref[...])
    merged = ga_ref[...].astype(F32) + sgb_ref[...].astype(F32) * yb
    mixed = _dot(merged.astype(BF16), wout_ref[...])
    h1 = _layer_norm(DN_ALPHA * x_ref[...] + mixed, g1_ref[...], b1_ref[...])
    h1_ref[...] = h1
    _tile_rows_store(h1t_ref, 0, tm, h1)

    scores_t = _sigmoid(_dot(h1.astype(BF16), wr_ref[...])).T
    biased = scores_t + rb_ref[...]

    riota = lax.broadcasted_iota(jnp.int32, (GROUP_SIZE, tm), 0)
    blocks, gscore = [], []
    for g in range(N_GROUPS):
        blk = biased[g * GROUP_SIZE:(g + 1) * GROUP_SIZE]
        m1 = jnp.max(blk, axis=0, keepdims=True)
        i1 = jnp.min(jnp.where(blk == m1, riota, GROUP_SIZE), axis=0, keepdims=True)
        m2 = jnp.max(jnp.where(riota == i1, NEG_INF, blk), axis=0, keepdims=True)
        blocks.append(blk)
        gscore.append(m1 + m2)

    selected = [jnp.zeros((1, tm), F32) for _ in range(N_GROUPS)]
    for _ in range(TOPK_GROUPS):
        best = functools.reduce(jnp.maximum, gscore)
        first = functools.reduce(
            jnp.minimum, [jnp.where(gscore[g] == best, g, N_GROUPS) for g in range(N_GROUPS)])
        for g in range(N_GROUPS):
            hit = first == g
            selected[g] = jnp.where(hit, 1.0, selected[g])
            gscore[g] = jnp.where(hit, NEG_INF, gscore[g])

    masked = jnp.concatenate(
        [jnp.where(jnp.broadcast_to(selected[g], blocks[g].shape) > 0.5, blocks[g], NEG_INF)
         for g in range(N_GROUPS)], axis=0)

    eiota = lax.broadcasted_iota(jnp.int32, (N_EXPERTS, tm), 0)
    picks, weights = [], []
    for _ in range(TOP_K):
        best = jnp.max(masked, axis=0, keepdims=True)
        e = jnp.min(jnp.where(masked == best, eiota, N_EXPERTS), axis=0, keepdims=True)
        hit = eiota == e
        masked = jnp.where(hit, NEG_INF, masked)
        picks.append(e)
        weights.append(jnp.sum(jnp.where(hit, scores_t, 0.0), axis=0, keepdims=True))
    wsum = functools.reduce(lambda a, b: a + b, weights)

    chosen = functools.reduce(
        lambda a, b: a + b, [jnp.where(eiota == e, 1.0, 0.0) for e in picks])
    before = (lax.broadcasted_iota(jnp.int32, (tm, tm), 0)
              < lax.broadcasted_iota(jnp.int32, (tm, tm), 1)).astype(BF16)
    rank_full = _dot(chosen.astype(BF16), before) + base_ref[:, :1]
    new_base = base_ref[:, :1] + jnp.sum(chosen, axis=1, keepdims=True)
    base_ref[...] = jnp.broadcast_to(new_base, base_ref.shape)
    cnt_ref[...] = jnp.broadcast_to(new_base, cnt_ref.shape)

    kiota = lax.broadcasted_iota(jnp.int32, (TOP_K, tm), 0)
    wiota = lax.broadcasted_iota(jnp.int32, (LANES, tm), 0)
    idx_out = jnp.zeros((TOP_K, tm), jnp.int32)
    rank_out = jnp.zeros((TOP_K, tm), jnp.int32)
    gw_t = jnp.zeros((LANES, tm), F32)
    for k in range(TOP_K):
        r = jnp.sum(jnp.where(eiota == picks[k], rank_full, 0.0), axis=0, keepdims=True)
        idx_out = jnp.where(kiota == k, picks[k], idx_out)
        rank_out = jnp.where(kiota == k, r.astype(jnp.int32), rank_out)
        gw_t = jnp.where(wiota == k, weights[k] / wsum * ROUTED_SCALE, gw_t)
    idx_ref[...] = idx_out
    rank_ref[...] = rank_out
    gw_ref[...] = gw_t.T


def _mix_route(x2, ga, sgb, o, w_bp, w_out, g1, b1, w_r, r_bias):
    T = x2.shape[0]
    tm = TOKEN_TILE
    row = lambda i: (i, 0)
    col = lambda i: (0, i)
    return pl.pallas_call(
        _mix_route_kernel,
        grid=(T // tm,),
        in_specs=[
            pl.BlockSpec((tm, D_MODEL), row),
            pl.BlockSpec((tm, D_MODEL), row),
            pl.BlockSpec((tm, D_MODEL), row),
            pl.BlockSpec((tm, D_MODEL), row),
            _const_spec((D_MODEL, D_MODEL)),
            _const_spec((D_MODEL, D_MODEL)),
            _const_spec((1, D_MODEL)),
            _const_spec((1, D_MODEL)),
            _const_spec((D_MODEL, N_EXPERTS)),
            _const_spec((N_EXPERTS, 1)),
        ],
        out_specs=[
            pl.BlockSpec((tm, D_MODEL), row),
            pl.BlockSpec((tm * SUBLANES, LANES), row),
            pl.BlockSpec((TOP_K, tm), col),
            pl.BlockSpec((TOP_K, tm), col),
            pl.BlockSpec((tm, LANES), row),
            _const_spec((N_EXPERTS, LANES)),
        ],
        out_shape=[
            jax.ShapeDtypeStruct((T, D_MODEL), F32),
            jax.ShapeDtypeStruct((T * SUBLANES, LANES), F32),
            jax.ShapeDtypeStruct((TOP_K, T), jnp.int32),
            jax.ShapeDtypeStruct((TOP_K, T), jnp.int32),
            jax.ShapeDtypeStruct((T, LANES), F32),
            jax.ShapeDtypeStruct((N_EXPERTS, LANES), F32),
        ],
        scratch_shapes=[pltpu.VMEM((N_EXPERTS, LANES), F32)],
        compiler_params=_params(("arbitrary",)),
        name="mix_route",
    )(x2, ga, sgb, o, w_bp, w_out, g1, b1, w_r, r_bias)


def _slot_kernel(starts_ref, idx_ref, rank_ref, slot_ref):
    idx = idx_ref[...]

    def lookup(e, acc):
        return jnp.where(idx == e, starts_ref[e], acc)

    slot_ref[...] = lax.fori_loop(0, N_EXPERTS, lookup, jnp.zeros_like(idx), unroll=8) + rank_ref[...]


def _slot_index(starts, idx_t, rank_t):
    T = idx_t.shape[1]
    tl = min(SLOT_LANES, T)
    col = lambda i: (0, i)
    return pl.pallas_call(
        _slot_kernel,
        grid=(T // tl,),
        in_specs=[
            pl.BlockSpec(memory_space=pltpu.SMEM),
            pl.BlockSpec((TOP_K, tl), col),
            pl.BlockSpec((TOP_K, tl), col),
        ],
        out_specs=pl.BlockSpec((TOP_K, tl), col),
        out_shape=jax.ShapeDtypeStruct((TOP_K, T), jnp.int32),
        compiler_params=_params(("arbitrary",)),
        name="slot_index",
    )(starts, idx_t, rank_t)


def _token_tile(ref, token):
    return ref.at[pl.ds(pl.multiple_of(token * SUBLANES, SUBLANES), SUBLANES), :]


def _dispatch_kernel(slot_ref, h1t_ref, xs_ref, sem):
    tm = h1t_ref.shape[0] // SUBLANES

    def issue(t, carry):
        src = _token_tile(h1t_ref, t)
        for k in range(TOP_K):
            pltpu.make_async_copy(src, _token_tile(xs_ref, slot_ref[k * tm + t]), sem).start(priority=k % 2)
        return carry

    lax.fori_loop(0, tm, issue, 0, unroll=ISSUE_UNROLL)
    for _ in range(TOP_K):
        pltpu.make_async_copy(h1t_ref, xs_ref.at[pl.ds(0, tm * SUBLANES), :], sem).wait()


def _dispatch(slots_tiled, h1t):
    T = h1t.shape[0] // SUBLANES
    tm = TOKEN_TILE
    return pl.pallas_call(
        _dispatch_kernel,
        grid=(T // tm,),
        in_specs=[
            pl.BlockSpec((TOP_K * tm,), lambda i: (i,), memory_space=pltpu.SMEM),
            pl.BlockSpec((tm * SUBLANES, LANES), lambda i: (i, 0)),
        ],
        out_specs=pl.BlockSpec(memory_space=pl.ANY),
        out_shape=jax.ShapeDtypeStruct((T * TOP_K * SUBLANES, LANES), F32),
        scratch_shapes=[pltpu.SemaphoreType.DMA(())],
        compiler_params=_params(("arbitrary",)),
        name="dispatch",
    )(slots_tiled, h1t)


def _weight_copies(expert, slot, w_hbm_refs, w_buf_refs, sems):
    return [pltpu.make_async_copy(w_hbm.at[expert], w_buf.at[slot], sems.at[slot, n])
            for n, (w_hbm, w_buf) in enumerate(zip(w_hbm_refs, w_buf_refs))]


def _experts_kernel(first_ref, exp_ref, lo_ref, hi_ref, next_ref, slot_ref,
                    xs_hbm, wg_hbm, wu_hbm, wd_hbm, ys_ref,
                    xbuf, x_ref, acc_ref, wg_buf, wu_buf, wd_buf, wg_bf_ref, wu_bf_ref, wd_bf_ref, sems, xsems):
    blk = pl.program_id(0)
    n_blocks = pl.num_programs(0)
    w_hbm_refs = (wg_hbm, wu_hbm, wd_hbm)
    w_buf_refs = (wg_buf, wu_buf, wd_buf)
    block_tiles = ROW_BLOCK * SUBLANES

    def block_copy(b):
        slot = b % X_RING
        start = pl.multiple_of(b * block_tiles, block_tiles)
        return pltpu.make_async_copy(xs_hbm.at[pl.ds(start, block_tiles), :], xbuf.at[slot], xsems.at[slot])

    @pl.when(blk == 0)
    def _():
        for b in range(X_RING - 1):
            block_copy(b).start()

    @pl.when(blk + X_RING - 1 < n_blocks)
    def _():
        block_copy(blk + X_RING - 1).start()

    block_copy(blk).wait()
    x_ref[...] = _tile_rows_load(xbuf.at[blk % X_RING], 0, ROW_BLOCK).astype(BF16)
    acc_ref[...] = jnp.zeros_like(acc_ref)
    rows = blk * ROW_BLOCK + lax.broadcasted_iota(jnp.int32, (ROW_BLOCK, 1), 0)

    def item(i, carry):
        expert = exp_ref[i]
        slot = slot_ref[i]

        @pl.when(i == 0)
        def _():
            for copy in _weight_copies(expert, slot, w_hbm_refs, w_buf_refs, sems):
                copy.start()

        @pl.when((i == 0) | (exp_ref[jnp.maximum(i - 1, 0)] != expert))
        def _():
            following = next_ref[i]

            @pl.when(following < N_EXPERTS)
            def _():
                for copy in _weight_copies(following, 1 - slot, w_hbm_refs, w_buf_refs, sems):
                    copy.start(priority=1)

            for copy in _weight_copies(expert, slot, w_hbm_refs, w_buf_refs, sems):
                copy.wait()
            wg_bf_ref[...] = wg_buf[slot].astype(BF16)
            wu_bf_ref[...] = wu_buf[slot].astype(BF16)
            wd_bf_ref[...] = wd_buf[slot].astype(BF16)

        x = x_ref[...]
        gate = _dot(x, wg_bf_ref[...])
        up = _dot(x, wu_bf_ref[...])
        hidden = (gate * _sigmoid(gate) * up).astype(BF16)
        y = _dot(hidden, wd_bf_ref[...])
        keep = (rows >= lo_ref[i]) & (rows < hi_ref[i])
        acc_ref[...] = jnp.where(keep, y, acc_ref[...])
        return carry

    lax.fori_loop(first_ref[blk], first_ref[blk + 1], item, 0)
    _tile_rows_store(ys_ref, 0, ROW_BLOCK, acc_ref[...])


def _experts(block_first, item_exp, item_lo, item_hi, item_next, item_slot, xs, w_gate, w_up, w_down):
    n_rows = xs.shape[0] // SUBLANES
    block = (ROW_BLOCK * SUBLANES, LANES)
    grid_spec = pltpu.PrefetchScalarGridSpec(
        num_scalar_prefetch=6,
        grid=(n_rows // ROW_BLOCK,),
        in_specs=[
            pl.BlockSpec(memory_space=pl.ANY),
            pl.BlockSpec(memory_space=pl.ANY),
            pl.BlockSpec(memory_space=pl.ANY),
            pl.BlockSpec(memory_space=pl.ANY),
        ],
        out_specs=pl.BlockSpec(block, lambda b, *_: (b, 0)),
        scratch_shapes=[
            pltpu.VMEM((X_RING,) + block, F32),
            pltpu.VMEM((ROW_BLOCK, D_MODEL), BF16),
            pltpu.VMEM((ROW_BLOCK, D_MODEL), F32),
            pltpu.VMEM((2, D_MODEL, D_EXPERT), F32),
            pltpu.VMEM((2, D_MODEL, D_EXPERT), F32),
            pltpu.VMEM((2, D_EXPERT, D_MODEL), F32),
            pltpu.VMEM((D_MODEL, D_EXPERT), BF16),
            pltpu.VMEM((D_MODEL, D_EXPERT), BF16),
            pltpu.VMEM((D_EXPERT, D_MODEL), BF16),
            pltpu.SemaphoreType.DMA((2, 3)),
            pltpu.SemaphoreType.DMA((X_RING,)),
        ],
    )
    assert n_rows // ROW_BLOCK >= X_RING
    return pl.pallas_call(
        _experts_kernel,
        grid_spec=grid_spec,
        out_shape=jax.ShapeDtypeStruct((n_rows * SUBLANES, LANES), F32),
        compiler_params=_params(("arbitrary",)),
        name="experts",
    )(block_first, item_exp, item_lo, item_hi, item_next, item_slot, xs, w_gate, w_up, w_down)


def _combine_kernel(slot0_ref, slot_next_ref, gw_ref, h1_ref, ys_ref, wsg_ref, wsu_ref, wsd_ref,
                    g2_ref, b2_ref, out_ref, buf_ref, sems):
    tm = h1_ref.shape[0]
    i = pl.program_id(0)
    cur = i % 2

    def gather(slot_ref, buf, sem):
        def issue(t, carry):
            for k in range(TOP_K):
                pltpu.make_async_copy(
                    _token_tile(ys_ref, slot_ref[k * tm + t]), _token_tile(buf, k * tm + t), sem).start(priority=k % 2)
            return carry

        lax.fori_loop(0, tm, issue, 0, unroll=ISSUE_UNROLL)

    @pl.when(i == 0)
    def _():
        gather(slot0_ref, buf_ref.at[0], sems.at[0])

    @pl.when(i + 1 < pl.num_programs(0))
    def _():
        gather(slot_next_ref, buf_ref.at[1 - cur], sems.at[1 - cur])

    h1 = h1_ref[...]
    hb = h1.astype(BF16)
    gate = _dot(hb, wsg_ref[...])
    hidden = (gate * _sigmoid(gate) * _dot(hb, wsu_ref[...])).astype(BF16)
    moe = _dot(hidden, wsd_ref[...])

    buf = buf_ref.at[cur]
    for k in range(TOP_K):
        pltpu.make_async_copy(
            ys_ref.at[pl.ds(0, tm * SUBLANES), :], buf.at[pl.ds(k * tm * SUBLANES, tm * SUBLANES), :],
            sems.at[cur]).wait()

    gw = gw_ref[...]
    for k in range(TOP_K):
        moe = moe + gw[:, k:k + 1] * _tile_rows_load(buf, k * tm, tm)
    out_ref[...] = _layer_norm(DN_ALPHA * h1 + moe, g2_ref[...], b2_ref[...])


def _combine(slots_tiled, gw, h1, ys, w_sg, w_su, w_sd, g2, b2):
    T = h1.shape[0]
    tm = TOKEN_TILE
    row = lambda i: (i, 0)
    n_steps = T // tm
    return pl.pallas_call(
        _combine_kernel,
        grid=(n_steps,),
        in_specs=[
            pl.BlockSpec((TOP_K * tm,), lambda i: (0,), memory_space=pltpu.SMEM),
            pl.BlockSpec((TOP_K * tm,), lambda i: (jnp.minimum(i + 1, n_steps - 1),), memory_space=pltpu.SMEM),
            pl.BlockSpec((tm, LANES), row),
            pl.BlockSpec((tm, D_MODEL), row),
            pl.BlockSpec(memory_space=pl.ANY),
            _const_spec((D_MODEL, D_EXPERT)),
            _const_spec((D_MODEL, D_EXPERT)),
            _const_spec((D_EXPERT, D_MODEL)),
            _const_spec((1, D_MODEL)),
            _const_spec((1, D_MODEL)),
        ],
        out_specs=pl.BlockSpec((tm, D_MODEL), row),
        out_shape=jax.ShapeDtypeStruct((T, D_MODEL), F32),
        scratch_shapes=[pltpu.VMEM((2, TOP_K * tm * SUBLANES, LANES), F32), pltpu.SemaphoreType.DMA((2,))],
        compiler_params=_params(("arbitrary",)),
        name="combine",
    )(slots_tiled, slots_tiled, gw, h1, ys, w_sg, w_su, w_sd, g2, b2)


def _expert_work_items(counts, n_rows):
    n_blocks = n_rows // ROW_BLOCK
    n_items = n_blocks + N_EXPERTS
    ends = jnp.cumsum(counts)
    starts = ends - counts
    first_blk = starts // ROW_BLOCK
    last_blk = jnp.maximum(ends - 1, 0) // ROW_BLOCK
    per_expert = jnp.where(counts > 0, last_blk - first_blk + 1, 0)
    item_end = jnp.cumsum(per_expert)
    item_start = item_end - per_expert
    total = item_end[-1]
    i = jnp.arange(n_items, dtype=jnp.int32)
    live = i < total
    j = jnp.minimum(i, total - 1)[:, None]
    owner = (item_start[None, :] <= j) & (j < item_end[None, :])

    def pick(per_expert_values):
        return jnp.sum(jnp.where(owner, per_expert_values[None, :], 0), axis=1)

    expert_ids = jnp.arange(N_EXPERTS, dtype=jnp.int32)
    nonempty = counts > 0
    later = (expert_ids[None, :] > expert_ids[:, None]) & nonempty[None, :]
    following = jnp.min(jnp.where(later, expert_ids[None, :], N_EXPERTS), axis=1)
    buffer_slot = (jnp.cumsum(nonempty) - 1) % 2
    exp = pick(expert_ids)
    blk = pick(first_blk - item_start) + j[:, 0]
    lo = jnp.maximum(pick(starts), blk * ROW_BLOCK)
    hi = jnp.minimum(pick(ends), (blk + 1) * ROW_BLOCK)
    lo = jnp.where(live, lo, 0)
    hi = jnp.where(live, hi, 0)
    block_ids = jnp.arange(n_blocks + 1, dtype=jnp.int32)
    block_first = jnp.sum(live[None, :] & (blk[None, :] < block_ids[:, None]), axis=1)
    as_i32 = lambda a: a.astype(jnp.int32)
    return (as_i32(starts), as_i32(block_first), as_i32(exp), as_i32(lo), as_i32(hi),
            as_i32(pick(following)), as_i32(pick(buffer_slot)))


def kernel(x, positions, w_in, gmlp_ln_g, gmlp_ln_b, w_spatial, b_spatial, w_a_proj, q_norm_g, w_uq, kv_norm_g, w_uk, w_uv, w_b_proj, w_out, ln1_g, ln1_b, w_router, router_bias, w_gate, w_up, w_down, w_sh_gate, w_sh_up, w_sh_down, ln2_g, ln2_b):
    B, S, D = x.shape
    T = B * S
    x2 = x.reshape(T, D)

    w = w_in[0]
    o_u, o_v, o_cq = 0, A_WIDTH, 2 * A_WIDTH
    o_ckv = o_cq + Q_LORA
    o_kr = o_ckv + KV_LORA
    o_ga = o_kr + QK_ROPE
    o_gb = o_ga + D_MODEL
    half_r = QK_ROPE // 2
    w_kr = w[:, o_kr:o_ga]
    w_kr_partner = jnp.concatenate([w_kr[:, half_r:], w_kr[:, :half_r]], axis=1)
    w_b = jnp.concatenate(
        [w[:, o_cq:o_kr], w_kr, w_kr, w_kr_partner, w_kr_partner, w[:, o_gb:]], axis=1).astype(BF16)
    uq = w_uq[0]
    uq_rope = uq[:, :, QK_NOPE:]
    uq_partner = jnp.concatenate([uq_rope[:, :, half_r:], uq_rope[:, :, :half_r]], axis=2)
    w_uqt = jnp.concatenate([uq, uq_partner], axis=2).reshape(Q_LORA, N_HEADS * QK_PAD).T.astype(BF16)
    w_ukm = w_uk[0].reshape(KV_LORA, N_HEADS * QK_NOPE).astype(BF16)
    w_uvt = w_uv[0].reshape(KV_LORA, N_HEADS * V_HEAD).T.astype(BF16)

    freq = ROPE_THETA ** (-jnp.arange(0, half_r, dtype=F32) * 2.0 / QK_ROPE)
    freq128 = jnp.tile(freq, LANES // half_r)
    sign128 = jnp.tile(jnp.concatenate([-jnp.ones((half_r,), F32), jnp.ones((half_r,), F32)]), LANES // QK_ROPE)
    pos_f = positions.astype(F32).reshape(T)
    bias_full = jnp.repeat(b_spatial[0].T, A_GROUP_DIM, axis=1)

    ga = _gmlp_branch(
        x2, w[:, o_u:o_v].astype(BF16), w[:, o_v:o_cq].astype(BF16), w[:, o_ga:o_gb].astype(BF16),
        gmlp_ln_g[0].reshape(1, A_WIDTH), gmlp_ln_b[0].reshape(1, A_WIDTH),
        w_spatial[0].astype(BF16), bias_full, w_a_proj[0].astype(BF16))

    q_t, k, v_t, sgb = _mla_prep(
        x2, pos_f.reshape(T // TOKEN_TILE, 1, TOKEN_TILE), pos_f.reshape(T, 1), w_b,
        q_norm_g[0].reshape(1, Q_LORA), kv_norm_g[0].reshape(1, KV_LORA), w_uqt, w_ukm, w_uvt,
        freq128.reshape(LANES, 1), freq128.reshape(1, LANES), sign128.reshape(LANES, 1), sign128.reshape(1, LANES),
        B, S)

    o = _attention(q_t, k, v_t).reshape(T, N_HEADS * V_HEAD)

    h1, h1t, idx_t, rank_t, gw, cnt = _mix_route(
        x2, ga, sgb, o, w_b_proj[0].astype(BF16), w_out[0].astype(BF16),
        ln1_g[0].reshape(1, D), ln1_b[0].reshape(1, D), w_router[0].astype(BF16),
        router_bias[0].reshape(N_EXPERTS, 1))

    counts = cnt[:, 0].astype(jnp.int32)
    starts, block_first, item_exp, item_lo, item_hi, item_next, item_slot = _expert_work_items(counts, T * TOP_K)

    slots = _slot_index(starts, idx_t, rank_t)
    slots_tiled = slots.reshape(TOP_K, T // TOKEN_TILE, TOKEN_TILE).transpose(1, 0, 2).reshape(T * TOP_K)

    xs = _dispatch(slots_tiled, h1t)
    ys = _experts(block_first, item_exp, item_lo, item_hi, item_next, item_slot, xs, w_gate[0], w_up[0], w_down[0])
    out = _combine(
        slots_tiled, gw, h1, ys, w_sh_gate[0].astype(BF16), w_sh_up[0].astype(BF16),
        w_sh_down[0].astype(BF16), ln2_g[0].reshape(1, D), ln2_b[0].reshape(1, D))
    return out.reshape(B, S, D)
```

```python
import functools
import math

import jax
import jax.numpy as jnp
from jax import lax
from jax.experimental import pallas as pl
from jax.experimental.pallas import tpu as pltpu

D_MODEL = 1024
CHUNK = 128
A_GROUPS = 8
A_GROUP_DIM = 128
A_WIDTH = A_GROUPS * A_GROUP_DIM
N_HEADS = 8
QK_NOPE = 128
QK_ROPE = 64
QK_DIM = QK_NOPE + QK_ROPE
V_HEAD = 128
V_ROWS = 144
Q_LORA = 384
KV_LORA = 256
ROPE_THETA = 10000.0
N_EXPERTS = 256
TOP_K = 8
N_GROUPS = 8
TOPK_GROUPS = 4
GROUP_SIZE = N_EXPERTS // N_GROUPS
D_EXPERT = 256
ROUTED_SCALE = 2.5
DN_ALPHA = 2.0 ** 0.25
LN_EPS = 1e-5
RMS_EPS = 1e-6

QK_PAD = 256
LANES = 128
TOKEN_TILE = 512
Q_TILE = 1024
KV_TILE = 512
KV_GROUP = 4
SLOT_LANES = 2048
ISSUE_UNROLL = 4
ROW_BLOCK = 256
X_RING = 3
SUBLANES = 8
VMEM_LIMIT = 56 * 1024 * 1024

F32 = jnp.float32
BF16 = jnp.bfloat16
NEG_INF = float("-inf")


def _dot(a, b):
    return jnp.dot(a, b, preferred_element_type=F32)


def _sigmoid(x):
    return 1.0 / (1.0 + jnp.exp(-x))


def _gelu_tanh(x):
    return 0.5 * x * (1.0 + jnp.tanh(math.sqrt(2.0 / math.pi) * (x + 0.044715 * (x * x * x))))


def _layer_norm(x, g, b):
    mu = jnp.mean(x, axis=-1, keepdims=True)
    d = x - mu
    var = jnp.mean(d * d, axis=-1, keepdims=True)
    return d * lax.rsqrt(var + LN_EPS) * g + b


def _rms_norm(x, g):
    return x * lax.rsqrt(jnp.mean(x * x, axis=-1, keepdims=True) + RMS_EPS) * g


def _tile_rows_load(ref, first, n):
    return jnp.concatenate(
        [ref[pl.ds(first * SUBLANES + j, n, stride=SUBLANES), :] for j in range(SUBLANES)], axis=1)


def _tile_rows_store(ref, first, n, value):
    for j in range(SUBLANES):
        ref[pl.ds(first * SUBLANES + j, n, stride=SUBLANES), :] = value[:, j * LANES:(j + 1) * LANES]


def _const_spec(shape):
    zeros = (0,) * len(shape)
    return pl.BlockSpec(shape, lambda *_: zeros)


def _params(semantics):
    return pltpu.CompilerParams(dimension_semantics=semantics, vmem_limit_bytes=VMEM_LIMIT)


def _gmlp_kernel(x_ref, wu_ref, wv_ref, wg_ref, lng_ref, lnb_ref, ws_ref, bias_ref, wa_ref, o_ref, mix_ref):
    tm = x_ref.shape[0]
    xb = x_ref[...].astype(BF16)
    v = _gelu_tanh(_dot(xb, wv_ref[...]))
    for h in range(A_GROUPS):
        cols = slice(h * A_GROUP_DIM, (h + 1) * A_GROUP_DIM)
        vn = _layer_norm(v[:, cols], lng_ref[:, cols], lnb_ref[:, cols]).astype(BF16)
        n_chunks = tm // CHUNK
        side_by_side = jnp.concatenate([vn[c * CHUNK:(c + 1) * CHUNK, :] for c in range(n_chunks)], axis=1)
        mixed = _dot(ws_ref[h], side_by_side)
        for c in range(n_chunks):
            mix_ref[c * CHUNK:(c + 1) * CHUNK, cols] = (
                mixed[:, c * A_GROUP_DIM:(c + 1) * A_GROUP_DIM] + bias_ref[:, cols])
    u = _gelu_tanh(_dot(xb, wu_ref[...]))
    ya = _dot((u * mix_ref[...]).astype(BF16), wa_ref[...])
    gate = _sigmoid(_dot(xb, wg_ref[...]))
    o_ref[...] = (gate * ya).astype(BF16)


def _gmlp_branch(x2, w_u, w_v, w_ga, ln_g, ln_b, w_s, bias_full, w_a):
    T = x2.shape[0]
    tm = TOKEN_TILE
    return pl.pallas_call(
        _gmlp_kernel,
        grid=(T // tm,),
        in_specs=[
            pl.BlockSpec((tm, D_MODEL), lambda i: (i, 0)),
            _const_spec((D_MODEL, A_WIDTH)),
            _const_spec((D_MODEL, A_WIDTH)),
            _const_spec((D_MODEL, D_MODEL)),
            _const_spec((1, A_WIDTH)),
            _const_spec((1, A_WIDTH)),
            _const_spec((A_GROUPS, CHUNK, CHUNK)),
            _const_spec((CHUNK, A_WIDTH)),
            _const_spec((A_WIDTH, D_MODEL)),
        ],
        out_specs=pl.BlockSpec((tm, D_MODEL), lambda i: (i, 0)),
        out_shape=jax.ShapeDtypeStruct((T, D_MODEL), BF16),
        scratch_shapes=[pltpu.VMEM((tm, A_WIDTH), F32)],
        compiler_params=_params(("arbitrary",)),
        name="gmlp_branch",
    )(x2, w_u, w_v, w_ga, ln_g, ln_b, w_s, bias_full, w_a)


def _mla_prep_kernel(x_ref, pos_ref, wb_ref, qg_ref, kvg_ref, wuqt_ref, wuk_ref, wuvt_ref, freq_ref,
                     qt_ref, k_ref, vt_ref, sgb_ref):
    xb = x_ref[...].astype(BF16)
    pb = _dot(xb, wb_ref[...])
    c_q = pb[:, :Q_LORA]
    c_kv = pb[:, Q_LORA:Q_LORA + KV_LORA]
    o = Q_LORA + KV_LORA
    kr = pb[:, o:o + LANES]
    kr_partner = pb[:, o + LANES:o + 2 * LANES]
    gb = pb[:, o + 2 * LANES:]
    sgb_ref[...] = _sigmoid(gb).astype(BF16)

    cqn = _rms_norm(c_q, qg_ref[...])
    ckvn = _rms_norm(c_kv, kvg_ref[...])
    cqn_t = cqn.T.astype(BF16)
    ckvn_t = ckvn.T.astype(BF16)

    scale = math.log2(math.e) / math.sqrt(QK_DIM)
    ang = freq_ref[...] * pos_ref[0]
    cos_t = jnp.cos(ang)
    sin_t = jnp.sin(ang)
    table_t = jnp.concatenate([cos_t, cos_t, -sin_t, sin_t], axis=0) * scale
    q_t = _dot(wuqt_ref[...], cqn_t)
    for h in range(N_HEADS):
        blk = q_t[h * QK_PAD:(h + 1) * QK_PAD]
        qt_ref[0, h] = jnp.concatenate(
            [blk[:QK_NOPE] * scale, blk[QK_NOPE:] * table_t], axis=0).astype(BF16)

    v_t = _dot(wuvt_ref[...], ckvn_t)
    ones_rows = jnp.ones((V_ROWS - V_HEAD, v_t.shape[1]), F32)
    for h in range(N_HEADS):
        vt_ref[0, h] = jnp.concatenate([v_t[h * V_HEAD:(h + 1) * V_HEAD], ones_rows], axis=0).astype(BF16)

    cos_k = jnp.concatenate([cos_t, cos_t, cos_t, cos_t], axis=0).T
    sin_k = jnp.concatenate([-sin_t, sin_t, -sin_t, sin_t], axis=0).T
    k_rope = (kr * cos_k + kr_partner * sin_k).astype(BF16)
    k_nope = _dot(ckvn.astype(BF16), wuk_ref[...])
    for h in range(N_HEADS):
        k_ref[0, h] = jnp.concatenate(
            [k_nope[:, h * QK_NOPE:(h + 1) * QK_NOPE].astype(BF16), k_rope], axis=1)


def _mla_prep(x2, pos_row, w_b, q_g, kv_g, w_uqt, w_uk, w_uvt, freq_col, B, S):
    T = x2.shape[0]
    tm = TOKEN_TILE
    spb = S // tm
    n_b = w_b.shape[1]
    head_map = lambda i: (i // spb, 0, 0, i % spb)
    return pl.pallas_call(
        _mla_prep_kernel,
        grid=(T // tm,),
        in_specs=[
            pl.BlockSpec((tm, D_MODEL), lambda i: (i, 0)),
            pl.BlockSpec((1, 1, tm), lambda i: (i, 0, 0)),
            _const_spec((D_MODEL, n_b)),
            _const_spec((1, Q_LORA)),
            _const_spec((1, KV_LORA)),
            _const_spec((N_HEADS * QK_PAD, Q_LORA)),
            _const_spec((KV_LORA, N_HEADS * QK_NOPE)),
            _const_spec((N_HEADS * V_HEAD, KV_LORA)),
            _const_spec((QK_ROPE // 2, 1)),
        ],
        out_specs=[
            pl.BlockSpec((1, N_HEADS, QK_PAD, tm), head_map),
            pl.BlockSpec((1, N_HEADS, tm, QK_PAD), lambda i: (i // spb, 0, i % spb, 0)),
            pl.BlockSpec((1, N_HEADS, V_ROWS, tm), head_map),
            pl.BlockSpec((tm, D_MODEL), lambda i: (i, 0)),
        ],
        out_shape=[
            jax.ShapeDtypeStruct((B, N_HEADS, QK_PAD, S), BF16),
            jax.ShapeDtypeStruct((B, N_HEADS, S, QK_PAD), BF16),
            jax.ShapeDtypeStruct((B, N_HEADS, V_ROWS, S), BF16),
            jax.ShapeDtypeStruct((T, D_MODEL), BF16),
        ],
        compiler_params=_params(("arbitrary",)),
        name="mla_prep",
    )(x2, pos_row, w_b, q_g, kv_g, w_uqt, w_uk, w_uvt, freq_col)


def _attention_kernel(qt_ref, k_ref, vt_ref, o_ref, s0_ref, s1_ref, m_ref, acc_ref):
    S = k_ref.shape[2]
    tq = s0_ref.shape[1]
    n_chunks = S // KV_TILE
    n_q = S // tq
    bufs = (s0_ref, s1_ref)

    def scores(qi, c, dst_ref):
        q_t = qt_ref[0, 0, :, pl.ds(pl.multiple_of(qi * tq, tq), tq)]
        start = pl.multiple_of(c * KV_TILE, KV_TILE)
        dst_ref[...] = _dot(k_ref[0, 0, pl.ds(start, KV_TILE), :], q_t)

    def accumulate(c, src_ref):
        start = pl.multiple_of(c * KV_TILE, KV_TILE)
        s_t = src_ref[...]
        m = m_ref[...]
        m_new = jnp.maximum(m, jnp.max(s_t, axis=0, keepdims=True))
        p_t = jnp.exp2(s_t - m_new)
        alpha = jnp.exp2(m - m_new)
        m_ref[...] = m_new
        acc_ref[...] = alpha * acc_ref[...] + _dot(vt_ref[0, 0, :, pl.ds(start, KV_TILE)], p_t.astype(BF16))

    def group(qi, c0, last_of_tile):
        for i in range(KV_GROUP):
            if i + 1 < KV_GROUP or not last_of_tile:
                scores(qi, c0 + i + 1, bufs[(i + 1) % 2])
            else:
                scores(jnp.minimum(qi + 1, n_q - 1), 0, bufs[(i + 1) % 2])
            accumulate(c0 + i, bufs[i % 2])

    scores(0, 0, s0_ref)

    def q_tile(qi, carry):
        m_ref[...] = jnp.full(m_ref.shape, NEG_INF, F32)
        acc_ref[...] = jnp.zeros(acc_ref.shape, F32)

        def body(g, c):
            group(qi, g * KV_GROUP, False)
            return c

        lax.fori_loop(0, n_chunks // KV_GROUP - 1, body, 0)
        group(qi, n_chunks - KV_GROUP, True)
        o_ref[0, pl.ds(pl.multiple_of(qi * tq, tq), tq), :] = (
            acc_ref[:V_HEAD] / acc_ref[V_HEAD:V_HEAD + 1]).T.astype(BF16)
        return carry

    lax.fori_loop(0, n_q, q_tile, 0)


def _attention(q_t, k, v_t):
    B, H, _, S = q_t.shape
    tq = min(Q_TILE, S)
    assert S % tq == 0 and S % (KV_TILE * KV_GROUP) == 0 and KV_GROUP % 2 == 0
    return pl.pallas_call(
        _attention_kernel,
        grid=(B, H),
        in_specs=[
            pl.BlockSpec((1, 1, QK_PAD, S), lambda b, h: (b, h, 0, 0)),
            pl.BlockSpec((1, 1, S, QK_PAD), lambda b, h: (b, h, 0, 0)),
            pl.BlockSpec((1, 1, V_ROWS, S), lambda b, h: (b, h, 0, 0)),
        ],
        out_specs=pl.BlockSpec((1, S, V_HEAD), lambda b, h: (b, 0, h)),
        out_shape=jax.ShapeDtypeStruct((B, S, H * V_HEAD), BF16),
        scratch_shapes=[
            pltpu.VMEM((KV_TILE, tq), F32),
            pltpu.VMEM((KV_TILE, tq), F32),
            pltpu.VMEM((1, tq), F32),
            pltpu.VMEM((V_ROWS, tq), F32),
        ],
        compiler_params=_params(("arbitrary", "arbitrary")),
        name="attention",
    )(q_t, k, v_t)


def _mix_route_kernel(x_ref, ga_ref, sgb_ref, o_ref, wbp_ref, wout_ref, g1_ref, b1_ref, wr_ref, rb_ref,
                      h1_ref, h1t_ref, idx_ref, rank_ref, gw_ref, cnt_ref, base_ref):
    tm = x_ref.shape[0]

    @pl.when(pl.program_id(0) == 0)
    def _():
        base_ref[...] = jnp.zeros_like(base_ref)

    yb = _dot(o_ref[...], wbp_ref[...])
    merged = ga_ref[...].astype(F32) + sgb_ref[...].astype(F32) * yb
    mixed = _dot(merged.astype(BF16), wout_ref[...])
    h1 = _layer_norm(DN_ALPHA * x_ref[...] + mixed, g1_ref[...], b1_ref[...])
    h1_ref[...] = h1
    _tile_rows_store(h1t_ref, 0, tm, h1)

    scores_t = _sigmoid(_dot(h1.astype(BF16), wr_ref[...])).T
    biased = scores_t + rb_ref[...]

    riota = lax.broadcasted_iota(jnp.int32, (GROUP_SIZE, tm), 0)
    blocks, gscore = [], []
    for g in range(N_GROUPS):
        blk = biased[g * GROUP_SIZE:(g + 1) * GROUP_SIZE]
        m1 = jnp.max(blk, axis=0, keepdims=True)
        i1 = jnp.min(jnp.where(blk == m1, riota, GROUP_SIZE), axis=0, keepdims=True)
        m2 = jnp.max(jnp.where(riota == i1, NEG_INF, blk), axis=0, keepdims=True)
        blocks.append(blk)
        gscore.append(m1 + m2)

    selected = [jnp.zeros((1, tm), F32) for _ in range(N_GROUPS)]
    for _ in range(TOPK_GROUPS):
        best = functools.reduce(jnp.maximum, gscore)
        first = functools.reduce(
            jnp.minimum, [jnp.where(gscore[g] == best, g, N_GROUPS) for g in range(N_GROUPS)])
        for g in range(N_GROUPS):
            hit = first == g
            selected[g] = jnp.where(hit, 1.0, selected[g])
            gscore[g] = jnp.where(hit, NEG_INF, gscore[g])

    masked = jnp.concatenate(
        [jnp.where(jnp.broadcast_to(selected[g], blocks[g].shape) > 0.5, blocks[g], NEG_INF)
         for g in range(N_GROUPS)], axis=0)

    eiota = lax.broadcasted_iota(jnp.int32, (N_EXPERTS, tm), 0)
    picks, weights = [], []
    for _ in range(TOP_K):
        best = jnp.max(masked, axis=0, keepdims=True)
        e = jnp.min(jnp.where(masked == best, eiota, N_EXPERTS), axis=0, keepdims=True)
        hit = eiota == e
        masked = jnp.where(hit, NEG_INF, masked)
        picks.append(e)
        weights.append(jnp.sum(jnp.where(hit, scores_t, 0.0), axis=0, keepdims=True))
    wsum = functools.reduce(lambda a, b: a + b, weights)

    chosen = functools.reduce(
        lambda a, b: a + b, [jnp.where(eiota == e, 1.0, 0.0) for e in picks])
    before = (lax.broadcasted_iota(jnp.int32, (tm, tm), 0)
              < lax.broadcasted_iota(jnp.int32, (tm, tm), 1)).astype(BF16)
    rank_full = _dot(chosen.astype(BF16), before) + base_ref[:, :1]
    new_base = base_ref[:, :1] + jnp.sum(chosen, axis=1, keepdims=True)
    base_ref[...] = jnp.broadcast_to(new_base, base_ref.shape)
    cnt_ref[...] = jnp.broadcast_to(new_base, cnt_ref.shape)

    kiota = lax.broadcasted_iota(jnp.int32, (TOP_K, tm), 0)
    wiota = lax.broadcasted_iota(jnp.int32, (LANES, tm), 0)
    idx_out = jnp.zeros((TOP_K, tm), jnp.int32)
    rank_out = jnp.zeros((TOP_K, tm), jnp.int32)
    gw_t = jnp.zeros((LANES, tm), F32)
    for k in range(TOP_K):
        r = jnp.sum(jnp.where(eiota == picks[k], rank_full, 0.0), axis=0, keepdims=True)
        idx_out = jnp.where(kiota == k, picks[k], idx_out)
        rank_out = jnp.where(kiota == k, r.astype(jnp.int32), rank_out)
        gw_t = jnp.where(wiota == k, weights[k] / wsum * ROUTED_SCALE, gw_t)
    idx_ref[...] = idx_out
    rank_ref[...] = rank_out
    gw_ref[...] = gw_t.T


def _mix_route(x2, ga, sgb, o, w_bp, w_out, g1, b1, w_r, r_bias):
    T = x2.shape[0]
    tm = TOKEN_TILE
    row = lambda i: (i, 0)
    col = lambda i: (0, i)
    return pl.pallas_call(
        _mix_route_kernel,
        grid=(T // tm,),
        in_specs=[
            pl.BlockSpec((tm, D_MODEL), row),
            pl.BlockSpec((tm, D_MODEL), row),
            pl.BlockSpec((tm, D_MODEL), row),
            pl.BlockSpec((tm, D_MODEL), row),
            _const_spec((D_MODEL, D_MODEL)),
            _const_spec((D_MODEL, D_MODEL)),
            _const_spec((1, D_MODEL)),
            _const_spec((1, D_MODEL)),
            _const_spec((D_MODEL, N_EXPERTS)),
            _const_spec((N_EXPERTS, 1)),
        ],
        out_specs=[
            pl.BlockSpec((tm, D_MODEL), row),
            pl.BlockSpec((tm * SUBLANES, LANES), row),
            pl.BlockSpec((TOP_K, tm), col),
            pl.BlockSpec((TOP_K, tm), col),
            pl.BlockSpec((tm, LANES), row),
            _const_spec((N_EXPERTS, LANES)),
        ],
        out_shape=[
            jax.ShapeDtypeStruct((T, D_MODEL), F32),
            jax.ShapeDtypeStruct((T * SUBLANES, LANES), F32),
            jax.ShapeDtypeStruct((TOP_K, T), jnp.int32),
            jax.ShapeDtypeStruct((TOP_K, T), jnp.int32),
            jax.ShapeDtypeStruct((T, LANES), F32),
            jax.ShapeDtypeStruct((N_EXPERTS, LANES), F32),
        ],
        scratch_shapes=[pltpu.VMEM((N_EXPERTS, LANES), F32)],
        compiler_params=_params(("arbitrary",)),
        name="mix_route",
    )(x2, ga, sgb, o, w_bp, w_out, g1, b1, w_r, r_bias)


def _slot_kernel(starts_ref, idx_ref, rank_ref, slot_ref):
    idx = idx_ref[...]

    def lookup(e, acc):
        return jnp.where(idx == e, starts_ref[e], acc)

    slot_ref[...] = lax.fori_loop(0, N_EXPERTS, lookup, jnp.zeros_like(idx), unroll=8) + rank_ref[...]


def _slot_index(starts, idx_t, rank_t):
    T = idx_t.shape[1]
    tl = min(SLOT_LANES, T)
    col = lambda i: (0, i)
    return pl.pallas_call(
        _slot_kernel,
        grid=(T // tl,),
        in_specs=[
            pl.BlockSpec(memory_space=pltpu.SMEM),
            pl.BlockSpec((TOP_K, tl), col),
            pl.BlockSpec((TOP_K, tl), col),
        ],
        out_specs=pl.BlockSpec((TOP_K, tl), col),
        out_shape=jax.ShapeDtypeStruct((TOP_K, T), jnp.int32),
        compiler_params=_params(("arbitrary",)),
        name="slot_index",
    )(starts, idx_t, rank_t)


def _token_tile(ref, token):
    return ref.at[pl.ds(pl.multiple_of(token * SUBLANES, SUBLANES), SUBLANES), :]


def _dispatch_kernel(slot_ref, h1t_ref, xs_ref, sem):
    tm = h1t_ref.shape[0] // SUBLANES

    def issue(t, carry):
        src = _token_tile(h1t_ref, t)
        for k in range(TOP_K):
            pltpu.make_async_copy(src, _token_tile(xs_ref, slot_ref[k * tm + t]), sem).start(priority=k % 2)
        return carry

    lax.fori_loop(0, tm, issue, 0, unroll=ISSUE_UNROLL)
    for _ in range(TOP_K):
        pltpu.make_async_copy(h1t_ref, xs_ref.at[pl.ds(0, tm * SUBLANES), :], sem).wait()


def _dispatch(slots_tiled, h1t):
    T = h1t.shape[0] // SUBLANES
    tm = TOKEN_TILE
    return pl.pallas_call(
        _dispatch_kernel,
        grid=(T // tm,),
        in_specs=[
            pl.BlockSpec((TOP_K * tm,), lambda i: (i,), memory_space=pltpu.SMEM),
            pl.BlockSpec((tm * SUBLANES, LANES), lambda i: (i, 0)),
        ],
        out_specs=pl.BlockSpec(memory_space=pl.ANY),
        out_shape=jax.ShapeDtypeStruct((T * TOP_K * SUBLANES, LANES), F32),
        scratch_shapes=[pltpu.SemaphoreType.DMA(())],
        compiler_params=_params(("arbitrary",)),
        name="dispatch",
    )(slots_tiled, h1t)


def _weight_copies(expert, slot, w_hbm_refs, w_buf_refs, sems):
    return [pltpu.make_async_copy(w_hbm.at[expert], w_buf.at[slot], sems.at[slot, n])
            for n, (w_hbm, w_buf) in enumerate(zip(w_hbm_refs, w_buf_refs))]


def _experts_kernel(first_ref, exp_ref, lo_ref, hi_ref, next_ref, slot_ref,
                    xs_hbm, wg_hbm, wu_hbm, wd_hbm, ys_ref,
                    xbuf, x_ref, acc_ref, wg_buf, wu_buf, wd_buf, wg_bf_ref, wu_bf_ref, wd_bf_ref, sems, xsems):
    blk = pl.program_id(0)
    n_blocks = pl.num_programs(0)
    w_hbm_refs = (wg_hbm, wu_hbm, wd_hbm)
    w_buf_refs = (wg_buf, wu_buf, wd_buf)
    block_tiles = ROW_BLOCK * SUBLANES

    def block_copy(b):
        slot = b % X_RING
        start = pl.multiple_of(b * block_tiles, block_tiles)
        return pltpu.make_async_copy(xs_hbm.at[pl.ds(start, block_tiles), :], xbuf.at[slot], xsems.at[slot])

    @pl.when(blk == 0)
    def _():
        for b in range(X_RING - 1):
            block_copy(b).start()

    @pl.when(blk + X_RING - 1 < n_blocks)
    def _():
        block_copy(blk + X_RING - 1).start()

    block_copy(blk).wait()
    x_ref[...] = _tile_rows_load(xbuf.at[blk % X_RING], 0, ROW_BLOCK).astype(BF16)
    acc_ref[...] = jnp.zeros_like(acc_ref)
    rows = blk * ROW_BLOCK + lax.broadcasted_iota(jnp.int32, (ROW_BLOCK, 1), 0)

    def item(i, carry):
        expert = exp_ref[i]
        slot = slot_ref[i]

        @pl.when(i == 0)
        def _():
            for copy in _weight_copies(expert, slot, w_hbm_refs, w_buf_refs, sems):
                copy.start()

        @pl.when((i == 0) | (exp_ref[jnp.maximum(i - 1, 0)] != expert))
        def _():
            following = next_ref[i]

            @pl.when(following < N_EXPERTS)
            def _():
                for copy in _weight_copies(following, 1 - slot, w_hbm_refs, w_buf_refs, sems):
                    copy.start(priority=1)

            for copy in _weight_copies(expert, slot, w_hbm_refs, w_buf_refs, sems):
                copy.wait()
            wg_bf_ref[...] = wg_buf[slot].astype(BF16)
            wu_bf_ref[...] = wu_buf[slot].astype(BF16)
            wd_bf_ref[...] = wd_buf[slot].astype(BF16)

        x = x_ref[...]
        gate = _dot(x, wg_bf_ref[...])
        up = _dot(x, wu_bf_ref[...])
        hidden = (gate * _sigmoid(gate) * up).astype(BF16)
        y = _dot(hidden, wd_bf_ref[...])
        keep = (rows >= lo_ref[i]) & (rows < hi_ref[i])
        acc_ref[...] = jnp.where(keep, y, acc_ref[...])
        return carry

    lax.fori_loop(first_ref[blk], first_ref[blk + 1], item, 0)
    _tile_rows_store(ys_ref, 0, ROW_BLOCK, acc_ref[...])


def _experts(block_first, item_exp, item_lo, item_hi, item_next, item_slot, xs, w_gate, w_up, w_down):
    n_rows = xs.shape[0] // SUBLANES
    block = (ROW_BLOCK * SUBLANES, LANES)
    grid_spec = pltpu.PrefetchScalarGridSpec(
        num_scalar_prefetch=6,
        grid=(n_rows // ROW_BLOCK,),
        in_specs=[
            pl.BlockSpec(memory_space=pl.ANY),
            pl.BlockSpec(memory_space=pl.ANY),
            pl.BlockSpec(memory_space=pl.ANY),
            pl.BlockSpec(memory_space=pl.ANY),
        ],
        out_specs=pl.BlockSpec(block, lambda b, *_: (b, 0)),
        scratch_shapes=[
            pltpu.VMEM((X_RING,) + block, F32),
            pltpu.VMEM((ROW_BLOCK, D_MODEL), BF16),
            pltpu.VMEM((ROW_BLOCK, D_MODEL), F32),
            pltpu.VMEM((2, D_MODEL, D_EXPERT), F32),
            pltpu.VMEM((2, D_MODEL, D_EXPERT), F32),
            pltpu.VMEM((2, D_EXPERT, D_MODEL), F32),
            pltpu.VMEM((D_MODEL, D_EXPERT), BF16),
            pltpu.VMEM((D_MODEL, D_EXPERT), BF16),
            pltpu.VMEM((D_EXPERT, D_MODEL), BF16),
            pltpu.SemaphoreType.DMA((2, 3)),
            pltpu.SemaphoreType.DMA((X_RING,)),
        ],
    )
    assert n_rows // ROW_BLOCK >= X_RING
    return pl.pallas_call(
        _experts_kernel,
        grid_spec=grid_spec,
        out_shape=jax.ShapeDtypeStruct((n_rows * SUBLANES, LANES), F32),
        compiler_params=_params(("arbitrary",)),
        name="experts",
    )(block_first, item_exp, item_lo, item_hi, item_next, item_slot, xs, w_gate, w_up, w_down)


def _combine_kernel(slot0_ref, slot_next_ref, gw_ref, h1_ref, ys_ref, wsg_ref, wsu_ref, wsd_ref,
                    g2_ref, b2_ref, out_ref, buf_ref, sems):
    tm = h1_ref.shape[0]
    i = pl.program_id(0)
    cur = i % 2

    def gather(slot_ref, buf, sem):
        def issue(t, carry):
            for k in range(TOP_K):
                pltpu.make_async_copy(
                    _token_tile(ys_ref, slot_ref[k * tm + t]), _token_tile(buf, k * tm + t), sem).start(priority=k % 2)
            return carry

        lax.fori_loop(0, tm, issue, 0, unroll=ISSUE_UNROLL)

    @pl.when(i == 0)
    def _():
        gather(slot0_ref, buf_ref.at[0], sems.at[0])

    @pl.when(i + 1 < pl.num_programs(0))
    def _():
        gather(slot_next_ref, buf_ref.at[1 - cur], sems.at[1 - cur])

    h1 = h1_ref[...]
    hb = h1.astype(BF16)
    gate = _dot(hb, wsg_ref[...])
    hidden = (gate * _sigmoid(gate) * _dot(hb, wsu_ref[...])).astype(BF16)
    moe = _dot(hidden, wsd_ref[...])

    buf = buf_ref.at[cur]
    for k in range(TOP_K):
        pltpu.make_async_copy(
            ys_ref.at[pl.ds(0, tm * SUBLANES), :], buf.at[pl.ds(k * tm * SUBLANES, tm * SUBLANES), :],
            sems.at[cur]).wait()

    gw = gw_ref[...]
    for k in range(TOP_K):
        moe = moe + gw[:, k:k + 1] * _tile_rows_load(buf, k * tm, tm)
    out_ref[...] = _layer_norm(DN_ALPHA * h1 + moe, g2_ref[...], b2_ref[...])


def _combine(slots_tiled, gw, h1, ys, w_sg, w_su, w_sd, g2, b2):
    T = h1.shape[0]
    tm = TOKEN_TILE
    row = lambda i: (i, 0)
    n_steps = T // tm
    return pl.pallas_call(
        _combine_kernel,
        grid=(n_steps,),
        in_specs=[
            pl.BlockSpec((TOP_K * tm,), lambda i: (0,), memory_space=pltpu.SMEM),
            pl.BlockSpec((TOP_K * tm,), lambda i: (jnp.minimum(i + 1, n_steps - 1),), memory_space=pltpu.SMEM),
            pl.BlockSpec((tm, LANES), row),
            pl.BlockSpec((tm, D_MODEL), row),
            pl.BlockSpec(memory_space=pl.ANY),
            _const_spec((D_MODEL, D_EXPERT)),
            _const_spec((D_MODEL, D_EXPERT)),
            _const_spec((D_EXPERT, D_MODEL)),
            _const_spec((1, D_MODEL)),
            _const_spec((1, D_MODEL)),
        ],
        out_specs=pl.BlockSpec((tm, D_MODEL), row),
        out_shape=jax.ShapeDtypeStruct((T, D_MODEL), F32),
        scratch_shapes=[pltpu.VMEM((2, TOP_K * tm * SUBLANES, LANES), F32), pltpu.SemaphoreType.DMA((2,))],
        compiler_params=_params(("arbitrary",)),
        name="combine",
    )(slots_tiled, slots_tiled, gw, h1, ys, w_sg, w_su, w_sd, g2, b2)


def _expert_work_items(counts, n_rows):
    n_blocks = n_rows // ROW_BLOCK
    n_items = n_blocks + N_EXPERTS
    ends = jnp.cumsum(counts)
    starts = ends - counts
    first_blk = starts // ROW_BLOCK
    last_blk = jnp.maximum(ends - 1, 0) // ROW_BLOCK
    per_expert = jnp.where(counts > 0, last_blk - first_blk + 1, 0)
    item_end = jnp.cumsum(per_expert)
    item_start = item_end - per_expert
    total = item_end[-1]
    i = jnp.arange(n_items, dtype=jnp.int32)
    live = i < total
    j = jnp.minimum(i, total - 1)[:, None]
    owner = (item_start[None, :] <= j) & (j < item_end[None, :])

    def pick(per_expert_values):
        return jnp.sum(jnp.where(owner, per_expert_values[None, :], 0), axis=1)

    expert_ids = jnp.arange(N_EXPERTS, dtype=jnp.int32)
    nonempty = counts > 0
    later = (expert_ids[None, :] > expert_ids[:, None]) & nonempty[None, :]
    following = jnp.min(jnp.where(later, expert_ids[None, :], N_EXPERTS), axis=1)
    buffer_slot = (jnp.cumsum(nonempty) - 1) % 2
    exp = pick(expert_ids)
    blk = pick(first_blk - item_start) + j[:, 0]
    lo = jnp.maximum(pick(starts), blk * ROW_BLOCK)
    hi = jnp.minimum(pick(ends), (blk + 1) * ROW_BLOCK)
    lo = jnp.where(live, lo, 0)
    hi = jnp.where(live, hi, 0)
    block_ids = jnp.arange(n_blocks + 1, dtype=jnp.int32)
    block_first = jnp.sum(live[None, :] & (blk[None, :] < block_ids[:, None]), axis=1)
    as_i32 = lambda a: a.astype(jnp.int32)
    return (as_i32(starts), as_i32(block_first), as_i32(exp), as_i32(lo), as_i32(hi),
            as_i32(pick(following)), as_i32(pick(buffer_slot)))


def kernel(x, positions, w_in, gmlp_ln_g, gmlp_ln_b, w_spatial, b_spatial, w_a_proj, q_norm_g, w_uq, kv_norm_g, w_uk, w_uv, w_b_proj, w_out, ln1_g, ln1_b, w_router, router_bias, w_gate, w_up, w_down, w_sh_gate, w_sh_up, w_sh_down, ln2_g, ln2_b):
    B, S, D = x.shape
    T = B * S
    x2 = x.reshape(T, D)

    w = w_in[0]
    o_u, o_v, o_cq = 0, A_WIDTH, 2 * A_WIDTH
    o_ckv = o_cq + Q_LORA
    o_kr = o_ckv + KV_LORA
    o_ga = o_kr + QK_ROPE
    o_gb = o_ga + D_MODEL
    half_r = QK_ROPE // 2
    w_kr = w[:, o_kr:o_ga]
    w_kr_partner = jnp.concatenate([w_kr[:, half_r:], w_kr[:, :half_r]], axis=1)
    w_b = jnp.concatenate(
        [w[:, o_cq:o_kr], w_kr, w_kr, w_kr_partner, w_kr_partner, w[:, o_gb:]], axis=1).astype(BF16)
    uq = w_uq[0]
    uq_rope = uq[:, :, QK_NOPE:]
    uq_partner = jnp.concatenate([uq_rope[:, :, half_r:], uq_rope[:, :, :half_r]], axis=2)
    w_uqt = jnp.concatenate([uq, uq_partner], axis=2).reshape(Q_LORA, N_HEADS * QK_PAD).T.astype(BF16)
    w_ukm = w_uk[0].reshape(KV_LORA, N_HEADS * QK_NOPE).astype(BF16)
    w_uvt = w_uv[0].reshape(KV_LORA, N_HEADS * V_HEAD).T.astype(BF16)

    freq = ROPE_THETA ** (-jnp.arange(0, half_r, dtype=F32) * 2.0 / QK_ROPE)
    pos_f = positions.astype(F32).reshape(T)
    bias_full = jnp.repeat(b_spatial[0].T, A_GROUP_DIM, axis=1)

    ga = _gmlp_branch(
        x2, w[:, o_u:o_v].astype(BF16), w[:, o_v:o_cq].astype(BF16), w[:, o_ga:o_gb].astype(BF16),
        gmlp_ln_g[0].reshape(1, A_WIDTH), gmlp_ln_b[0].reshape(1, A_WIDTH),
        w_spatial[0].astype(BF16), bias_full, w_a_proj[0].astype(BF16))

    q_t, k, v_t, sgb = _mla_prep(
        x2, pos_f.reshape(T // TOKEN_TILE, 1, TOKEN_TILE), w_b,
        q_norm_g[0].reshape(1, Q_LORA), kv_norm_g[0].reshape(1, KV_LORA), w_uqt, w_ukm, w_uvt,
        freq.reshape(half_r, 1), B, S)

    o = _attention(q_t, k, v_t).reshape(T, N_HEADS * V_HEAD)

    h1, h1t, idx_t, rank_t, gw, cnt = _mix_route(
        x2, ga, sgb, o, w_b_proj[0].astype(BF16), w_out[0].astype(BF16),
        ln1_g[0].reshape(1, D), ln1_b[0].reshape(1, D), w_router[0].astype(BF16),
        router_bias[0].reshape(N_EXPERTS, 1))

    counts = cnt[:, 0].astype(jnp.int32)
    starts, block_first, item_exp, item_lo, item_hi, item_next, item_slot = _expert_work_items(counts, T * TOP_K)

    slots = _slot_index(starts, idx_t, rank_t)
    slots_tiled = slots.reshape(TOP_K, T // TOKEN_TILE, TOKEN_TILE).transpose(1, 0, 2).reshape(T * TOP_K)

    xs = _dispatch(slots_tiled, h1t)
    ys = _experts(block_first, item_exp, item_lo, item_hi, item_next, item_slot, xs, w_gate[0], w_up[0], w_down[0])
    out = _combine(
        slots_tiled, gw, h1, ys, w_sh_gate[0].astype(BF16), w_sh_up[0].astype(BF16),
        w_sh_down[0].astype(BF16), ln2_g[0].reshape(1, D), ln2_b[0].reshape(1, D))
    return out.reshape(B, S, D)
```

```python
import functools
import math

import jax
import jax.numpy as jnp
from jax import lax
from jax.experimental import pallas as pl
from jax.experimental.pallas import tpu as pltpu

D_MODEL = 1024
CHUNK = 128
A_GROUPS = 8
A_GROUP_DIM = 128
A_WIDTH = A_GROUPS * A_GROUP_DIM
N_HEADS = 8
QK_NOPE = 128
QK_ROPE = 64
QK_DIM = QK_NOPE + QK_ROPE
V_HEAD = 128
V_ROWS = 144
Q_LORA = 384
KV_LORA = 256
ROPE_THETA = 10000.0
N_EXPERTS = 256
TOP_K = 8
N_GROUPS = 8
TOPK_GROUPS = 4
GROUP_SIZE = N_EXPERTS // N_GROUPS
D_EXPERT = 256
ROUTED_SCALE = 2.5
DN_ALPHA = 2.0 ** 0.25
LN_EPS = 1e-5
RMS_EPS = 1e-6

QK_PAD = 256
LANES = 128
TOKEN_TILE = 512
Q_TILE = 1024
KV_TILE = 256
KV_GROUP = 8
SLOT_LANES = 2048
ISSUE_UNROLL = 4
ROW_BLOCK = 512
X_RING = 3
SUBLANES = 8
VMEM_LIMIT = 56 * 1024 * 1024

F32 = jnp.float32
BF16 = jnp.bfloat16
NEG_INF = float("-inf")


def _dot(a, b):
    return jnp.dot(a, b, preferred_element_type=F32)


def _sigmoid(x):
    return 1.0 / (1.0 + jnp.exp(-x))


def _gelu_tanh(x):
    return 0.5 * x * (1.0 + jnp.tanh(math.sqrt(2.0 / math.pi) * (x + 0.044715 * (x * x * x))))


def _layer_norm(x, g, b):
    mu = jnp.mean(x, axis=-1, keepdims=True)
    d = x - mu
    var = jnp.mean(d * d, axis=-1, keepdims=True)
    return d * lax.rsqrt(var + LN_EPS) * g + b


def _rms_norm(x, g):
    return x * lax.rsqrt(jnp.mean(x * x, axis=-1, keepdims=True) + RMS_EPS) * g


def _tile_rows_load(ref, first, n):
    return jnp.concatenate(
        [ref[pl.ds(first * SUBLANES + j, n, stride=SUBLANES), :] for j in range(SUBLANES)], axis=1)


def _tile_rows_store(ref, first, n, value):
    for j in range(SUBLANES):
        ref[pl.ds(first * SUBLANES + j, n, stride=SUBLANES), :] = value[:, j * LANES:(j + 1) * LANES]


def _const_spec(shape):
    zeros = (0,) * len(shape)
    return pl.BlockSpec(shape, lambda *_: zeros)


def _params(semantics):
    return pltpu.CompilerParams(dimension_semantics=semantics, vmem_limit_bytes=VMEM_LIMIT)


def _gmlp_kernel(x_ref, wu_ref, wv_ref, wg_ref, lng_ref, lnb_ref, ws_ref, bias_ref, wa_ref, o_ref, mix_ref):
    tm = x_ref.shape[0]
    xb = x_ref[...].astype(BF16)
    v = _gelu_tanh(_dot(xb, wv_ref[...]))
    for h in range(A_GROUPS):
        cols = slice(h * A_GROUP_DIM, (h + 1) * A_GROUP_DIM)
        vn = _layer_norm(v[:, cols], lng_ref[:, cols], lnb_ref[:, cols]).astype(BF16)
        n_chunks = tm // CHUNK
        side_by_side = jnp.concatenate([vn[c * CHUNK:(c + 1) * CHUNK, :] for c in range(n_chunks)], axis=1)
        mixed = _dot(ws_ref[h], side_by_side)
        for c in range(n_chunks):
            mix_ref[c * CHUNK:(c + 1) * CHUNK, cols] = (
                mixed[:, c * A_GROUP_DIM:(c + 1) * A_GROUP_DIM] + bias_ref[:, cols])
    u = _gelu_tanh(_dot(xb, wu_ref[...]))
    ya = _dot((u * mix_ref[...]).astype(BF16), wa_ref[...])
    gate = _sigmoid(_dot(xb, wg_ref[...]))
    o_ref[...] = (gate * ya).astype(BF16)


def _gmlp_branch(x2, w_u, w_v, w_ga, ln_g, ln_b, w_s, bias_full, w_a):
    T = x2.shape[0]
    tm = TOKEN_TILE
    return pl.pallas_call(
        _gmlp_kernel,
        grid=(T // tm,),
        in_specs=[
            pl.BlockSpec((tm, D_MODEL), lambda i: (i, 0)),
            _const_spec((D_MODEL, A_WIDTH)),
            _const_spec((D_MODEL, A_WIDTH)),
            _const_spec((D_MODEL, D_MODEL)),
            _const_spec((1, A_WIDTH)),
            _const_spec((1, A_WIDTH)),
            _const_spec((A_GROUPS, CHUNK, CHUNK)),
            _const_spec((CHUNK, A_WIDTH)),
            _const_spec((A_WIDTH, D_MODEL)),
        ],
        out_specs=pl.BlockSpec((tm, D_MODEL), lambda i: (i, 0)),
        out_shape=jax.ShapeDtypeStruct((T, D_MODEL), BF16),
        scratch_shapes=[pltpu.VMEM((tm, A_WIDTH), F32)],
        compiler_params=_params(("arbitrary",)),
        name="gmlp_branch",
    )(x2, w_u, w_v, w_ga, ln_g, ln_b, w_s, bias_full, w_a)


def _mla_prep_kernel(x_ref, pos_ref, wb_ref, qg_ref, kvg_ref, wuqt_ref, wuk_ref, wuvt_ref, freq_ref,
                     qt_ref, k_ref, vt_ref, sgb_ref):
    xb = x_ref[...].astype(BF16)
    pb = _dot(xb, wb_ref[...])
    c_q = pb[:, :Q_LORA]
    c_kv = pb[:, Q_LORA:Q_LORA + KV_LORA]
    o = Q_LORA + KV_LORA
    kr = pb[:, o:o + LANES]
    kr_partner = pb[:, o + LANES:o + 2 * LANES]
    gb = pb[:, o + 2 * LANES:]
    sgb_ref[...] = _sigmoid(gb).astype(BF16)

    cqn = _rms_norm(c_q, qg_ref[...])
    ckvn = _rms_norm(c_kv, kvg_ref[...])
    cqn_t = cqn.T.astype(BF16)
    ckvn_t = ckvn.T.astype(BF16)

    scale = math.log2(math.e) / math.sqrt(QK_DIM)
    ang = freq_ref[...] * pos_ref[0]
    cos_t = jnp.cos(ang)
    sin_t = jnp.sin(ang)
    table_t = jnp.concatenate([cos_t, cos_t, -sin_t, sin_t], axis=0) * scale
    q_t = _dot(wuqt_ref[...], cqn_t)
    for h in range(N_HEADS):
        blk = q_t[h * QK_PAD:(h + 1) * QK_PAD]
        qt_ref[0, h] = jnp.concatenate(
            [blk[:QK_NOPE] * scale, blk[QK_NOPE:] * table_t], axis=0).astype(BF16)

    v_t = _dot(wuvt_ref[...], ckvn_t)
    ones_rows = jnp.ones((V_ROWS - V_HEAD, v_t.shape[1]), F32)
    for h in range(N_HEADS):
        vt_ref[0, h] = jnp.concatenate([v_t[h * V_HEAD:(h + 1) * V_HEAD], ones_rows], axis=0).astype(BF16)

    cos_k = jnp.concatenate([cos_t, cos_t, cos_t, cos_t], axis=0).T
    sin_k = jnp.concatenate([-sin_t, sin_t, -sin_t, sin_t], axis=0).T
    k_rope = (kr * cos_k + kr_partner * sin_k).astype(BF16)
    k_nope = _dot(ckvn.astype(BF16), wuk_ref[...])
    for h in range(N_HEADS):
        k_ref[0, h] = jnp.concatenate(
            [k_nope[:, h * QK_NOPE:(h + 1) * QK_NOPE].astype(BF16), k_rope], axis=1)


def _mla_prep(x2, pos_row, w_b, q_g, kv_g, w_uqt, w_uk, w_uvt, freq_col, B, S):
    T = x2.shape[0]
    tm = TOKEN_TILE
    spb = S // tm
    n_b = w_b.shape[1]
    head_map = lambda i: (i // spb, 0, 0, i % spb)
    return pl.pallas_call(
        _mla_prep_kernel,
        grid=(T // tm,),
        in_specs=[
            pl.BlockSpec((tm, D_MODEL), lambda i: (i, 0)),
            pl.BlockSpec((1, 1, tm), lambda i: (i, 0, 0)),
            _const_spec((D_MODEL, n_b)),
            _const_spec((1, Q_LORA)),
            _const_spec((1, KV_LORA)),
            _const_spec((N_HEADS * QK_PAD, Q_LORA)),
            _const_spec((KV_LORA, N_HEADS * QK_NOPE)),
            _const_spec((N_HEADS * V_HEAD, KV_LORA)),
            _const_spec((QK_ROPE // 2, 1)),
        ],
        out_specs=[
            pl.BlockSpec((1, N_HEADS, QK_PAD, tm), head_map),
            pl.BlockSpec((1, N_HEADS, tm, QK_PAD), lambda i: (i // spb, 0, i % spb, 0)),
            pl.BlockSpec((1, N_HEADS, V_ROWS, tm), head_map),
            pl.BlockSpec((tm, D_MODEL), lambda i: (i, 0)),
        ],
        out_shape=[
            jax.ShapeDtypeStruct((B, N_HEADS, QK_PAD, S), BF16),
            jax.ShapeDtypeStruct((B, N_HEADS, S, QK_PAD), BF16),
            jax.ShapeDtypeStruct((B, N_HEADS, V_ROWS, S), BF16),
            jax.ShapeDtypeStruct((T, D_MODEL), BF16),
        ],
        compiler_params=_params(("arbitrary",)),
        name="mla_prep",
    )(x2, pos_row, w_b, q_g, kv_g, w_uqt, w_uk, w_uvt, freq_col)


def _attention_kernel(qt_ref, k_ref, vt_ref, o_ref, s0_ref, s1_ref, m_ref, acc_ref):
    S = k_ref.shape[2]
    tq = s0_ref.shape[1]
    n_chunks = S // KV_TILE
    n_q = S // tq
    bufs = (s0_ref, s1_ref)

    def scores(qi, c, dst_ref):
        q_t = qt_ref[0, 0, :, pl.ds(pl.multiple_of(qi * tq, tq), tq)]
        start = pl.multiple_of(c * KV_TILE, KV_TILE)
        dst_ref[...] = _dot(k_ref[0, 0, pl.ds(start, KV_TILE), :], q_t)

    def accumulate(c, src_ref):
        start = pl.multiple_of(c * KV_TILE, KV_TILE)
        s_t = src_ref[...]
        m = m_ref[...]
        m_new = jnp.maximum(m, jnp.max(s_t, axis=0, keepdims=True))
        p_t = jnp.exp2(s_t - m_new)
        alpha = jnp.exp2(m - m_new)
        m_ref[...] = m_new
        acc_ref[...] = alpha * acc_ref[...] + _dot(vt_ref[0, 0, :, pl.ds(start, KV_TILE)], p_t.astype(BF16))

    def group(qi, c0, last_of_tile):
        for i in range(KV_GROUP):
            if i + 1 < KV_GROUP or not last_of_tile:
                scores(qi, c0 + i + 1, bufs[(i + 1) % 2])
            else:
                scores(jnp.minimum(qi + 1, n_q - 1), 0, bufs[(i + 1) % 2])
            accumulate(c0 + i, bufs[i % 2])

    scores(0, 0, s0_ref)

    def q_tile(qi, carry):
        m_ref[...] = jnp.full(m_ref.shape, NEG_INF, F32)
        acc_ref[...] = jnp.zeros(acc_ref.shape, F32)

        def body(g, c):
            group(qi, g * KV_GROUP, False)
            return c

        lax.fori_loop(0, n_chunks // KV_GROUP - 1, body, 0)
        group(qi, n_chunks - KV_GROUP, True)
        o_ref[0, pl.ds(pl.multiple_of(qi * tq, tq), tq), :] = (
            acc_ref[:V_HEAD] / acc_ref[V_HEAD:V_HEAD + 1]).T.astype(BF16)
        return carry

    lax.fori_loop(0, n_q, q_tile, 0)


def _attention(q_t, k, v_t):
    B, H, _, S = q_t.shape
    tq = min(Q_TILE, S)
    assert S % tq == 0 and S % (KV_TILE * KV_GROUP) == 0 and KV_GROUP % 2 == 0
    return pl.pallas_call(
        _attention_kernel,
        grid=(B, H),
        in_specs=[
            pl.BlockSpec((1, 1, QK_PAD, S), lambda b, h: (b, h, 0, 0)),
            pl.BlockSpec((1, 1, S, QK_PAD), lambda b, h: (b, h, 0, 0)),
            pl.BlockSpec((1, 1, V_ROWS, S), lambda b, h: (b, h, 0, 0)),
        ],
        out_specs=pl.BlockSpec((1, S, V_HEAD), lambda b, h: (b, 0, h)),
        out_shape=jax.ShapeDtypeStruct((B, S, H * V_HEAD), BF16),
        scratch_shapes=[
            pltpu.VMEM((KV_TILE, tq), F32),
            pltpu.VMEM((KV_TILE, tq), F32),
            pltpu.VMEM((1, tq), F32),
            pltpu.VMEM((V_ROWS, tq), F32),
        ],
        compiler_params=_params(("arbitrary", "arbitrary")),
        name="attention",
    )(q_t, k, v_t)


def _mix_route_kernel(x_ref, ga_ref, sgb_ref, o_ref, wbp_ref, wout_ref, g1_ref, b1_ref, wr_ref, rb_ref,
                      h1_ref, h1t_ref, idx_ref, rank_ref, gw_ref, cnt_ref, base_ref):
    tm = x_ref.shape[0]

    @pl.when(pl.program_id(0) == 0)
    def _():
        base_ref[...] = jnp.zeros_like(base_ref)

    yb = _dot(o_ref[...], wbp_ref[...])
    merged = ga_ref[...].astype(F32) + sgb_ref[...].astype(F32) * yb
    mixed = _dot(merged.astype(BF16), wout_ref[...])
    h1 = _layer_norm(DN_ALPHA * x_ref[...] + mixed, g1_ref[...], b1_ref[...])
    h1_ref[...] = h1
    _tile_rows_store(h1t_ref, 0, tm, h1)

    scores_t = _sigmoid(_dot(h1.astype(BF16), wr_ref[...])).T
    biased = scores_t + rb_ref[...]

    riota = lax.broadcasted_iota(jnp.int32, (GROUP_SIZE, tm), 0)
    blocks, gscore = [], []
    for g in range(N_GROUPS):
        blk = biased[g * GROUP_SIZE:(g + 1) * GROUP_SIZE]
        m1 = jnp.max(blk, axis=0, keepdims=True)
        i1 = jnp.min(jnp.where(blk == m1, riota, GROUP_SIZE), axis=0, keepdims=True)
        m2 = jnp.max(jnp.where(riota == i1, NEG_INF, blk), axis=0, keepdims=True)
        blocks.append(blk)
        gscore.append(m1 + m2)

    selected = [jnp.zeros((1, tm), F32) for _ in range(N_GROUPS)]
    for _ in range(TOPK_GROUPS):
        best = functools.reduce(jnp.maximum, gscore)
        first = functools.reduce(
            jnp.minimum, [jnp.where(gscore[g] == best, g, N_GROUPS) for g in range(N_GROUPS)])
        for g in range(N_GROUPS):
            hit = first == g
            selected[g] = jnp.where(hit, 1.0, selected[g])
            gscore[g] = jnp.where(hit, NEG_INF, gscore[g])

    masked = jnp.concatenate(
        [jnp.where(jnp.broadcast_to(selected[g], blocks[g].shape) > 0.5, blocks[g], NEG_INF)
         for g in range(N_GROUPS)], axis=0)

    eiota = lax.broadcasted_iota(jnp.int32, (N_EXPERTS, tm), 0)
    picks, weights = [], []
    for _ in range(TOP_K):
        best = jnp.max(masked, axis=0, keepdims=True)
        e = jnp.min(jnp.where(masked == best, eiota, N_EXPERTS), axis=0, keepdims=True)
        hit = eiota == e
        masked = jnp.where(hit, NEG_INF, masked)
        picks.append(e)
        weights.append(jnp.sum(jnp.where(hit, scores_t, 0.0), axis=0, keepdims=True))
    wsum = functools.reduce(lambda a, b: a + b, weights)

    chosen = functools.reduce(
        lambda a, b: a + b, [jnp.where(eiota == e, 1.0, 0.0) for e in picks])
    before = (lax.broadcasted_iota(jnp.int32, (tm, tm), 0)
              < lax.broadcasted_iota(jnp.int32, (tm, tm), 1)).astype(BF16)
    rank_full = _dot(chosen.astype(BF16), before) + base_ref[:, :1]
    new_base = base_ref[:, :1] + jnp.sum(chosen, axis=1, keepdims=True)
    base_ref[...] = jnp.broadcast_to(new_base, base_ref.shape)
    cnt_ref[...] = jnp.broadcast_to(new_base, cnt_ref.shape)

    kiota = lax.broadcasted_iota(jnp.int32, (TOP_K, tm), 0)
    wiota = lax.broadcasted_iota(jnp.int32, (LANES, tm), 0)
    idx_out = jnp.zeros((TOP_K, tm), jnp.int32)
    rank_out = jnp.zeros((TOP_K, tm), jnp.int32)
    gw_t = jnp.zeros((LANES, tm), F32)
    for k in range(TOP_K):
        r = jnp.sum(jnp.where(eiota == picks[k], rank_full, 0.0), axis=0, keepdims=True)
        idx_out = jnp.where(kiota == k, picks[k], idx_out)
        rank_out = jnp.where(kiota == k, r.astype(jnp.int32), rank_out)
        gw_t = jnp.where(wiota == k, weights[k] / wsum * ROUTED_SCALE, gw_t)
    idx_ref[...] = idx_out
    rank_ref[...] = rank_out
    gw_ref[...] = gw_t.T


def _mix_route(x2, ga, sgb, o, w_bp, w_out, g1, b1, w_r, r_bias):
    T = x2.shape[0]
    tm = TOKEN_TILE
    row = lambda i: (i, 0)
    col = lambda i: (0, i)
    return pl.pallas_call(
        _mix_route_kernel,
        grid=(T // tm,),
        in_specs=[
            pl.BlockSpec((tm, D_MODEL), row),
            pl.BlockSpec((tm, D_MODEL), row),
            pl.BlockSpec((tm, D_MODEL), row),
            pl.BlockSpec((tm, D_MODEL), row),
            _const_spec((D_MODEL, D_MODEL)),
            _const_spec((D_MODEL, D_MODEL)),
            _const_spec((1, D_MODEL)),
            _const_spec((1, D_MODEL)),
            _const_spec((D_MODEL, N_EXPERTS)),
            _const_spec((N_EXPERTS, 1)),
        ],
        out_specs=[
            pl.BlockSpec((tm, D_MODEL), row),
            pl.BlockSpec((tm * SUBLANES, LANES), row),
            pl.BlockSpec((TOP_K, tm), col),
            pl.BlockSpec((TOP_K, tm), col),
            pl.BlockSpec((tm, LANES), row),
            _const_spec((N_EXPERTS, LANES)),
        ],
        out_shape=[
            jax.ShapeDtypeStruct((T, D_MODEL), F32),
            jax.ShapeDtypeStruct((T * SUBLANES, LANES), F32),
            jax.ShapeDtypeStruct((TOP_K, T), jnp.int32),
            jax.ShapeDtypeStruct((TOP_K, T), jnp.int32),
            jax.ShapeDtypeStruct((T, LANES), F32),
            jax.ShapeDtypeStruct((N_EXPERTS, LANES), F32),
        ],
        scratch_shapes=[pltpu.VMEM((N_EXPERTS, LANES), F32)],
        compiler_params=_params(("arbitrary",)),
        name="mix_route",
    )(x2, ga, sgb, o, w_bp, w_out, g1, b1, w_r, r_bias)


def _slot_kernel(starts_ref, idx_ref, rank_ref, slot_ref):
    idx = idx_ref[...]

    def lookup(e, acc):
        return jnp.where(idx == e, starts_ref[e], acc)

    slot_ref[...] = lax.fori_loop(0, N_EXPERTS, lookup, jnp.zeros_like(idx), unroll=8) + rank_ref[...]


def _slot_index(starts, idx_t, rank_t):
    T = idx_t.shape[1]
    tl = min(SLOT_LANES, T)
    col = lambda i: (0, i)
    return pl.pallas_call(
        _slot_kernel,
        grid=(T // tl,),
        in_specs=[
            pl.BlockSpec(memory_space=pltpu.SMEM),
            pl.BlockSpec((TOP_K, tl), col),
            pl.BlockSpec((TOP_K, tl), col),
        ],
        out_specs=pl.BlockSpec((TOP_K, tl), col),
        out_shape=jax.ShapeDtypeStruct((TOP_K, T), jnp.int32),
        compiler_params=_params(("arbitrary",)),
        name="slot_index",
    )(starts, idx_t, rank_t)


def _token_tile(ref, token):
    return ref.at[pl.ds(pl.multiple_of(token * SUBLANES, SUBLANES), SUBLANES), :]


def _dispatch_kernel(slot_ref, h1t_ref, xs_ref, sem):
    tm = h1t_ref.shape[0] // SUBLANES

    def issue(t, carry):
        src = _token_tile(h1t_ref, t)
        for k in range(TOP_K):
            pltpu.make_async_copy(src, _token_tile(xs_ref, slot_ref[k * tm + t]), sem).start(priority=k % 2)
        return carry

    lax.fori_loop(0, tm, issue, 0, unroll=ISSUE_UNROLL)
    for _ in range(TOP_K):
        pltpu.make_async_copy(h1t_ref, xs_ref.at[pl.ds(0, tm * SUBLANES), :], sem).wait()


def _dispatch(slots_tiled, h1t):
    T = h1t.shape[0] // SUBLANES
    tm = TOKEN_TILE
    return pl.pallas_call(
        _dispatch_kernel,
        grid=(T // tm,),
        in_specs=[
            pl.BlockSpec((TOP_K * tm,), lambda i: (i,), memory_space=pltpu.SMEM),
            pl.BlockSpec((tm * SUBLANES, LANES), lambda i: (i, 0)),
        ],
        out_specs=pl.BlockSpec(memory_space=pl.ANY),
        out_shape=jax.ShapeDtypeStruct((T * TOP_K * SUBLANES, LANES), F32),
        scratch_shapes=[pltpu.SemaphoreType.DMA(())],
        compiler_params=_params(("arbitrary",)),
        name="dispatch",
    )(slots_tiled, h1t)


def _weight_copies(expert, slot, w_hbm_refs, w_buf_refs, sems):
    return [pltpu.make_async_copy(w_hbm.at[expert], w_buf.at[slot], sems.at[slot, n])
            for n, (w_hbm, w_buf) in enumerate(zip(w_hbm_refs, w_buf_refs))]


def _experts_kernel(first_ref, exp_ref, lo_ref, hi_ref, next_ref, slot_ref,
                    xs_hbm, wg_hbm, wu_hbm, wd_hbm, ys_ref,
                    xbuf, x_ref, acc_ref, wg_buf, wu_buf, wd_buf, wg_bf_ref, wu_bf_ref, wd_bf_ref, sems, xsems):
    blk = pl.program_id(0)
    n_blocks = pl.num_programs(0)
    w_hbm_refs = (wg_hbm, wu_hbm, wd_hbm)
    w_buf_refs = (wg_buf, wu_buf, wd_buf)
    block_tiles = ROW_BLOCK * SUBLANES

    def block_copy(b):
        slot = b % X_RING
        start = pl.multiple_of(b * block_tiles, block_tiles)
        return pltpu.make_async_copy(xs_hbm.at[pl.ds(start, block_tiles), :], xbuf.at[slot], xsems.at[slot])

    @pl.when(blk == 0)
    def _():
        for b in range(X_RING - 1):
            block_copy(b).start()

    @pl.when(blk + X_RING - 1 < n_blocks)
    def _():
        block_copy(blk + X_RING - 1).start()

    block_copy(blk).wait()
    x_ref[...] = _tile_rows_load(xbuf.at[blk % X_RING], 0, ROW_BLOCK).astype(BF16)
    acc_ref[...] = jnp.zeros_like(acc_ref)
    rows = blk * ROW_BLOCK + lax.broadcasted_iota(jnp.int32, (ROW_BLOCK, 1), 0)

    def item(i, carry):
        expert = exp_ref[i]
        slot = slot_ref[i]

        @pl.when(i == 0)
        def _():
            for copy in _weight_copies(expert, slot, w_hbm_refs, w_buf_refs, sems):
                copy.start()

        @pl.when((i == 0) | (exp_ref[jnp.maximum(i - 1, 0)] != expert))
        def _():
            following = next_ref[i]

            @pl.when(following < N_EXPERTS)
            def _():
                for copy in _weight_copies(following, 1 - slot, w_hbm_refs, w_buf_refs, sems):
                    copy.start(priority=1)

            for copy in _weight_copies(expert, slot, w_hbm_refs, w_buf_refs, sems):
                copy.wait()
            wg_bf_ref[...] = wg_buf[slot].astype(BF16)
            wu_bf_ref[...] = wu_buf[slot].astype(BF16)
            wd_bf_ref[...] = wd_buf[slot].astype(BF16)

        x = x_ref[...]
        gate = _dot(x, wg_bf_ref[...])
        up = _dot(x, wu_bf_ref[...])
        hidden = (gate * _sigmoid(gate) * up).astype(BF16)
        y = _dot(hidden, wd_bf_ref[...])
        keep = (rows >= lo_ref[i]) & (rows < hi_ref[i])
        acc_ref[...] = jnp.where(keep, y, acc_ref[...])
        return carry

    lax.fori_loop(first_ref[blk], first_ref[blk + 1], item, 0)
    _tile_rows_store(ys_ref, 0, ROW_BLOCK, acc_ref[...])


def _experts(block_first, item_exp, item_lo, item_hi, item_next, item_slot, xs, w_gate, w_up, w_down):
    n_rows = xs.shape[0] // SUBLANES
    block = (ROW_BLOCK * SUBLANES, LANES)
    grid_spec = pltpu.PrefetchScalarGridSpec(
        num_scalar_prefetch=6,
        grid=(n_rows // ROW_BLOCK,),
        in_specs=[
            pl.BlockSpec(memory_space=pl.ANY),
            pl.BlockSpec(memory_space=pl.ANY),
            pl.BlockSpec(memory_space=pl.ANY),
            pl.BlockSpec(memory_space=pl.ANY),
        ],
        out_specs=pl.BlockSpec(block, lambda b, *_: (b, 0)),
        scratch_shapes=[
            pltpu.VMEM((X_RING,) + block, F32),
            pltpu.VMEM((ROW_BLOCK, D_MODEL), BF16),
            pltpu.VMEM((ROW_BLOCK, D_MODEL), F32),
            pltpu.VMEM((2, D_MODEL, D_EXPERT), F32),
            pltpu.VMEM((2, D_MODEL, D_EXPERT), F32),
            pltpu.VMEM((2, D_EXPERT, D_MODEL), F32),
            pltpu.VMEM((D_MODEL, D_EXPERT), BF16),
            pltpu.VMEM((D_MODEL, D_EXPERT), BF16),
            pltpu.VMEM((D_EXPERT, D_MODEL), BF16),
            pltpu.SemaphoreType.DMA((2, 3)),
            pltpu.SemaphoreType.DMA((X_RING,)),
        ],
    )
    assert n_rows // ROW_BLOCK >= X_RING
    return pl.pallas_call(
        _experts_kernel,
        grid_spec=grid_spec,
        out_shape=jax.ShapeDtypeStruct((n_rows * SUBLANES, LANES), F32),
        compiler_params=_params(("arbitrary",)),
        name="experts",
    )(block_first, item_exp, item_lo, item_hi, item_next, item_slot, xs, w_gate, w_up, w_down)


def _combine_kernel(slot0_ref, slot_next_ref, gw_ref, h1_ref, ys_ref, wsg_ref, wsu_ref, wsd_ref,
                    g2_ref, b2_ref, out_ref, buf_ref, sems):
    tm = h1_ref.shape[0]
    i = pl.program_id(0)
    cur = i % 2

    def gather(slot_ref, buf, sem):
        def issue(t, carry):
            for k in range(TOP_K):
                pltpu.make_async_copy(
                    _token_tile(ys_ref, slot_ref[k * tm + t]), _token_tile(buf, k * tm + t), sem).start(priority=k % 2)
            return carry

        lax.fori_loop(0, tm, issue, 0, unroll=ISSUE_UNROLL)

    @pl.when(i == 0)
    def _():
        gather(slot0_ref, buf_ref.at[0], sems.at[0])

    @pl.when(i + 1 < pl.num_programs(0))
    def _():
        gather(slot_next_ref, buf_ref.at[1 - cur], sems.at[1 - cur])

    h1 = h1_ref[...]
    hb = h1.astype(BF16)
    gate = _dot(hb, wsg_ref[...])
    hidden = (gate * _sigmoid(gate) * _dot(hb, wsu_ref[...])).astype(BF16)
    moe = _dot(hidden, wsd_ref[...])

    buf = buf_ref.at[cur]
    for k in range(TOP_K):
        pltpu.make_async_copy(
            ys_ref.at[pl.ds(0, tm * SUBLANES), :], buf.at[pl.ds(k * tm * SUBLANES, tm * SUBLANES), :],
            sems.at[cur]).wait()

    gw = gw_ref[...]
    for k in range(TOP_K):
        moe = moe + gw[:, k:k + 1] * _tile_rows_load(buf, k * tm, tm)
    out_ref[...] = _layer_norm(DN_ALPHA * h1 + moe, g2_ref[...], b2_ref[...])


def _combine(slots_tiled, gw, h1, ys, w_sg, w_su, w_sd, g2, b2):
    T = h1.shape[0]
    tm = TOKEN_TILE
    row = lambda i: (i, 0)
    n_steps = T // tm
    return pl.pallas_call(
        _combine_kernel,
        grid=(n_steps,),
        in_specs=[
            pl.BlockSpec((TOP_K * tm,), lambda i: (0,), memory_space=pltpu.SMEM),
            pl.BlockSpec((TOP_K * tm,), lambda i: (jnp.minimum(i + 1, n_steps - 1),), memory_space=pltpu.SMEM),
            pl.BlockSpec((tm, LANES), row),
            pl.BlockSpec((tm, D_MODEL), row),
            pl.BlockSpec(memory_space=pl.ANY),
            _const_spec((D_MODEL, D_EXPERT)),
            _const_spec((D_MODEL, D_EXPERT)),
            _const_spec((D_EXPERT, D_MODEL)),
            _const_spec((1, D_MODEL)),
            _const_spec((1, D_MODEL)),
        ],
        out_specs=pl.BlockSpec((tm, D_MODEL), row),
        out_shape=jax.ShapeDtypeStruct((T, D_MODEL), F32),
        scratch_shapes=[pltpu.VMEM((2, TOP_K * tm * SUBLANES, LANES), F32), pltpu.SemaphoreType.DMA((2,))],
        compiler_params=_params(("arbitrary",)),
        name="combine",
    )(slots_tiled, slots_tiled, gw, h1, ys, w_sg, w_su, w_sd, g2, b2)


def _expert_work_items(counts, n_rows):
    n_blocks = n_rows // ROW_BLOCK
    n_items = n_blocks + N_EXPERTS
    ends = jnp.cumsum(counts)
    starts = ends - counts
    first_blk = starts // ROW_BLOCK
    last_blk = jnp.maximum(ends - 1, 0) // ROW_BLOCK
    per_expert = jnp.where(counts > 0, last_blk - first_blk + 1, 0)
    item_end = jnp.cumsum(per_expert)
    item_start = item_end - per_expert
    total = item_end[-1]
    i = jnp.arange(n_items, dtype=jnp.int32)
    live = i < total
    j = jnp.minimum(i, total - 1)[:, None]
    owner = (item_start[None, :] <= j) & (j < item_end[None, :])

    def pick(per_expert_values):
        return jnp.sum(jnp.where(owner, per_expert_values[None, :], 0), axis=1)

    expert_ids = jnp.arange(N_EXPERTS, dtype=jnp.int32)
    nonempty = counts > 0
    later = (expert_ids[None, :] > expert_ids[:, None]) & nonempty[None, :]
    following = jnp.min(jnp.where(later, expert_ids[None, :], N_EXPERTS), axis=1)
    buffer_slot = (jnp.cumsum(nonempty) - 1) % 2
    exp = pick(expert_ids)
    blk = pick(first_blk - item_start) + j[:, 0]
    lo = jnp.maximum(pick(starts), blk * ROW_BLOCK)
    hi = jnp.minimum(pick(ends), (blk + 1) * ROW_BLOCK)
    lo = jnp.where(live, lo, 0)
    hi = jnp.where(live, hi, 0)
    block_ids = jnp.arange(n_blocks + 1, dtype=jnp.int32)
    block_first = jnp.sum(live[None, :] & (blk[None, :] < block_ids[:, None]), axis=1)
    as_i32 = lambda a: a.astype(jnp.int32)
    return (as_i32(starts), as_i32(block_first), as_i32(exp), as_i32(lo), as_i32(hi),
            as_i32(pick(following)), as_i32(pick(buffer_slot)))


def kernel(x, positions, w_in, gmlp_ln_g, gmlp_ln_b, w_spatial, b_spatial, w_a_proj, q_norm_g, w_uq, kv_norm_g, w_uk, w_uv, w_b_proj, w_out, ln1_g, ln1_b, w_router, router_bias, w_gate, w_up, w_down, w_sh_gate, w_sh_up, w_sh_down, ln2_g, ln2_b):
    B, S, D = x.shape
    T = B * S
    x2 = x.reshape(T, D)

    w = w_in[0]
    o_u, o_v, o_cq = 0, A_WIDTH, 2 * A_WIDTH
    o_ckv = o_cq + Q_LORA
    o_kr = o_ckv + KV_LORA
    o_ga = o_kr + QK_ROPE
    o_gb = o_ga + D_MODEL
    half_r = QK_ROPE // 2
    w_kr = w[:, o_kr:o_ga]
    w_kr_partner = jnp.concatenate([w_kr[:, half_r:], w_kr[:, :half_r]], axis=1)
    w_b = jnp.concatenate(
        [w[:, o_cq:o_kr], w_kr, w_kr, w_kr_partner, w_kr_partner, w[:, o_gb:]], axis=1).astype(BF16)
    uq = w_uq[0]
    uq_rope = uq[:, :, QK_NOPE:]
    uq_partner = jnp.concatenate([uq_rope[:, :, half_r:], uq_rope[:, :, :half_r]], axis=2)
    w_uqt = jnp.concatenate([uq, uq_partner], axis=2).reshape(Q_LORA, N_HEADS * QK_PAD).T.astype(BF16)
    w_ukm = w_uk[0].reshape(KV_LORA, N_HEADS * QK_NOPE).astype(BF16)
    w_uvt = w_uv[0].reshape(KV_LORA, N_HEADS * V_HEAD).T.astype(BF16)

    freq = ROPE_THETA ** (-jnp.arange(0, half_r, dtype=F32) * 2.0 / QK_ROPE)
    pos_f = positions.astype(F32).reshape(T)
    bias_full = jnp.repeat(b_spatial[0].T, A_GROUP_DIM, axis=1)

    ga = _gmlp_branch(
        x2, w[:, o_u:o_v].astype(BF16), w[:, o_v:o_cq].astype(BF16), w[:, o_ga:o_gb].astype(BF16),
        gmlp_ln_g[0].reshape(1, A_WIDTH), gmlp_ln_b[0].reshape(1, A_WIDTH),
        w_spatial[0].astype(BF16), bias_full, w_a_proj[0].astype(BF16))

    q_t, k, v_t, sgb = _mla_prep(
        x2, pos_f.reshape(T // TOKEN_TILE, 1, TOKEN_TILE), w_b,
        q_norm_g[0].reshape(1, Q_LORA), kv_norm_g[0].reshape(1, KV_LORA), w_uqt, w_ukm, w_uvt,
        freq.reshape(half_r, 1), B, S)

    o = _attention(q_t, k, v_t).reshape(T, N_HEADS * V_HEAD)

    h1, h1t, idx_t, rank_t, gw, cnt = _mix_route(
        x2, ga, sgb, o, w_b_proj[0].astype(BF16), w_out[0].astype(BF16),
        ln1_g[0].reshape(1, D), ln1_b[0].reshape(1, D), w_router[0].astype(BF16),
        router_bias[0].reshape(N_EXPERTS, 1))

    counts = cnt[:, 0].astype(jnp.int32)
    starts, block_first, item_exp, item_lo, item_hi, item_next, item_slot = _expert_work_items(counts, T * TOP_K)

    slots = _slot_index(starts, idx_t, rank_t)
    slots_tiled = slots.reshape(TOP_K, T // TOKEN_TILE, TOKEN_TILE).transpose(1, 0, 2).reshape(T * TOP_K)

    xs = _dispatch(slots_tiled, h1t)
    ys = _experts(block_first, item_exp, item_lo, item_hi, item_next, item_slot, xs, w_gate[0], w_up[0], w_down[0])
    out = _combine(
        slots_tiled, gw, h1, ys, w_sh_gate[0].astype(BF16), w_sh_up[0].astype(BF16),
        w_sh_down[0].astype(BF16), ln2_g[0].reshape(1, D), ln2_b[0].reshape(1, D))
    return out.reshape(B, S, D)
```

```python
import functools
import math

import jax
import jax.numpy as jnp
from jax import lax
from jax.experimental import pallas as pl
from jax.experimental.pallas import tpu as pltpu

D_MODEL = 1024
CHUNK = 128
A_GROUPS = 8
A_GROUP_DIM = 128
A_WIDTH = A_GROUPS * A_GROUP_DIM
N_HEADS = 8
QK_NOPE = 128
QK_ROPE = 64
QK_DIM = QK_NOPE + QK_ROPE
V_HEAD = 128
V_ROWS = 144
Q_LORA = 384
KV_LORA = 256
ROPE_THETA = 10000.0
N_EXPERTS = 256
TOP_K = 8
N_GROUPS = 8
TOPK_GROUPS = 4
GROUP_SIZE = N_EXPERTS // N_GROUPS
D_EXPERT = 256
ROUTED_SCALE = 2.5
DN_ALPHA = 2.0 ** 0.25
LN_EPS = 1e-5
RMS_EPS = 1e-6

QK_PAD = 256
LANES = 128
TOKEN_TILE = 512
Q_TILE = 1024
KV_TILE = 256
KV_GROUP = 8
SLOT_LANES = 2048
ISSUE_UNROLL = 4
ROW_BLOCK = 512
X_RING = 3
SUBLANES = 8
VMEM_LIMIT = 56 * 1024 * 1024

F32 = jnp.float32
BF16 = jnp.bfloat16
NEG_INF = float("-inf")


def _dot(a, b):
    return jnp.dot(a, b, preferred_element_type=F32)


def _sigmoid(x):
    return 1.0 / (1.0 + jnp.exp(-x))


def _gelu_tanh(x):
    return 0.5 * x * (1.0 + jnp.tanh(math.sqrt(2.0 / math.pi) * (x + 0.044715 * (x * x * x))))


def _layer_norm(x, g, b):
    mu = jnp.mean(x, axis=-1, keepdims=True)
    d = x - mu
    var = jnp.mean(d * d, axis=-1, keepdims=True)
    return d * lax.rsqrt(var + LN_EPS) * g + b


def _rms_norm(x, g):
    return x * lax.rsqrt(jnp.mean(x * x, axis=-1, keepdims=True) + RMS_EPS) * g


def _tile_rows_load(ref, first, n):
    return jnp.concatenate(
        [ref[pl.ds(first * SUBLANES + j, n, stride=SUBLANES), :] for j in range(SUBLANES)], axis=1)


def _tile_rows_store(ref, first, n, value):
    for j in range(SUBLANES):
        ref[pl.ds(first * SUBLANES + j, n, stride=SUBLANES), :] = value[:, j * LANES:(j + 1) * LANES]


def _const_spec(shape):
    zeros = (0,) * len(shape)
    return pl.BlockSpec(shape, lambda *_: zeros)


def _params(semantics):
    return pltpu.CompilerParams(dimension_semantics=semantics, vmem_limit_bytes=VMEM_LIMIT)


def _gmlp_kernel(x_ref, wu_ref, wv_ref, wg_ref, lng_ref, lnb_ref, ws_ref, bias_ref, wa_ref, o_ref, mix_ref):
    tm = x_ref.shape[0]
    xb = x_ref[...].astype(BF16)
    v = _gelu_tanh(_dot(xb, wv_ref[...]))
    for h in range(A_GROUPS):
        cols = slice(h * A_GROUP_DIM, (h + 1) * A_GROUP_DIM)
        vn = _layer_norm(v[:, cols], lng_ref[:, cols], lnb_ref[:, cols]).astype(BF16)
        n_chunks = tm // CHUNK
        side_by_side = jnp.concatenate([vn[c * CHUNK:(c + 1) * CHUNK, :] for c in range(n_chunks)], axis=1)
        mixed = _dot(ws_ref[h], side_by_side)
        for c in range(n_chunks):
            mix_ref[c * CHUNK:(c + 1) * CHUNK, cols] = (
                mixed[:, c * A_GROUP_DIM:(c + 1) * A_GROUP_DIM] + bias_ref[:, cols])
    u = _gelu_tanh(_dot(xb, wu_ref[...]))
    ya = _dot((u * mix_ref[...]).astype(BF16), wa_ref[...])
    gate = _sigmoid(_dot(xb, wg_ref[...]))
    o_ref[...] = (gate * ya).astype(BF16)


def _gmlp_branch(x2, w_u, w_v, w_ga, ln_g, ln_b, w_s, bias_full, w_a):
    T = x2.shape[0]
    tm = TOKEN_TILE
    return pl.pallas_call(
        _gmlp_kernel,
        grid=(T // tm,),
        in_specs=[
            pl.BlockSpec((tm, D_MODEL), lambda i: (i, 0)),
            _const_spec((D_MODEL, A_WIDTH)),
            _const_spec((D_MODEL, A_WIDTH)),
            _const_spec((D_MODEL, D_MODEL)),
            _const_spec((1, A_WIDTH)),
            _const_spec((1, A_WIDTH)),
            _const_spec((A_GROUPS, CHUNK, CHUNK)),
            _const_spec((CHUNK, A_WIDTH)),
            _const_spec((A_WIDTH, D_MODEL)),
        ],
        out_specs=pl.BlockSpec((tm, D_MODEL), lambda i: (i, 0)),
        out_shape=jax.ShapeDtypeStruct((T, D_MODEL), BF16),
        scratch_shapes=[pltpu.VMEM((tm, A_WIDTH), F32)],
        compiler_params=_params(("arbitrary",)),
        name="gmlp_branch",
    )(x2, w_u, w_v, w_ga, ln_g, ln_b, w_s, bias_full, w_a)


def _mla_prep_kernel(x_ref, pos_ref, wb_ref, qg_ref, kvg_ref, wuqt_ref, wuk_ref, wuvt_ref, freq_ref,
                     qt_ref, k_ref, vt_ref, sgb_ref):
    xb = x_ref[...].astype(BF16)
    pb = _dot(xb, wb_ref[...])
    c_q = pb[:, :Q_LORA]
    c_kv = pb[:, Q_LORA:Q_LORA + KV_LORA]
    o = Q_LORA + KV_LORA
    kr = pb[:, o:o + LANES]
    kr_partner = pb[:, o + LANES:o + 2 * LANES]
    gb = pb[:, o + 2 * LANES:]
    sgb_ref[...] = _sigmoid(gb).astype(BF16)

    cqn = _rms_norm(c_q, qg_ref[...])
    ckvn = _rms_norm(c_kv, kvg_ref[...])
    cqn_t = cqn.T.astype(BF16)
    ckvn_t = ckvn.T.astype(BF16)

    scale = math.log2(math.e) / math.sqrt(QK_DIM)
    ang = freq_ref[...] * pos_ref[0]
    cos_t = jnp.cos(ang)
    sin_t = jnp.sin(ang)
    table_t = jnp.concatenate([cos_t, cos_t, -sin_t, sin_t], axis=0) * scale
    q_t = _dot(wuqt_ref[...], cqn_t)
    for h in range(N_HEADS):
        blk = q_t[h * QK_PAD:(h + 1) * QK_PAD]
        qt_ref[0, h] = jnp.concatenate(
            [blk[:QK_NOPE] * scale, blk[QK_NOPE:] * table_t], axis=0).astype(BF16)

    v_t = _dot(wuvt_ref[...], ckvn_t)
    ones_rows = jnp.ones((V_ROWS - V_HEAD, v_t.shape[1]), F32)
    for h in range(N_HEADS):
        vt_ref[0, h] = jnp.concatenate([v_t[h * V_HEAD:(h + 1) * V_HEAD], ones_rows], axis=0).astype(BF16)

    cos_k = jnp.concatenate([cos_t, cos_t, cos_t, cos_t], axis=0).T
    sin_k = jnp.concatenate([-sin_t, sin_t, -sin_t, sin_t], axis=0).T
    k_rope = (kr * cos_k + kr_partner * sin_k).astype(BF16)
    k_nope = _dot(ckvn.astype(BF16), wuk_ref[...])
    for h in range(N_HEADS):
        k_ref[0, h] = jnp.concatenate(
            [k_nope[:, h * QK_NOPE:(h + 1) * QK_NOPE].astype(BF16), k_rope], axis=1)


def _mla_prep(x2, pos_row, w_b, q_g, kv_g, w_uqt, w_uk, w_uvt, freq_col, B, S):
    T = x2.shape[0]
    tm = TOKEN_TILE
    spb = S // tm
    n_b = w_b.shape[1]
    head_map = lambda i: (i // spb, 0, 0, i % spb)
    return pl.pallas_call(
        _mla_prep_kernel,
        grid=(T // tm,),
        in_specs=[
            pl.BlockSpec((tm, D_MODEL), lambda i: (i, 0)),
            pl.BlockSpec((1, 1, tm), lambda i: (i, 0, 0)),
            _const_spec((D_MODEL, n_b)),
            _const_spec((1, Q_LORA)),
            _const_spec((1, KV_LORA)),
            _const_spec((N_HEADS * QK_PAD, Q_LORA)),
            _const_spec((KV_LORA, N_HEADS * QK_NOPE)),
            _const_spec((N_HEADS * V_HEAD, KV_LORA)),
            _const_spec((QK_ROPE // 2, 1)),
        ],
        out_specs=[
            pl.BlockSpec((1, N_HEADS, QK_PAD, tm), head_map),
            pl.BlockSpec((1, N_HEADS, tm, QK_PAD), lambda i: (i // spb, 0, i % spb, 0)),
            pl.BlockSpec((1, N_HEADS, V_ROWS, tm), head_map),
            pl.BlockSpec((tm, D_MODEL), lambda i: (i, 0)),
        ],
        out_shape=[
            jax.ShapeDtypeStruct((B, N_HEADS, QK_PAD, S), BF16),
            jax.ShapeDtypeStruct((B, N_HEADS, S, QK_PAD), BF16),
            jax.ShapeDtypeStruct((B, N_HEADS, V_ROWS, S), BF16),
            jax.ShapeDtypeStruct((T, D_MODEL), BF16),
        ],
        compiler_params=_params(("arbitrary",)),
        name="mla_prep",
    )(x2, pos_row, w_b, q_g, kv_g, w_uqt, w_uk, w_uvt, freq_col)


def _attention_kernel(qt_ref, k_ref, vt_ref, o_ref, s0_ref, s1_ref, m_ref, acc_ref):
    S = k_ref.shape[2]
    tq = s0_ref.shape[1]
    n_chunks = S // KV_TILE
    n_q = S // tq
    bufs = (s0_ref, s1_ref)

    def scores(qi, c, dst_ref):
        q_t = qt_ref[0, 0, :, pl.ds(pl.multiple_of(qi * tq, tq), tq)]
        start = pl.multiple_of(c * KV_TILE, KV_TILE)
        dst_ref[...] = _dot(k_ref[0, 0, pl.ds(start, KV_TILE), :], q_t)

    def accumulate(c, src_ref):
        start = pl.multiple_of(c * KV_TILE, KV_TILE)
        s_t = src_ref[...]
        m = m_ref[...]
        m_new = jnp.maximum(m, jnp.max(s_t, axis=0, keepdims=True))
        p_t = jnp.exp2(s_t - m_new)
        alpha = jnp.exp2(m - m_new)
        m_ref[...] = m_new
        acc_ref[...] = alpha * acc_ref[...] + _dot(vt_ref[0, 0, :, pl.ds(start, KV_TILE)], p_t.astype(BF16))

    def group(qi, c0, last_of_tile):
        for i in range(KV_GROUP):
            if i + 1 < KV_GROUP or not last_of_tile:
                scores(qi, c0 + i + 1, bufs[(i + 1) % 2])
            else:
                scores(jnp.minimum(qi + 1, n_q - 1), 0, bufs[(i + 1) % 2])
            accumulate(c0 + i, bufs[i % 2])

    scores(0, 0, s0_ref)

    def q_tile(qi, carry):
        m_ref[...] = jnp.full(m_ref.shape, NEG_INF, F32)
        acc_ref[...] = jnp.zeros(acc_ref.shape, F32)

        def body(g, c):
            group(qi, g * KV_GROUP, False)
            return c

        lax.fori_loop(0, n_chunks // KV_GROUP - 1, body, 0)
        group(qi, n_chunks - KV_GROUP, True)
        o_ref[0, pl.ds(pl.multiple_of(qi * tq, tq), tq), :] = (
            acc_ref[:V_HEAD] / acc_ref[V_HEAD:V_HEAD + 1]).T.astype(BF16)
        return carry

    lax.fori_loop(0, n_q, q_tile, 0)


def _attention(q_t, k, v_t):
    B, H, _, S = q_t.shape
    tq = min(Q_TILE, S)
    assert S % tq == 0 and S % (KV_TILE * KV_GROUP) == 0 and KV_GROUP % 2 == 0
    return pl.pallas_call(
        _attention_kernel,
        grid=(B, H),
        in_specs=[
            pl.BlockSpec((1, 1, QK_PAD, S), lambda b, h: (b, h, 0, 0)),
            pl.BlockSpec((1, 1, S, QK_PAD), lambda b, h: (b, h, 0, 0)),
            pl.BlockSpec((1, 1, V_ROWS, S), lambda b, h: (b, h, 0, 0)),
        ],
        out_specs=pl.BlockSpec((1, S, V_HEAD), lambda b, h: (b, 0, h)),
        out_shape=jax.ShapeDtypeStruct((B, S, H * V_HEAD), BF16),
        scratch_shapes=[
            pltpu.VMEM((KV_TILE, tq), F32),
            pltpu.VMEM((KV_TILE, tq), F32),
            pltpu.VMEM((1, tq), F32),
            pltpu.VMEM((V_ROWS, tq), F32),
        ],
        compiler_params=_params(("arbitrary", "arbitrary")),
        name="attention",
    )(q_t, k, v_t)


def _mix_route_kernel(x_ref, ga_ref, sgb_ref, o_ref, wbp_ref, wout_ref, g1_ref, b1_ref, wr_ref, rb_ref,
                      h1_ref, h1t_ref, idx_ref, rank_ref, gw_ref, cnt_ref, base_ref):
    tm = x_ref.shape[0]

    @pl.when(pl.program_id(0) == 0)
    def _():
        base_ref[...] = jnp.zeros_like(base_ref)

    yb = _dot(o_ref[...], wbp_ref[...])
    merged = ga_ref[...].astype(F32) + sgb_ref[...].astype(F32) * yb
    mixed = _dot(merged.astype(BF16), wout_ref[...])
    h1 = _layer_norm(DN_ALPHA * x_ref[...] + mixed, g1_ref[...], b1_ref[...])
    h1_ref[...] = h1
    _tile_rows_store(h1t_ref, 0, tm, h1)

    scores_t = _sigmoid(_dot(h1.astype(BF16), wr_ref[...])).T
    biased = scores_t + rb_ref[...]

    riota = lax.broadcasted_iota(jnp.int32, (GROUP_SIZE, tm), 0)
    blocks, gscore = [], []
    for g in range(N_GROUPS):
        blk = biased[g * GROUP_SIZE:(g + 1) * GROUP_SIZE]
        m1 = jnp.max(blk, axis=0, keepdims=True)
        i1 = jnp.min(jnp.where(blk == m1, riota, GROUP_SIZE), axis=0, keepdims=True)
        m2 = jnp.max(jnp.where(riota == i1, NEG_INF, blk), axis=0, keepdims=True)
        blocks.append(blk)
        gscore.append(m1 + m2)

    selected = [jnp.zeros((1, tm), F32) for _ in range(N_GROUPS)]
    for _ in range(TOPK_GROUPS):
        best = functools.reduce(jnp.maximum, gscore)
        first = functools.reduce(
            jnp.minimum, [jnp.where(gscore[g] == best, g, N_GROUPS) for g in range(N_GROUPS)])
        for g in range(N_GROUPS):
            hit = first == g
            selected[g] = jnp.where(hit, 1.0, selected[g])
            gscore[g] = jnp.where(hit, NEG_INF, gscore[g])

    masked = jnp.concatenate(
        [jnp.where(jnp.broadcast_to(selected[g], blocks[g].shape) > 0.5, blocks[g], NEG_INF)
         for g in range(N_GROUPS)], axis=0)

    eiota = lax.broadcasted_iota(jnp.int32, (N_EXPERTS, tm), 0)
    picks, weights = [], []
    for _ in range(TOP_K):
        best = jnp.max(masked, axis=0, keepdims=True)
        e = jnp.min(jnp.where(masked == best, eiota, N_EXPERTS), axis=0, keepdims=True)
        hit = eiota == e
        masked = jnp.where(hit, NEG_INF, masked)
        picks.append(e)
        weights.append(jnp.sum(jnp.where(hit, scores_t, 0.0), axis=0, keepdims=True))
    wsum = functools.reduce(lambda a, b: a + b, weights)

    chosen = functools.reduce(
        lambda a, b: a + b, [jnp.where(eiota == e, 1.0, 0.0) for e in picks])
    before = (lax.broadcasted_iota(jnp.int32, (tm, tm), 0)
              < lax.broadcasted_iota(jnp.int32, (tm, tm), 1)).astype(BF16)
    rank_full = _dot(chosen.astype(BF16), before) + base_ref[:, :1]
    new_base = base_ref[:, :1] + jnp.sum(chosen, axis=1, keepdims=True)
    base_ref[...] = jnp.broadcast_to(new_base, base_ref.shape)
    cnt_ref[...] = jnp.broadcast_to(new_base, cnt_ref.shape)

    kiota = lax.broadcasted_iota(jnp.int32, (TOP_K, tm), 0)
    wiota = lax.broadcasted_iota(jnp.int32, (LANES, tm), 0)
    idx_out = jnp.zeros((TOP_K, tm), jnp.int32)
    rank_out = jnp.zeros((TOP_K, tm), jnp.int32)
    gw_t = jnp.zeros((LANES, tm), F32)
    for k in range(TOP_K):
        r = jnp.sum(jnp.where(eiota == picks[k], rank_full, 0.0), axis=0, keepdims=True)
        idx_out = jnp.where(kiota == k, picks[k], idx_out)
        rank_out = jnp.where(kiota == k, r.astype(jnp.int32), rank_out)
        gw_t = jnp.where(wiota == k, weights[k] / wsum * ROUTED_SCALE, gw_t)
    idx_ref[...] = idx_out
    rank_ref[...] = rank_out
    gw_ref[...] = gw_t.T


def _mix_route(x2, ga, sgb, o, w_bp, w_out, g1, b1, w_r, r_bias):
    T = x2.shape[0]
    tm = TOKEN_TILE
    row = lambda i: (i, 0)
    col = lambda i: (0, i)
    return pl.pallas_call(
        _mix_route_kernel,
        grid=(T // tm,),
        in_specs=[
            pl.BlockSpec((tm, D_MODEL), row),
            pl.BlockSpec((tm, D_MODEL), row),
            pl.BlockSpec((tm, D_MODEL), row),
            pl.BlockSpec((tm, D_MODEL), row),
            _const_spec((D_MODEL, D_MODEL)),
            _const_spec((D_MODEL, D_MODEL)),
            _const_spec((1, D_MODEL)),
            _const_spec((1, D_MODEL)),
            _const_spec((D_MODEL, N_EXPERTS)),
            _const_spec((N_EXPERTS, 1)),
        ],
        out_specs=[
            pl.BlockSpec((tm, D_MODEL), row),
            pl.BlockSpec((tm * SUBLANES, LANES), row),
            pl.BlockSpec((TOP_K, tm), col),
            pl.BlockSpec((TOP_K, tm), col),
            pl.BlockSpec((tm, LANES), row),
            _const_spec((N_EXPERTS, LANES)),
        ],
        out_shape=[
            jax.ShapeDtypeStruct((T, D_MODEL), F32),
            jax.ShapeDtypeStruct((T * SUBLANES, LANES), F32),
            jax.ShapeDtypeStruct((TOP_K, T), jnp.int32),
            jax.ShapeDtypeStruct((TOP_K, T), jnp.int32),
            jax.ShapeDtypeStruct((T, LANES), F32),
            jax.ShapeDtypeStruct((N_EXPERTS, LANES), F32),
        ],
        scratch_shapes=[pltpu.VMEM((N_EXPERTS, LANES), F32)],
        compiler_params=_params(("arbitrary",)),
        name="mix_route",
    )(x2, ga, sgb, o, w_bp, w_out, g1, b1, w_r, r_bias)


def _slot_kernel(starts_ref, idx_ref, rank_ref, slot_ref):
    idx = idx_ref[...]

    def lookup(e, acc):
        return jnp.where(idx == e, starts_ref[e], acc)

    slot_ref[...] = lax.fori_loop(0, N_EXPERTS, lookup, jnp.zeros_like(idx), unroll=8) + rank_ref[...]


def _slot_index(starts, idx_t, rank_t):
    T = idx_t.shape[1]
    tl = min(SLOT_LANES, T)
    col = lambda i: (0, i)
    return pl.pallas_call(
        _slot_kernel,
        grid=(T // tl,),
        in_specs=[
            pl.BlockSpec(memory_space=pltpu.SMEM),
            pl.BlockSpec((TOP_K, tl), col),
            pl.BlockSpec((TOP_K, tl), col),
        ],
        out_specs=pl.BlockSpec((TOP_K, tl), col),
        out_shape=jax.ShapeDtypeStruct((TOP_K, T), jnp.int32),
        compiler_params=_params(("arbitrary",)),
        name="slot_index",
    )(starts, idx_t, rank_t)


def _token_tile(ref, token):
    return ref.at[pl.ds(pl.multiple_of(token * SUBLANES, SUBLANES), SUBLANES), :]


def _dispatch_kernel(slot_ref, h1t_ref, xs_ref, sem):
    tm = h1t_ref.shape[0] // SUBLANES

    def issue(t, carry):
        src = _token_tile(h1t_ref, t)
        for k in range(TOP_K):
            pltpu.make_async_copy(src, _token_tile(xs_ref, slot_ref[k * tm + t]), sem).start(priority=k % 2)
        return carry

    lax.fori_loop(0, tm, issue, 0, unroll=ISSUE_UNROLL)
    for _ in range(TOP_K):
        pltpu.make_async_copy(h1t_ref, xs_ref.at[pl.ds(0, tm * SUBLANES), :], sem).wait()


def _dispatch(slots_tiled, h1t):
    T = h1t.shape[0] // SUBLANES
    tm = TOKEN_TILE
    return pl.pallas_call(
        _dispatch_kernel,
        grid=(T // tm,),
        in_specs=[
            pl.BlockSpec((TOP_K * tm,), lambda i: (i,), memory_space=pltpu.SMEM),
            pl.BlockSpec((tm * SUBLANES, LANES), lambda i: (i, 0)),
        ],
        out_specs=pl.BlockSpec(memory_space=pl.ANY),
        out_shape=jax.ShapeDtypeStruct((T * TOP_K * SUBLANES, LANES), F32),
        scratch_shapes=[pltpu.SemaphoreType.DMA(())],
        compiler_params=_params(("arbitrary",)),
        name="dispatch",
    )(slots_tiled, h1t)


def _weight_copies(expert, slot, w_hbm_refs, w_buf_refs, sems):
    return [pltpu.make_async_copy(w_hbm.at[expert], w_buf.at[slot], sems.at[slot, n])
            for n, (w_hbm, w_buf) in enumerate(zip(w_hbm_refs, w_buf_refs))]


def _experts_kernel(first_ref, exp_ref, lo_ref, hi_ref, next_ref, slot_ref,
                    xs_hbm, wg_hbm, wu_hbm, wd_hbm, ys_ref,
                    xbuf, x_ref, acc_ref, wg_buf, wu_buf, wd_buf, wg_bf_ref, wu_bf_ref, wd_bf_ref, sems, xsems):
    blk = pl.program_id(0)
    n_blocks = pl.num_programs(0)
    w_hbm_refs = (wg_hbm, wu_hbm, wd_hbm)
    w_buf_refs = (wg_buf, wu_buf, wd_buf)
    block_tiles = ROW_BLOCK * SUBLANES

    def block_copy(b):
        slot = b % X_RING
        start = pl.multiple_of(b * block_tiles, block_tiles)
        return pltpu.make_async_copy(xs_hbm.at[pl.ds(start, block_tiles), :], xbuf.at[slot], xsems.at[slot])

    @pl.when(blk == 0)
    def _():
        for b in range(X_RING - 1):
            block_copy(b).start()

    @pl.when(blk + X_RING - 1 < n_blocks)
    def _():
        block_copy(blk + X_RING - 1).start()

    block_copy(blk).wait()
    x_ref[...] = _tile_rows_load(xbuf.at[blk % X_RING], 0, ROW_BLOCK).astype(BF16)
    rows = blk * ROW_BLOCK + lax.broadcasted_iota(jnp.int32, (ROW_BLOCK, 1), 0)

    def item(i):
        expert = exp_ref[i]
        slot = slot_ref[i]

        @pl.when(i == 0)
        def _():
            for copy in _weight_copies(expert, slot, w_hbm_refs, w_buf_refs, sems):
                copy.start()

        @pl.when((i == 0) | (exp_ref[jnp.maximum(i - 1, 0)] != expert))
        def _():
            following = next_ref[i]

            @pl.when(following < N_EXPERTS)
            def _():
                for copy in _weight_copies(following, 1 - slot, w_hbm_refs, w_buf_refs, sems):
                    copy.start(priority=1)

            for copy in _weight_copies(expert, slot, w_hbm_refs, w_buf_refs, sems):
                copy.wait()
            wg_bf_ref[...] = wg_buf[slot].astype(BF16)
            wu_bf_ref[...] = wu_buf[slot].astype(BF16)
            wd_bf_ref[...] = wd_buf[slot].astype(BF16)

        x = x_ref[...]
        gate = _dot(x, wg_bf_ref[...])
        up = _dot(x, wu_bf_ref[...])
        hidden = (gate * _sigmoid(gate) * up).astype(BF16)
        y = _dot(hidden, wd_bf_ref[...])
        return y, (rows >= lo_ref[i]) & (rows < hi_ref[i])

    def middle_item(i, carry):
        y, keep = item(i)
        acc_ref[...] = jnp.where(keep, y, acc_ref[...])
        return carry

    first = first_ref[blk]
    last = first_ref[blk + 1] - 1

    @pl.when(first == last)
    def _():
        y, keep = item(first)
        _tile_rows_store(ys_ref, 0, ROW_BLOCK, jnp.where(keep, y, 0.0))

    @pl.when(first < last)
    def _():
        y, keep = item(first)
        acc_ref[...] = jnp.where(keep, y, 0.0)
        lax.fori_loop(first + 1, last, middle_item, 0)
        y, keep = item(last)
        _tile_rows_store(ys_ref, 0, ROW_BLOCK, jnp.where(keep, y, acc_ref[...]))


def _experts(block_first, item_exp, item_lo, item_hi, item_next, item_slot, xs, w_gate, w_up, w_down):
    n_rows = xs.shape[0] // SUBLANES
    block = (ROW_BLOCK * SUBLANES, LANES)
    grid_spec = pltpu.PrefetchScalarGridSpec(
        num_scalar_prefetch=6,
        grid=(n_rows // ROW_BLOCK,),
        in_specs=[
            pl.BlockSpec(memory_space=pl.ANY),
            pl.BlockSpec(memory_space=pl.ANY),
            pl.BlockSpec(memory_space=pl.ANY),
            pl.BlockSpec(memory_space=pl.ANY),
        ],
        out_specs=pl.BlockSpec(block, lambda b, *_: (b, 0)),
        scratch_shapes=[
            pltpu.VMEM((X_RING,) + block, F32),
            pltpu.VMEM((ROW_BLOCK, D_MODEL), BF16),
            pltpu.VMEM((ROW_BLOCK, D_MODEL), F32),
            pltpu.VMEM((2, D_MODEL, D_EXPERT), F32),
            pltpu.VMEM((2, D_MODEL, D_EXPERT), F32),
            pltpu.VMEM((2, D_EXPERT, D_MODEL), F32),
            pltpu.VMEM((D_MODEL, D_EXPERT), BF16),
            pltpu.VMEM((D_MODEL, D_EXPERT), BF16),
            pltpu.VMEM((D_EXPERT, D_MODEL), BF16),
            pltpu.SemaphoreType.DMA((2, 3)),
            pltpu.SemaphoreType.DMA((X_RING,)),
        ],
    )
    assert n_rows // ROW_BLOCK >= X_RING
    return pl.pallas_call(
        _experts_kernel,
        grid_spec=grid_spec,
        out_shape=jax.ShapeDtypeStruct((n_rows * SUBLANES, LANES), F32),
        compiler_params=_params(("arbitrary",)),
        name="experts",
    )(block_first, item_exp, item_lo, item_hi, item_next, item_slot, xs, w_gate, w_up, w_down)


def _combine_kernel(slot0_ref, slot_next_ref, gw_ref, h1_ref, ys_ref, wsg_ref, wsu_ref, wsd_ref,
                    g2_ref, b2_ref, out_ref, buf_ref, sems):
    tm = h1_ref.shape[0]
    i = pl.program_id(0)
    cur = i % 2

    def gather(slot_ref, buf, sem):
        def issue(t, carry):
            for k in range(TOP_K):
                pltpu.make_async_copy(
                    _token_tile(ys_ref, slot_ref[k * tm + t]), _token_tile(buf, k * tm + t), sem).start(priority=k % 2)
            return carry

        lax.fori_loop(0, tm, issue, 0, unroll=ISSUE_UNROLL)

    @pl.when(i == 0)
    def _():
        gather(slot0_ref, buf_ref.at[0], sems.at[0])

    @pl.when(i + 1 < pl.num_programs(0))
    def _():
        gather(slot_next_ref, buf_ref.at[1 - cur], sems.at[1 - cur])

    h1 = h1_ref[...]
    hb = h1.astype(BF16)
    gate = _dot(hb, wsg_ref[...])
    hidden = (gate * _sigmoid(gate) * _dot(hb, wsu_ref[...])).astype(BF16)
    moe = _dot(hidden, wsd_ref[...])

    buf = buf_ref.at[cur]
    for k in range(TOP_K):
        pltpu.make_async_copy(
            ys_ref.at[pl.ds(0, tm * SUBLANES), :], buf.at[pl.ds(k * tm * SUBLANES, tm * SUBLANES), :],
            sems.at[cur]).wait()

    gw = gw_ref[...]
    for k in range(TOP_K):
        moe = moe + gw[:, k:k + 1] * _tile_rows_load(buf, k * tm, tm)
    out_ref[...] = _layer_norm(DN_ALPHA * h1 + moe, g2_ref[...], b2_ref[...])


def _combine(slots_tiled, gw, h1, ys, w_sg, w_su, w_sd, g2, b2):
    T = h1.shape[0]
    tm = TOKEN_TILE
    row = lambda i: (i, 0)
    n_steps = T // tm
    return pl.pallas_call(
        _combine_kernel,
        grid=(n_steps,),
        in_specs=[
            pl.BlockSpec((TOP_K * tm,), lambda i: (0,), memory_space=pltpu.SMEM),
            pl.BlockSpec((TOP_K * tm,), lambda i: (jnp.minimum(i + 1, n_steps - 1),), memory_space=pltpu.SMEM),
            pl.BlockSpec((tm, LANES), row),
            pl.BlockSpec((tm, D_MODEL), row),
            pl.BlockSpec(memory_space=pl.ANY),
            _const_spec((D_MODEL, D_EXPERT)),
            _const_spec((D_MODEL, D_EXPERT)),
            _const_spec((D_EXPERT, D_MODEL)),
            _const_spec((1, D_MODEL)),
            _const_spec((1, D_MODEL)),
        ],
        out_specs=pl.BlockSpec((tm, D_MODEL), row),
        out_shape=jax.ShapeDtypeStruct((T, D_MODEL), F32),
        scratch_shapes=[pltpu.VMEM((2, TOP_K * tm * SUBLANES, LANES), F32), pltpu.SemaphoreType.DMA((2,))],
        compiler_params=_params(("arbitrary",)),
        name="combine",
    )(slots_tiled, slots_tiled, gw, h1, ys, w_sg, w_su, w_sd, g2, b2)


def _expert_work_items(counts, n_rows):
    n_blocks = n_rows // ROW_BLOCK
    n_items = n_blocks + N_EXPERTS
    ends = jnp.cumsum(counts)
    starts = ends - counts
    first_blk = starts // ROW_BLOCK
    last_blk = jnp.maximum(ends - 1, 0) // ROW_BLOCK
    per_expert = jnp.where(counts > 0, last_blk - first_blk + 1, 0)
    item_end = jnp.cumsum(per_expert)
    item_start = item_end - per_expert
    total = item_end[-1]
    i = jnp.arange(n_items, dtype=jnp.int32)
    live = i < total
    j = jnp.minimum(i, total - 1)[:, None]
    owner = (item_start[None, :] <= j) & (j < item_end[None, :])

    def pick(per_expert_values):
        return jnp.sum(jnp.where(owner, per_expert_values[None, :], 0), axis=1)

    expert_ids = jnp.arange(N_EXPERTS, dtype=jnp.int32)
    nonempty = counts > 0
    later = (expert_ids[None, :] > expert_ids[:, None]) & nonempty[None, :]
    following = jnp.min(jnp.where(later, expert_ids[None, :], N_EXPERTS), axis=1)
    buffer_slot = (jnp.cumsum(nonempty) - 1) % 2
    exp = pick(expert_ids)
    blk = pick(first_blk - item_start) + j[:, 0]
    lo = jnp.maximum(pick(starts), blk * ROW_BLOCK)
    hi = jnp.minimum(pick(ends), (blk + 1) * ROW_BLOCK)
    lo = jnp.where(live, lo, 0)
    hi = jnp.where(live, hi, 0)
    block_ids = jnp.arange(n_blocks + 1, dtype=jnp.int32)
    block_first = jnp.sum(live[None, :] & (blk[None, :] < block_ids[:, None]), axis=1)
    as_i32 = lambda a: a.astype(jnp.int32)
    return (as_i32(starts), as_i32(block_first), as_i32(exp), as_i32(lo), as_i32(hi),
            as_i32(pick(following)), as_i32(pick(buffer_slot)))


def kernel(x, positions, w_in, gmlp_ln_g, gmlp_ln_b, w_spatial, b_spatial, w_a_proj, q_norm_g, w_uq, kv_norm_g, w_uk, w_uv, w_b_proj, w_out, ln1_g, ln1_b, w_router, router_bias, w_gate, w_up, w_down, w_sh_gate, w_sh_up, w_sh_down, ln2_g, ln2_b):
    B, S, D = x.shape
    T = B * S
    x2 = x.reshape(T, D)

    w = w_in[0]
    o_u, o_v, o_cq = 0, A_WIDTH, 2 * A_WIDTH
    o_ckv = o_cq + Q_LORA
    o_kr = o_ckv + KV_LORA
    o_ga = o_kr + QK_ROPE
    o_gb = o_ga + D_MODEL
    half_r = QK_ROPE // 2
    w_kr = w[:, o_kr:o_ga]
    w_kr_partner = jnp.concatenate([w_kr[:, half_r:], w_kr[:, :half_r]], axis=1)
    w_b = jnp.concatenate(
        [w[:, o_cq:o_kr], w_kr, w_kr, w_kr_partner, w_kr_partner, w[:, o_gb:]], axis=1).astype(BF16)
    uq = w_uq[0]
    uq_rope = uq[:, :, QK_NOPE:]
    uq_partner = jnp.concatenate([uq_rope[:, :, half_r:], uq_rope[:, :, :half_r]], axis=2)
    w_uqt = jnp.concatenate([uq, uq_partner], axis=2).reshape(Q_LORA, N_HEADS * QK_PAD).T.astype(BF16)
    w_ukm = w_uk[0].reshape(KV_LORA, N_HEADS * QK_NOPE).astype(BF16)
    w_uvt = w_uv[0].reshape(KV_LORA, N_HEADS * V_HEAD).T.astype(BF16)

    freq = ROPE_THETA ** (-jnp.arange(0, half_r, dtype=F32) * 2.0 / QK_ROPE)
    pos_f = positions.astype(F32).reshape(T)
    bias_full = jnp.repeat(b_spatial[0].T, A_GROUP_DIM, axis=1)

    ga = _gmlp_branch(
        x2, w[:, o_u:o_v].astype(BF16), w[:, o_v:o_cq].astype(BF16), w[:, o_ga:o_gb].astype(BF16),
        gmlp_ln_g[0].reshape(1, A_WIDTH), gmlp_ln_b[0].reshape(1, A_WIDTH),
        w_spatial[0].astype(BF16), bias_full, w_a_proj[0].astype(BF16))

    q_t, k, v_t, sgb = _mla_prep(
        x2, pos_f.reshape(T // TOKEN_TILE, 1, TOKEN_TILE), w_b,
        q_norm_g[0].reshape(1, Q_LORA), kv_norm_g[0].reshape(1, KV_LORA), w_uqt, w_ukm, w_uvt,
        freq.reshape(half_r, 1), B, S)

    o = _attention(q_t, k, v_t).reshape(T, N_HEADS * V_HEAD)

    h1, h1t, idx_t, rank_t, gw, cnt = _mix_route(
        x2, ga, sgb, o, w_b_proj[0].astype(BF16), w_out[0].astype(BF16),
        ln1_g[0].reshape(1, D), ln1_b[0].reshape(1, D), w_router[0].astype(BF16),
        router_bias[0].reshape(N_EXPERTS, 1))

    counts = cnt[:, 0].astype(jnp.int32)
    starts, block_first, item_exp, item_lo, item_hi, item_next, item_slot = _expert_work_items(counts, T * TOP_K)

    slots = _slot_index(starts, idx_t, rank_t)
    slots_tiled = slots.reshape(TOP_K, T // TOKEN_TILE, TOKEN_TILE).transpose(1, 0, 2).reshape(T * TOP_K)

    xs = _dispatch(slots_tiled, h1t)
    ys = _experts(block_first, item_exp, item_lo, item_hi, item_next, item_slot, xs, w_gate[0], w_up[0], w_down[0])
    out = _combine(
        slots_tiled, gw, h1, ys, w_sh_gate[0].astype(BF16), w_sh_up[0].astype(BF16),
        w_sh_down[0].astype(BF16), ln2_g[0].reshape(1, D), ln2_b[0].reshape(1, D))
    return out.reshape(B, S, D)
```

```python
import functools
import math

import jax
import jax.numpy as jnp
from jax import lax
from jax.experimental import pallas as pl
from jax.experimental.pallas import tpu as pltpu

D_MODEL = 1024
CHUNK = 128
A_GROUPS = 8
A_GROUP_DIM = 128
A_WIDTH = A_GROUPS * A_GROUP_DIM
N_HEADS = 8
QK_NOPE = 128
QK_ROPE = 64
QK_DIM = QK_NOPE + QK_ROPE
V_HEAD = 128
V_ROWS = 144
Q_LORA = 384
KV_LORA = 256
ROPE_THETA = 10000.0
N_EXPERTS = 256
TOP_K = 8
N_GROUPS = 8
TOPK_GROUPS = 4
GROUP_SIZE = N_EXPERTS // N_GROUPS
D_EXPERT = 256
ROUTED_SCALE = 2.5
DN_ALPHA = 2.0 ** 0.25
LN_EPS = 1e-5
RMS_EPS = 1e-6

QK_PAD = 256
LANES = 128
TOKEN_TILE = 512
Q_TILE = 1024
KV_TILE = 256
KV_GROUP = 8
SLOT_LANES = 2048
ISSUE_UNROLL = 4
ROW_BLOCK = 512
X_RING = 3
SUBLANES = 8
VMEM_LIMIT = 56 * 1024 * 1024

F32 = jnp.float32
BF16 = jnp.bfloat16
NEG_INF = float("-inf")


def _dot(a, b):
    return jnp.dot(a, b, preferred_element_type=F32)


def _sigmoid(x):
    return 1.0 / (1.0 + jnp.exp(-x))


def _gelu_tanh(x):
    return 0.5 * x * (1.0 + jnp.tanh(math.sqrt(2.0 / math.pi) * (x + 0.044715 * (x * x * x))))


def _layer_norm(x, g, b):
    mu = jnp.mean(x, axis=-1, keepdims=True)
    d = x - mu
    var = jnp.mean(d * d, axis=-1, keepdims=True)
    return d * lax.rsqrt(var + LN_EPS) * g + b


def _rms_norm(x, g):
    return x * lax.rsqrt(jnp.mean(x * x, axis=-1, keepdims=True) + RMS_EPS) * g


def _tile_rows_load(ref, first, n):
    return jnp.concatenate(
        [ref[pl.ds(first * SUBLANES + j, n, stride=SUBLANES), :] for j in range(SUBLANES)], axis=1)


def _tile_rows_store(ref, first, n, value):
    for j in range(SUBLANES):
        ref[pl.ds(first * SUBLANES + j, n, stride=SUBLANES), :] = value[:, j * LANES:(j + 1) * LANES]


def _const_spec(shape):
    zeros = (0,) * len(shape)
    return pl.BlockSpec(shape, lambda *_: zeros)


def _params(semantics):
    return pltpu.CompilerParams(dimension_semantics=semantics, vmem_limit_bytes=VMEM_LIMIT)


def _gmlp_kernel(x_ref, wu_ref, wv_ref, wg_ref, lng_ref, lnb_ref, ws_ref, bias_ref, wa_ref, o_ref, mix_ref):
    tm = x_ref.shape[0]
    xb = x_ref[...].astype(BF16)
    v = _gelu_tanh(_dot(xb, wv_ref[...]))
    for h in range(A_GROUPS):
        cols = slice(h * A_GROUP_DIM, (h + 1) * A_GROUP_DIM)
        vn = _layer_norm(v[:, cols], lng_ref[:, cols], lnb_ref[:, cols]).astype(BF16)
        n_chunks = tm // CHUNK
        side_by_side = jnp.concatenate([vn[c * CHUNK:(c + 1) * CHUNK, :] for c in range(n_chunks)], axis=1)
        mixed = _dot(ws_ref[h], side_by_side)
        for c in range(n_chunks):
            mix_ref[c * CHUNK:(c + 1) * CHUNK, cols] = (
                mixed[:, c * A_GROUP_DIM:(c + 1) * A_GROUP_DIM] + bias_ref[:, cols])
    u = _gelu_tanh(_dot(xb, wu_ref[...]))
    ya = _dot((u * mix_ref[...]).astype(BF16), wa_ref[...])
    gate = _sigmoid(_dot(xb, wg_ref[...]))
    o_ref[...] = (gate * ya).astype(BF16)


def _gmlp_branch(x2, w_u, w_v, w_ga, ln_g, ln_b, w_s, bias_full, w_a):
    T = x2.shape[0]
    tm = TOKEN_TILE
    return pl.pallas_call(
        _gmlp_kernel,
        grid=(T // tm,),
        in_specs=[
            pl.BlockSpec((tm, D_MODEL), lambda i: (i, 0)),
            _const_spec((D_MODEL, A_WIDTH)),
            _const_spec((D_MODEL, A_WIDTH)),
            _const_spec((D_MODEL, D_MODEL)),
            _const_spec((1, A_WIDTH)),
            _const_spec((1, A_WIDTH)),
            _const_spec((A_GROUPS, CHUNK, CHUNK)),
            _const_spec((CHUNK, A_WIDTH)),
            _const_spec((A_WIDTH, D_MODEL)),
        ],
        out_specs=pl.BlockSpec((tm, D_MODEL), lambda i: (i, 0)),
        out_shape=jax.ShapeDtypeStruct((T, D_MODEL), BF16),
        scratch_shapes=[pltpu.VMEM((tm, A_WIDTH), F32)],
        compiler_params=_params(("arbitrary",)),
        name="gmlp_branch",
    )(x2, w_u, w_v, w_ga, ln_g, ln_b, w_s, bias_full, w_a)


def _mla_prep_kernel(x_ref, pos_ref, wb_ref, qg_ref, kvg_ref, wuqt_ref, wuk_ref, wuvt_ref, freq_ref,
                     qt_ref, k_ref, vt_ref, sgb_ref):
    xb = x_ref[...].astype(BF16)
    pb = _dot(xb, wb_ref[...])
    c_q = pb[:, :Q_LORA]
    c_kv = pb[:, Q_LORA:Q_LORA + KV_LORA]
    o = Q_LORA + KV_LORA
    kr = pb[:, o:o + LANES]
    kr_partner = pb[:, o + LANES:o + 2 * LANES]
    gb = pb[:, o + 2 * LANES:]
    sgb_ref[...] = _sigmoid(gb).astype(BF16)

    cqn = _rms_norm(c_q, qg_ref[...])
    ckvn = _rms_norm(c_kv, kvg_ref[...])
    cqn_t = cqn.T.astype(BF16)
    ckvn_t = ckvn.T.astype(BF16)

    scale = math.log2(math.e) / math.sqrt(QK_DIM)
    ang = freq_ref[...] * pos_ref[0]
    cos_t = jnp.cos(ang)
    sin_t = jnp.sin(ang)
    table_t = jnp.concatenate([cos_t, cos_t, -sin_t, sin_t], axis=0) * scale
    q_t = _dot(wuqt_ref[...], cqn_t)
    for h in range(N_HEADS):
        blk = q_t[h * QK_PAD:(h + 1) * QK_PAD]
        qt_ref[0, h] = jnp.concatenate(
            [blk[:QK_NOPE] * scale, blk[QK_NOPE:] * table_t], axis=0).astype(BF16)

    v_t = _dot(wuvt_ref[...], ckvn_t)
    ones_rows = jnp.ones((V_ROWS - V_HEAD, v_t.shape[1]), F32)
    for h in range(N_HEADS):
        vt_ref[0, h] = jnp.concatenate([v_t[h * V_HEAD:(h + 1) * V_HEAD], ones_rows], axis=0).astype(BF16)

    cos_k = jnp.concatenate([cos_t, cos_t, cos_t, cos_t], axis=0).T
    sin_k = jnp.concatenate([-sin_t, sin_t, -sin_t, sin_t], axis=0).T
    k_rope = (kr * cos_k + kr_partner * sin_k).astype(BF16)
    k_nope = _dot(ckvn.astype(BF16), wuk_ref[...])
    for h in range(N_HEADS):
        k_ref[0, h] = jnp.concatenate(
            [k_nope[:, h * QK_NOPE:(h + 1) * QK_NOPE].astype(BF16), k_rope], axis=1)


def _mla_prep(x2, pos_row, w_b, q_g, kv_g, w_uqt, w_uk, w_uvt, freq_col, B, S):
    T = x2.shape[0]
    tm = TOKEN_TILE
    spb = S // tm
    n_b = w_b.shape[1]
    head_map = lambda i: (i // spb, 0, 0, i % spb)
    return pl.pallas_call(
        _mla_prep_kernel,
        grid=(T // tm,),
        in_specs=[
            pl.BlockSpec((tm, D_MODEL), lambda i: (i, 0)),
            pl.BlockSpec((1, 1, tm), lambda i: (i, 0, 0)),
            _const_spec((D_MODEL, n_b)),
            _const_spec((1, Q_LORA)),
            _const_spec((1, KV_LORA)),
            _const_spec((N_HEADS * QK_PAD, Q_LORA)),
            _const_spec((KV_LORA, N_HEADS * QK_NOPE)),
            _const_spec((N_HEADS * V_HEAD, KV_LORA)),
            _const_spec((QK_ROPE // 2, 1)),
        ],
        out_specs=[
            pl.BlockSpec((1, N_HEADS, QK_PAD, tm), head_map),
            pl.BlockSpec((1, N_HEADS, tm, QK_PAD), lambda i: (i // spb, 0, i % spb, 0)),
            pl.BlockSpec((1, N_HEADS, V_ROWS, tm), head_map),
            pl.BlockSpec((tm, D_MODEL), lambda i: (i, 0)),
        ],
        out_shape=[
            jax.ShapeDtypeStruct((B, N_HEADS, QK_PAD, S), BF16),
            jax.ShapeDtypeStruct((B, N_HEADS, S, QK_PAD), BF16),
            jax.ShapeDtypeStruct((B, N_HEADS, V_ROWS, S), BF16),
            jax.ShapeDtypeStruct((T, D_MODEL), BF16),
        ],
        compiler_params=_params(("arbitrary",)),
        name="mla_prep",
    )(x2, pos_row, w_b, q_g, kv_g, w_uqt, w_uk, w_uvt, freq_col)


def _attention_kernel(qt_ref, k_ref, vt_ref, o_ref, s0_ref, s1_ref, m_ref, acc_ref):
    S = k_ref.shape[2]
    tq = s0_ref.shape[1]
    n_chunks = S // KV_TILE
    n_q = S // tq
    bufs = (s0_ref, s1_ref)

    def scores(qi, c, dst_ref):
        q_t = qt_ref[0, 0, :, pl.ds(pl.multiple_of(qi * tq, tq), tq)]
        start = pl.multiple_of(c * KV_TILE, KV_TILE)
        dst_ref[...] = _dot(k_ref[0, 0, pl.ds(start, KV_TILE), :], q_t)

    def accumulate(c, src_ref):
        start = pl.multiple_of(c * KV_TILE, KV_TILE)
        s_t = src_ref[...]
        m = m_ref[...]
        m_new = jnp.maximum(m, jnp.max(s_t, axis=0, keepdims=True))
        p_t = jnp.exp2(s_t - m_new)
        alpha = jnp.exp2(m - m_new)
        m_ref[...] = m_new
        acc_ref[...] = alpha * acc_ref[...] + _dot(vt_ref[0, 0, :, pl.ds(start, KV_TILE)], p_t.astype(BF16))

    def group(qi, c0, last_of_tile):
        for i in range(KV_GROUP):
            if i + 1 < KV_GROUP or not last_of_tile:
                scores(qi, c0 + i + 1, bufs[(i + 1) % 2])
            else:
                scores(jnp.minimum(qi + 1, n_q - 1), 0, bufs[(i + 1) % 2])
            accumulate(c0 + i, bufs[i % 2])

    scores(0, 0, s0_ref)

    def q_tile(qi, carry):
        m_ref[...] = jnp.full(m_ref.shape, NEG_INF, F32)
        acc_ref[...] = jnp.zeros(acc_ref.shape, F32)

        def body(g, c):
            group(qi, g * KV_GROUP, False)
            return c

        lax.fori_loop(0, n_chunks // KV_GROUP - 1, body, 0)
        group(qi, n_chunks - KV_GROUP, True)
        o_ref[0, pl.ds(pl.multiple_of(qi * tq, tq), tq), :] = (
            acc_ref[:V_HEAD] / acc_ref[V_HEAD:V_HEAD + 1]).T.astype(BF16)
        return carry

    lax.fori_loop(0, n_q, q_tile, 0)


def _attention(q_t, k, v_t):
    B, H, _, S = q_t.shape
    tq = min(Q_TILE, S)
    assert S % tq == 0 and S % (KV_TILE * KV_GROUP) == 0 and KV_GROUP % 2 == 0
    return pl.pallas_call(
        _attention_kernel,
        grid=(B, H),
        in_specs=[
            pl.BlockSpec((1, 1, QK_PAD, S), lambda b, h: (b, h, 0, 0)),
            pl.BlockSpec((1, 1, S, QK_PAD), lambda b, h: (b, h, 0, 0)),
            pl.BlockSpec((1, 1, V_ROWS, S), lambda b, h: (b, h, 0, 0)),
        ],
        out_specs=pl.BlockSpec((1, S, V_HEAD), lambda b, h: (b, 0, h)),
        out_shape=jax.ShapeDtypeStruct((B, S, H * V_HEAD), BF16),
        scratch_shapes=[
            pltpu.VMEM((KV_TILE, tq), F32),
            pltpu.VMEM((KV_TILE, tq), F32),
            pltpu.VMEM((1, tq), F32),
            pltpu.VMEM((V_ROWS, tq), F32),
        ],
        compiler_params=_params(("arbitrary", "arbitrary")),
        name="attention",
    )(q_t, k, v_t)


def _mix_route_kernel(x_ref, ga_ref, sgb_ref, o_ref, wbp_ref, wout_ref, g1_ref, b1_ref, wr_ref, rb_ref,
                      h1_ref, h1t_ref, idx_ref, rank_ref, gw_ref, cnt_ref, base_ref):
    tm = x_ref.shape[0]

    @pl.when(pl.program_id(0) == 0)
    def _():
        base_ref[...] = jnp.zeros_like(base_ref)

    yb = _dot(o_ref[...], wbp_ref[...])
    merged = ga_ref[...].astype(F32) + sgb_ref[...].astype(F32) * yb
    mixed = _dot(merged.astype(BF16), wout_ref[...])
    h1 = _layer_norm(DN_ALPHA * x_ref[...] + mixed, g1_ref[...], b1_ref[...])
    h1_ref[...] = h1
    _tile_rows_store(h1t_ref, 0, tm, h1)

    scores_t = _sigmoid(_dot(h1.astype(BF16), wr_ref[...])).T
    biased = scores_t + rb_ref[...]

    riota = lax.broadcasted_iota(jnp.int32, (GROUP_SIZE, tm), 0)
    blocks, gscore = [], []
    for g in range(N_GROUPS):
        blk = biased[g * GROUP_SIZE:(g + 1) * GROUP_SIZE]
        m1 = jnp.max(blk, axis=0, keepdims=True)
        i1 = jnp.min(jnp.where(blk == m1, riota, GROUP_SIZE), axis=0, keepdims=True)
        m2 = jnp.max(jnp.where(riota == i1, NEG_INF, blk), axis=0, keepdims=True)
        blocks.append(blk)
        gscore.append(m1 + m2)

    selected = [jnp.zeros((1, tm), F32) for _ in range(N_GROUPS)]
    for _ in range(TOPK_GROUPS):
        best = functools.reduce(jnp.maximum, gscore)
        first = functools.reduce(
            jnp.minimum, [jnp.where(gscore[g] == best, g, N_GROUPS) for g in range(N_GROUPS)])
        for g in range(N_GROUPS):
            hit = first == g
            selected[g] = jnp.where(hit, 1.0, selected[g])
            gscore[g] = jnp.where(hit, NEG_INF, gscore[g])

    masked = jnp.concatenate(
        [jnp.where(jnp.broadcast_to(selected[g], blocks[g].shape) > 0.5, blocks[g], NEG_INF)
         for g in range(N_GROUPS)], axis=0)

    eiota = lax.broadcasted_iota(jnp.int32, (N_EXPERTS, tm), 0)
    picks, weights = [], []
    chosen = jnp.zeros((N_EXPERTS, tm), F32)
    for _ in range(TOP_K):
        best = jnp.max(masked, axis=0, keepdims=True)
        e = jnp.min(jnp.where(masked == best, eiota, N_EXPERTS), axis=0, keepdims=True)
        hit = eiota == e
        masked = jnp.where(hit, NEG_INF, masked)
        chosen = jnp.where(hit, 1.0, chosen)
        picks.append(e)
        weights.append(jnp.sum(jnp.where(hit, scores_t, 0.0), axis=0, keepdims=True))
    wsum = functools.reduce(lambda a, b: a + b, weights)

    before = (lax.broadcasted_iota(jnp.int32, (tm, tm), 0)
              < lax.broadcasted_iota(jnp.int32, (tm, tm), 1)).astype(BF16)
    rank_full = _dot(chosen.astype(BF16), before) + base_ref[:, :1]
    new_base = base_ref[:, :1] + jnp.sum(chosen, axis=1, keepdims=True)
    base_ref[...] = jnp.broadcast_to(new_base, base_ref.shape)
    cnt_ref[...] = jnp.broadcast_to(new_base, cnt_ref.shape)

    kiota = lax.broadcasted_iota(jnp.int32, (TOP_K, tm), 0)
    wiota = lax.broadcasted_iota(jnp.int32, (LANES, tm), 0)
    idx_out = jnp.zeros((TOP_K, tm), jnp.int32)
    rank_out = jnp.zeros((TOP_K, tm), jnp.int32)
    gw_t = jnp.zeros((LANES, tm), F32)
    for k in range(TOP_K):
        r = jnp.sum(jnp.where(eiota == picks[k], rank_full, 0.0), axis=0, keepdims=True)
        idx_out = jnp.where(kiota == k, picks[k], idx_out)
        rank_out = jnp.where(kiota == k, r.astype(jnp.int32), rank_out)
        gw_t = jnp.where(wiota == k, weights[k] / wsum * ROUTED_SCALE, gw_t)
    idx_ref[...] = idx_out
    rank_ref[...] = rank_out
    gw_ref[...] = gw_t.T


def _mix_route(x2, ga, sgb, o, w_bp, w_out, g1, b1, w_r, r_bias):
    T = x2.shape[0]
    tm = TOKEN_TILE
    row = lambda i: (i, 0)
    col = lambda i: (0, i)
    return pl.pallas_call(
        _mix_route_kernel,
        grid=(T // tm,),
        in_specs=[
            pl.BlockSpec((tm, D_MODEL), row),
            pl.BlockSpec((tm, D_MODEL), row),
            pl.BlockSpec((tm, D_MODEL), row),
            pl.BlockSpec((tm, D_MODEL), row),
            _const_spec((D_MODEL, D_MODEL)),
            _const_spec((D_MODEL, D_MODEL)),
            _const_spec((1, D_MODEL)),
            _const_spec((1, D_MODEL)),
            _const_spec((D_MODEL, N_EXPERTS)),
            _const_spec((N_EXPERTS, 1)),
        ],
        out_specs=[
            pl.BlockSpec((tm, D_MODEL), row),
            pl.BlockSpec((tm * SUBLANES, LANES), row),
            pl.BlockSpec((TOP_K, tm), col),
            pl.BlockSpec((TOP_K, tm), col),
            pl.BlockSpec((tm, LANES), row),
            _const_spec((N_EXPERTS, LANES)),
        ],
        out_shape=[
            jax.ShapeDtypeStruct((T, D_MODEL), F32),
            jax.ShapeDtypeStruct((T * SUBLANES, LANES), F32),
            jax.ShapeDtypeStruct((TOP_K, T), jnp.int32),
            jax.ShapeDtypeStruct((TOP_K, T), jnp.int32),
            jax.ShapeDtypeStruct((T, LANES), F32),
            jax.ShapeDtypeStruct((N_EXPERTS, LANES), F32),
        ],
        scratch_shapes=[pltpu.VMEM((N_EXPERTS, LANES), F32)],
        compiler_params=_params(("arbitrary",)),
        name="mix_route",
    )(x2, ga, sgb, o, w_bp, w_out, g1, b1, w_r, r_bias)


def _slot_kernel(starts_ref, idx_ref, rank_ref, slot_ref):
    idx = idx_ref[...]

    def lookup(e, acc):
        return jnp.where(idx == e, starts_ref[e], acc)

    slot_ref[...] = lax.fori_loop(0, N_EXPERTS, lookup, jnp.zeros_like(idx), unroll=8) + rank_ref[...]


def _slot_index(starts, idx_t, rank_t):
    T = idx_t.shape[1]
    tl = min(SLOT_LANES, T)
    col = lambda i: (0, i)
    return pl.pallas_call(
        _slot_kernel,
        grid=(T // tl,),
        in_specs=[
            pl.BlockSpec(memory_space=pltpu.SMEM),
            pl.BlockSpec((TOP_K, tl), col),
            pl.BlockSpec((TOP_K, tl), col),
        ],
        out_specs=pl.BlockSpec((TOP_K, tl), col),
        out_shape=jax.ShapeDtypeStruct((TOP_K, T), jnp.int32),
        compiler_params=_params(("arbitrary",)),
        name="slot_index",
    )(starts, idx_t, rank_t)


def _token_tile(ref, token):
    return ref.at[pl.ds(pl.multiple_of(token * SUBLANES, SUBLANES), SUBLANES), :]


def _dispatch_kernel(slot_ref, h1t_ref, xs_ref, sem):
    tm = h1t_ref.shape[0] // SUBLANES

    def issue(t, carry):
        src = _token_tile(h1t_ref, t)
        for k in range(TOP_K):
            pltpu.make_async_copy(src, _token_tile(xs_ref, slot_ref[k * tm + t]), sem).start(priority=k % 2)
        return carry

    lax.fori_loop(0, tm, issue, 0, unroll=ISSUE_UNROLL)
    for _ in range(TOP_K):
        pltpu.make_async_copy(h1t_ref, xs_ref.at[pl.ds(0, tm * SUBLANES), :], sem).wait()


def _dispatch(slots_tiled, h1t):
    T = h1t.shape[0] // SUBLANES
    tm = TOKEN_TILE
    return pl.pallas_call(
        _dispatch_kernel,
        grid=(T // tm,),
        in_specs=[
            pl.BlockSpec((TOP_K * tm,), lambda i: (i,), memory_space=pltpu.SMEM),
            pl.BlockSpec((tm * SUBLANES, LANES), lambda i: (i, 0)),
        ],
        out_specs=pl.BlockSpec(memory_space=pl.ANY),
        out_shape=jax.ShapeDtypeStruct((T * TOP_K * SUBLANES, LANES), F32),
        scratch_shapes=[pltpu.SemaphoreType.DMA(())],
        compiler_params=_params(("arbitrary",)),
        name="dispatch",
    )(slots_tiled, h1t)


def _weight_copies(expert, slot, w_hbm_refs, w_buf_refs, sems):
    return [pltpu.make_async_copy(w_hbm.at[expert], w_buf.at[slot], sems.at[slot, n])
            for n, (w_hbm, w_buf) in enumerate(zip(w_hbm_refs, w_buf_refs))]


def _experts_kernel(first_ref, exp_ref, lo_ref, hi_ref, next_ref, slot_ref,
                    xs_hbm, wg_hbm, wu_hbm, wd_hbm, ys_ref,
                    xbuf, x_ref, acc_ref, wg_buf, wu_buf, wd_buf, wg_bf_ref, wu_bf_ref, wd_bf_ref, sems, xsems):
    blk = pl.program_id(0)
    n_blocks = pl.num_programs(0)
    w_hbm_refs = (wg_hbm, wu_hbm, wd_hbm)
    w_buf_refs = (wg_buf, wu_buf, wd_buf)
    block_tiles = ROW_BLOCK * SUBLANES

    def block_copy(b):
        slot = b % X_RING
        start = pl.multiple_of(b * block_tiles, block_tiles)
        return pltpu.make_async_copy(xs_hbm.at[pl.ds(start, block_tiles), :], xbuf.at[slot], xsems.at[slot])

    @pl.when(blk == 0)
    def _():
        for b in range(X_RING - 1):
            block_copy(b).start()

    @pl.when(blk + X_RING - 1 < n_blocks)
    def _():
        block_copy(blk + X_RING - 1).start()

    block_copy(blk).wait()
    x_ref[...] = _tile_rows_load(xbuf.at[blk % X_RING], 0, ROW_BLOCK).astype(BF16)
    rows = blk * ROW_BLOCK + lax.broadcasted_iota(jnp.int32, (ROW_BLOCK, 1), 0)

    def item(i):
        expert = exp_ref[i]
        slot = slot_ref[i]

        @pl.when(i == 0)
        def _():
            for copy in _weight_copies(expert, slot, w_hbm_refs, w_buf_refs, sems):
                copy.start()

        @pl.when((i == 0) | (exp_ref[jnp.maximum(i - 1, 0)] != expert))
        def _():
            following = next_ref[i]

            @pl.when(following < N_EXPERTS)
            def _():
                for copy in _weight_copies(following, 1 - slot, w_hbm_refs, w_buf_refs, sems):
                    copy.start(priority=1)

            for copy in _weight_copies(expert, slot, w_hbm_refs, w_buf_refs, sems):
                copy.wait()
            wg_bf_ref[...] = wg_buf[slot].astype(BF16)
            wu_bf_ref[...] = wu_buf[slot].astype(BF16)
            wd_bf_ref[...] = wd_buf[slot].astype(BF16)

        x = x_ref[...]
        gate = _dot(x, wg_bf_ref[...])
        up = _dot(x, wu_bf_ref[...])
        hidden = (gate * _sigmoid(gate) * up).astype(BF16)
        y = _dot(hidden, wd_bf_ref[...])
        return y, (rows >= lo_ref[i]) & (rows < hi_ref[i])

    def middle_item(i, carry):
        y, keep = item(i)
        acc_ref[...] = jnp.where(keep, y, acc_ref[...])
        return carry

    first = first_ref[blk]
    last = first_ref[blk + 1] - 1

    @pl.when(first == last)
    def _():
        y, keep = item(first)
        _tile_rows_store(ys_ref, 0, ROW_BLOCK, jnp.where(keep, y, 0.0))

    @pl.when(first < last)
    def _():
        y, keep = item(first)
        acc_ref[...] = jnp.where(keep, y, 0.0)
        lax.fori_loop(first + 1, last, middle_item, 0)
        y, keep = item(last)
        _tile_rows_store(ys_ref, 0, ROW_BLOCK, jnp.where(keep, y, acc_ref[...]))


def _experts(block_first, item_exp, item_lo, item_hi, item_next, item_slot, xs, w_gate, w_up, w_down):
    n_rows = xs.shape[0] // SUBLANES
    block = (ROW_BLOCK * SUBLANES, LANES)
    grid_spec = pltpu.PrefetchScalarGridSpec(
        num_scalar_prefetch=6,
        grid=(n_rows // ROW_BLOCK,),
        in_specs=[
            pl.BlockSpec(memory_space=pl.ANY),
            pl.BlockSpec(memory_space=pl.ANY),
            pl.BlockSpec(memory_space=pl.ANY),
            pl.BlockSpec(memory_space=pl.ANY),
        ],
        out_specs=pl.BlockSpec(block, lambda b, *_: (b, 0)),
        scratch_shapes=[
            pltpu.VMEM((X_RING,) + block, F32),
            pltpu.VMEM((ROW_BLOCK, D_MODEL), BF16),
            pltpu.VMEM((ROW_BLOCK, D_MODEL), F32),
            pltpu.VMEM((2, D_MODEL, D_EXPERT), F32),
            pltpu.VMEM((2, D_MODEL, D_EXPERT), F32),
            pltpu.VMEM((2, D_EXPERT, D_MODEL), F32),
            pltpu.VMEM((D_MODEL, D_EXPERT), BF16),
            pltpu.VMEM((D_MODEL, D_EXPERT), BF16),
            pltpu.VMEM((D_EXPERT, D_MODEL), BF16),
            pltpu.SemaphoreType.DMA((2, 3)),
            pltpu.SemaphoreType.DMA((X_RING,)),
        ],
    )
    assert n_rows // ROW_BLOCK >= X_RING
    return pl.pallas_call(
        _experts_kernel,
        grid_spec=grid_spec,
        out_shape=jax.ShapeDtypeStruct((n_rows * SUBLANES, LANES), F32),
        compiler_params=_params(("arbitrary",)),
        name="experts",
    )(block_first, item_exp, item_lo, item_hi, item_next, item_slot, xs, w_gate, w_up, w_down)


def _combine_kernel(slot0_ref, slot_next_ref, gw_ref, h1_ref, ys_ref, wsg_ref, wsu_ref, wsd_ref,
                    g2_ref, b2_ref, out_ref, buf_ref, sems):
    tm = h1_ref.shape[0]
    i = pl.program_id(0)
    cur = i % 2

    def gather(slot_ref, buf, sem):
        def issue(t, carry):
            for k in range(TOP_K):
                pltpu.make_async_copy(
                    _token_tile(ys_ref, slot_ref[k * tm + t]), _token_tile(buf, k * tm + t), sem).start(priority=k % 2)
            return carry

        lax.fori_loop(0, tm, issue, 0, unroll=ISSUE_UNROLL)

    @pl.when(i == 0)
    def _():
        gather(slot0_ref, buf_ref.at[0], sems.at[0])

    @pl.when(i + 1 < pl.num_programs(0))
    def _():
        gather(slot_next_ref, buf_ref.at[1 - cur], sems.at[1 - cur])

    h1 = h1_ref[...]
    hb = h1.astype(BF16)
    gate = _dot(hb, wsg_ref[...])
    hidden = (gate * _sigmoid(gate) * _dot(hb, wsu_ref[...])).astype(BF16)
    moe = _dot(hidden, wsd_ref[...])

    buf = buf_ref.at[cur]
    for k in range(TOP_K):
        pltpu.make_async_copy(
            ys_ref.at[pl.ds(0, tm * SUBLANES), :], buf.at[pl.ds(k * tm * SUBLANES, tm * SUBLANES), :],
            sems.at[cur]).wait()

    gw = gw_ref[...]
    for k in range(TOP_K):
        moe = moe + gw[:, k:k + 1] * _tile_rows_load(buf, k * tm, tm)
    out_ref[...] = _layer_norm(DN_ALPHA * h1 + moe, g2_ref[...], b2_ref[...])


def _combine(slots_tiled, gw, h1, ys, w_sg, w_su, w_sd, g2, b2):
    T = h1.shape[0]
    tm = TOKEN_TILE
    row = lambda i: (i, 0)
    n_steps = T // tm
    return pl.pallas_call(
        _combine_kernel,
        grid=(n_steps,),
        in_specs=[
            pl.BlockSpec((TOP_K * tm,), lambda i: (0,), memory_space=pltpu.SMEM),
            pl.BlockSpec((TOP_K * tm,), lambda i: (jnp.minimum(i + 1, n_steps - 1),), memory_space=pltpu.SMEM),
            pl.BlockSpec((tm, LANES), row),
            pl.BlockSpec((tm, D_MODEL), row),
            pl.BlockSpec(memory_space=pl.ANY),
            _const_spec((D_MODEL, D_EXPERT)),
            _const_spec((D_MODEL, D_EXPERT)),
            _const_spec((D_EXPERT, D_MODEL)),
            _const_spec((1, D_MODEL)),
            _const_spec((1, D_MODEL)),
        ],
        out_specs=pl.BlockSpec((tm, D_MODEL), row),
        out_shape=jax.ShapeDtypeStruct((T, D_MODEL), F32),
        scratch_shapes=[pltpu.VMEM((2, TOP_K * tm * SUBLANES, LANES), F32), pltpu.SemaphoreType.DMA((2,))],
        compiler_params=_params(("arbitrary",)),
        name="combine",
    )(slots_tiled, slots_tiled, gw, h1, ys, w_sg, w_su, w_sd, g2, b2)


def _expert_work_items(counts, n_rows):
    n_blocks = n_rows // ROW_BLOCK
    n_items = n_blocks + N_EXPERTS
    ends = jnp.cumsum(counts)
    starts = ends - counts
    first_blk = starts // ROW_BLOCK
    last_blk = jnp.maximum(ends - 1, 0) // ROW_BLOCK
    per_expert = jnp.where(counts > 0, last_blk - first_blk + 1, 0)
    item_end = jnp.cumsum(per_expert)
    item_start = item_end - per_expert
    total = item_end[-1]
    i = jnp.arange(n_items, dtype=jnp.int32)
    live = i < total
    j = jnp.minimum(i, total - 1)[:, None]
    owner = (item_start[None, :] <= j) & (j < item_end[None, :])

    def pick(per_expert_values):
        return jnp.sum(jnp.where(owner, per_expert_values[None, :], 0), axis=1)

    expert_ids = jnp.arange(N_EXPERTS, dtype=jnp.int32)
    nonempty = counts > 0
    later = (expert_ids[None, :] > expert_ids[:, None]) & nonempty[None, :]
    following = jnp.min(jnp.where(later, expert_ids[None, :], N_EXPERTS), axis=1)
    buffer_slot = (jnp.cumsum(nonempty) - 1) % 2
    exp = pick(expert_ids)
    blk = pick(first_blk - item_start) + j[:, 0]
    lo = jnp.maximum(pick(starts), blk * ROW_BLOCK)
    hi = jnp.minimum(pick(ends), (blk + 1) * ROW_BLOCK)
    lo = jnp.where(live, lo, 0)
    hi = jnp.where(live, hi, 0)
    block_ids = jnp.arange(n_blocks + 1, dtype=jnp.int32)
    block_first = jnp.sum(live[None, :] & (blk[None, :] < block_ids[:, None]), axis=1)
    as_i32 = lambda a: a.astype(jnp.int32)
    return (as_i32(starts), as_i32(block_first), as_i32(exp), as_i32(lo), as_i32(hi),
            as_i32(pick(following)), as_i32(pick(buffer_slot)))


def kernel(x, positions, w_in, gmlp_ln_g, gmlp_ln_b, w_spatial, b_spatial, w_a_proj, q_norm_g, w_uq, kv_norm_g, w_uk, w_uv, w_b_proj, w_out, ln1_g, ln1_b, w_router, router_bias, w_gate, w_up, w_down, w_sh_gate, w_sh_up, w_sh_down, ln2_g, ln2_b):
    B, S, D = x.shape
    T = B * S
    x2 = x.reshape(T, D)

    w = w_in[0]
    o_u, o_v, o_cq = 0, A_WIDTH, 2 * A_WIDTH
    o_ckv = o_cq + Q_LORA
    o_kr = o_ckv + KV_LORA
    o_ga = o_kr + QK_ROPE
    o_gb = o_ga + D_MODEL
    half_r = QK_ROPE // 2
    w_kr = w[:, o_kr:o_ga]
    w_kr_partner = jnp.concatenate([w_kr[:, half_r:], w_kr[:, :half_r]], axis=1)
    w_b = jnp.concatenate(
        [w[:, o_cq:o_kr], w_kr, w_kr, w_kr_partner, w_kr_partner, w[:, o_gb:]], axis=1).astype(BF16)
    uq = w_uq[0]
    uq_rope = uq[:, :, QK_NOPE:]
    uq_partner = jnp.concatenate([uq_rope[:, :, half_r:], uq_rope[:, :, :half_r]], axis=2)
    w_uqt = jnp.concatenate([uq, uq_partner], axis=2).reshape(Q_LORA, N_HEADS * QK_PAD).T.astype(BF16)
    w_ukm = w_uk[0].reshape(KV_LORA, N_HEADS * QK_NOPE).astype(BF16)
    w_uvt = w_uv[0].reshape(KV_LORA, N_HEADS * V_HEAD).T.astype(BF16)

    freq = ROPE_THETA ** (-jnp.arange(0, half_r, dtype=F32) * 2.0 / QK_ROPE)
    pos_f = positions.astype(F32).reshape(T)
    bias_full = jnp.repeat(b_spatial[0].T, A_GROUP_DIM, axis=1)

    ga = _gmlp_branch(
        x2, w[:, o_u:o_v].astype(BF16), w[:, o_v:o_cq].astype(BF16), w[:, o_ga:o_gb].astype(BF16),
        gmlp_ln_g[0].reshape(1, A_WIDTH), gmlp_ln_b[0].reshape(1, A_WIDTH),
        w_spatial[0].astype(BF16), bias_full, w_a_proj[0].astype(BF16))

    q_t, k, v_t, sgb = _mla_prep(
        x2, pos_f.reshape(T // TOKEN_TILE, 1, TOKEN_TILE), w_b,
        q_norm_g[0].reshape(1, Q_LORA), kv_norm_g[0].reshape(1, KV_LORA), w_uqt, w_ukm, w_uvt,
        freq.reshape(half_r, 1), B, S)

    o = _attention(q_t, k, v_t).reshape(T, N_HEADS * V_HEAD)

    h1, h1t, idx_t, rank_t, gw, cnt = _mix_route(
        x2, ga, sgb, o, w_b_proj[0].astype(BF16), w_out[0].astype(BF16),
        ln1_g[0].reshape(1, D), ln1_b[0].reshape(1, D), w_router[0].astype(BF16),
        router_bias[0].reshape(N_EXPERTS, 1))

    counts = cnt[:, 0].astype(jnp.int32)
    starts, block_first, item_exp, item_lo, item_hi, item_next, item_slot = _expert_work_items(counts, T * TOP_K)

    slots = _slot_index(starts, idx_t, rank_t)
    slots_tiled = slots.reshape(TOP_K, T // TOKEN_TILE, TOKEN_TILE).transpose(1, 0, 2).reshape(T * TOP_K)

    xs = _dispatch(slots_tiled, h1t)
    ys = _experts(block_first, item_exp, item_lo, item_hi, item_next, item_slot, xs, w_gate[0], w_up[0], w_down[0])
    out = _combine(
        slots_tiled, gw, h1, ys, w_sh_gate[0].astype(BF16), w_sh_up[0].astype(BF16),
        w_sh_down[0].astype(BF16), ln2_g[0].reshape(1, D), ln2_b[0].reshape(1, D))
    return out.reshape(B, S, D)
```

```python
import functools
import math

import jax
import jax.numpy as jnp
from jax import lax
from jax.experimental import pallas as pl
from jax.experimental.pallas import tpu as pltpu

D_MODEL = 1024
CHUNK = 128
A_GROUPS = 8
A_GROUP_DIM = 128
A_WIDTH = A_GROUPS * A_GROUP_DIM
N_HEADS = 8
QK_NOPE = 128
QK_ROPE = 64
QK_DIM = QK_NOPE + QK_ROPE
V_HEAD = 128
V_ROWS = 144
Q_LORA = 384
KV_LORA = 256
ROPE_THETA = 10000.0
N_EXPERTS = 256
TOP_K = 8
N_GROUPS = 8
TOPK_GROUPS = 4
GROUP_SIZE = N_EXPERTS // N_GROUPS
D_EXPERT = 256
ROUTED_SCALE = 2.5
DN_ALPHA = 2.0 ** 0.25
LN_EPS = 1e-5
RMS_EPS = 1e-6

QK_PAD = 256
LANES = 128
TOKEN_TILE = 512
Q_TILE = 1024
KV_TILE = 256
KV_GROUP = 8
SLOT_LANES = 2048
ISSUE_UNROLL = 4
ROW_BLOCK = 512
X_RING = 3
SUBLANES = 8
VMEM_LIMIT = 56 * 1024 * 1024

F32 = jnp.float32
BF16 = jnp.bfloat16
NEG_INF = float("-inf")


def _dot(a, b):
    return jnp.dot(a, b, preferred_element_type=F32)


def _sigmoid(x):
    return 1.0 / (1.0 + jnp.exp(-x))


def _gelu_tanh(x):
    return 0.5 * x * (1.0 + jnp.tanh(math.sqrt(2.0 / math.pi) * (x + 0.044715 * (x * x * x))))


def _layer_norm(x, g, b):
    mu = jnp.mean(x, axis=-1, keepdims=True)
    d = x - mu
    var = jnp.mean(d * d, axis=-1, keepdims=True)
    return d * lax.rsqrt(var + LN_EPS) * g + b


def _rms_norm(x, g):
    return x * lax.rsqrt(jnp.mean(x * x, axis=-1, keepdims=True) + RMS_EPS) * g


def _tile_rows_load(ref, first, n):
    return jnp.concatenate(
        [ref[pl.ds(first * SUBLANES + j, n, stride=SUBLANES), :] for j in range(SUBLANES)], axis=1)


def _tile_rows_store(ref, first, n, value):
    for j in range(SUBLANES):
        ref[pl.ds(first * SUBLANES + j, n, stride=SUBLANES), :] = value[:, j * LANES:(j + 1) * LANES]


def _const_spec(shape):
    zeros = (0,) * len(shape)
    return pl.BlockSpec(shape, lambda *_: zeros)


def _params(semantics):
    return pltpu.CompilerParams(dimension_semantics=semantics, vmem_limit_bytes=VMEM_LIMIT)


def _gmlp_kernel(x_ref, wu_ref, wv_ref, wg_ref, lng_ref, lnb_ref, ws_ref, bias_ref, wa_ref, o_ref, mix_ref):
    tm = x_ref.shape[0]
    xb = x_ref[...].astype(BF16)
    v = _gelu_tanh(_dot(xb, wv_ref[...]))
    for h in range(A_GROUPS):
        cols = slice(h * A_GROUP_DIM, (h + 1) * A_GROUP_DIM)
        vn = _layer_norm(v[:, cols], lng_ref[:, cols], lnb_ref[:, cols]).astype(BF16)
        n_chunks = tm // CHUNK
        side_by_side = jnp.concatenate([vn[c * CHUNK:(c + 1) * CHUNK, :] for c in range(n_chunks)], axis=1)
        mixed = _dot(ws_ref[h], side_by_side)
        for c in range(n_chunks):
            mix_ref[c * CHUNK:(c + 1) * CHUNK, cols] = (
                mixed[:, c * A_GROUP_DIM:(c + 1) * A_GROUP_DIM] + bias_ref[:, cols])
    u = _gelu_tanh(_dot(xb, wu_ref[...]))
    ya = _dot((u * mix_ref[...]).astype(BF16), wa_ref[...])
    gate = _sigmoid(_dot(xb, wg_ref[...]))
    o_ref[...] = (gate * ya).astype(BF16)


def _gmlp_branch(x2, w_u, w_v, w_ga, ln_g, ln_b, w_s, bias_full, w_a):
    T = x2.shape[0]
    tm = TOKEN_TILE
    return pl.pallas_call(
        _gmlp_kernel,
        grid=(T // tm,),
        in_specs=[
            pl.BlockSpec((tm, D_MODEL), lambda i: (i, 0)),
            _const_spec((D_MODEL, A_WIDTH)),
            _const_spec((D_MODEL, A_WIDTH)),
            _const_spec((D_MODEL, D_MODEL)),
            _const_spec((1, A_WIDTH)),
            _const_spec((1, A_WIDTH)),
            _const_spec((A_GROUPS, CHUNK, CHUNK)),
            _const_spec((CHUNK, A_WIDTH)),
            _const_spec((A_WIDTH, D_MODEL)),
        ],
        out_specs=pl.BlockSpec((tm, D_MODEL), lambda i: (i, 0)),
        out_shape=jax.ShapeDtypeStruct((T, D_MODEL), BF16),
        scratch_shapes=[pltpu.VMEM((tm, A_WIDTH), F32)],
        compiler_params=_params(("arbitrary",)),
        name="gmlp_branch",
    )(x2, w_u, w_v, w_ga, ln_g, ln_b, w_s, bias_full, w_a)


def _mla_prep_kernel(x_ref, pos_ref, wb_ref, qg_ref, kvg_ref, wuqt_ref, wuk_ref, wuvt_ref, freq_ref,
                     qt_ref, k_ref, vt_ref, sgb_ref):
    xb = x_ref[...].astype(BF16)
    pb = _dot(xb, wb_ref[...])
    c_q = pb[:, :Q_LORA]
    c_kv = pb[:, Q_LORA:Q_LORA + KV_LORA]
    o = Q_LORA + KV_LORA
    kr = pb[:, o:o + LANES]
    kr_partner = pb[:, o + LANES:o + 2 * LANES]
    gb = pb[:, o + 2 * LANES:]
    sgb_ref[...] = _sigmoid(gb).astype(BF16)

    cqn = _rms_norm(c_q, qg_ref[...])
    ckvn = _rms_norm(c_kv, kvg_ref[...])
    cqn_t = cqn.T.astype(BF16)
    ckvn_t = ckvn.T.astype(BF16)

    scale = math.log2(math.e) / math.sqrt(QK_DIM)
    ang = freq_ref[...] * pos_ref[0]
    cos_t = jnp.cos(ang)
    sin_t = jnp.sin(ang)
    table_t = jnp.concatenate([cos_t, cos_t, -sin_t, sin_t], axis=0) * scale
    q_t = _dot(wuqt_ref[...], cqn_t)
    for h in range(N_HEADS):
        blk = q_t[h * QK_PAD:(h + 1) * QK_PAD]
        qt_ref[0, h] = jnp.concatenate(
            [blk[:QK_NOPE] * scale, blk[QK_NOPE:] * table_t], axis=0).astype(BF16)

    v_t = _dot(wuvt_ref[...], ckvn_t)
    ones_rows = jnp.ones((V_ROWS - V_HEAD, v_t.shape[1]), F32)
    for h in range(N_HEADS):
        vt_ref[0, h] = jnp.concatenate([v_t[h * V_HEAD:(h + 1) * V_HEAD], ones_rows], axis=0).astype(BF16)

    cos_k = jnp.concatenate([cos_t, cos_t, cos_t, cos_t], axis=0).T
    sin_k = jnp.concatenate([-sin_t, sin_t, -sin_t, sin_t], axis=0).T
    k_rope = (kr * cos_k + kr_partner * sin_k).astype(BF16)
    k_nope = _dot(ckvn.astype(BF16), wuk_ref[...])
    for h in range(N_HEADS):
        k_ref[0, h] = jnp.concatenate(
            [k_nope[:, h * QK_NOPE:(h + 1) * QK_NOPE].astype(BF16), k_rope], axis=1)


def _mla_prep(x2, pos_row, w_b, q_g, kv_g, w_uqt, w_uk, w_uvt, freq_col, B, S):
    T = x2.shape[0]
    tm = TOKEN_TILE
    spb = S // tm
    n_b = w_b.shape[1]
    head_map = lambda i: (i // spb, 0, 0, i % spb)
    return pl.pallas_call(
        _mla_prep_kernel,
        grid=(T // tm,),
        in_specs=[
            pl.BlockSpec((tm, D_MODEL), lambda i: (i, 0)),
            pl.BlockSpec((1, 1, tm), lambda i: (i, 0, 0)),
            _const_spec((D_MODEL, n_b)),
            _const_spec((1, Q_LORA)),
            _const_spec((1, KV_LORA)),
            _const_spec((N_HEADS * QK_PAD, Q_LORA)),
            _const_spec((KV_LORA, N_HEADS * QK_NOPE)),
            _const_spec((N_HEADS * V_HEAD, KV_LORA)),
            _const_spec((QK_ROPE // 2, 1)),
        ],
        out_specs=[
            pl.BlockSpec((1, N_HEADS, QK_PAD, tm), head_map),
            pl.BlockSpec((1, N_HEADS, tm, QK_PAD), lambda i: (i // spb, 0, i % spb, 0)),
            pl.BlockSpec((1, N_HEADS, V_ROWS, tm), head_map),
            pl.BlockSpec((tm, D_MODEL), lambda i: (i, 0)),
        ],
        out_shape=[
            jax.ShapeDtypeStruct((B, N_HEADS, QK_PAD, S), BF16),
            jax.ShapeDtypeStruct((B, N_HEADS, S, QK_PAD), BF16),
            jax.ShapeDtypeStruct((B, N_HEADS, V_ROWS, S), BF16),
            jax.ShapeDtypeStruct((T, D_MODEL), BF16),
        ],
        compiler_params=_params(("arbitrary",)),
        name="mla_prep",
    )(x2, pos_row, w_b, q_g, kv_g, w_uqt, w_uk, w_uvt, freq_col)


def _attention_kernel(qt_ref, k_ref, vt_ref, o_ref, s0_ref, s1_ref, m_ref, acc_ref):
    S = k_ref.shape[2]
    tq = s0_ref.shape[1]
    n_chunks = S // KV_TILE
    n_q = S // tq
    bufs = (s0_ref, s1_ref)

    def scores(qi, c, dst_ref):
        q_t = qt_ref[0, 0, :, pl.ds(pl.multiple_of(qi * tq, tq), tq)]
        start = pl.multiple_of(c * KV_TILE, KV_TILE)
        dst_ref[...] = _dot(k_ref[0, 0, pl.ds(start, KV_TILE), :], q_t)

    def accumulate(c, src_ref):
        start = pl.multiple_of(c * KV_TILE, KV_TILE)
        s_t = src_ref[...]
        m = m_ref[...]
        m_new = jnp.maximum(m, jnp.max(s_t, axis=0, keepdims=True))
        p_t = jnp.exp2(s_t - m_new)
        alpha = jnp.exp2(m - m_new)
        m_ref[...] = m_new
        acc_ref[...] = alpha * acc_ref[...] + _dot(vt_ref[0, 0, :, pl.ds(start, KV_TILE)], p_t.astype(BF16))

    def group(qi, c0, last_of_tile):
        for i in range(KV_GROUP):
            if i + 1 < KV_GROUP or not last_of_tile:
                scores(qi, c0 + i + 1, bufs[(i + 1) % 2])
            else:
                scores(jnp.minimum(qi + 1, n_q - 1), 0, bufs[(i + 1) % 2])
            accumulate(c0 + i, bufs[i % 2])

    scores(0, 0, s0_ref)

    def q_tile(qi, carry):
        m_ref[...] = jnp.full(m_ref.shape, NEG_INF, F32)
        acc_ref[...] = jnp.zeros(acc_ref.shape, F32)

        def body(g, c):
            group(qi, g * KV_GROUP, False)
            return c

        lax.fori_loop(0, n_chunks // KV_GROUP - 1, body, 0)
        group(qi, n_chunks - KV_GROUP, True)
        o_ref[0, pl.ds(pl.multiple_of(qi * tq, tq), tq), :] = (
            acc_ref[:V_HEAD] / acc_ref[V_HEAD:V_HEAD + 1]).T.astype(BF16)
        return carry

    lax.fori_loop(0, n_q, q_tile, 0)


def _attention(q_t, k, v_t):
    B, H, _, S = q_t.shape
    tq = min(Q_TILE, S)
    assert S % tq == 0 and S % (KV_TILE * KV_GROUP) == 0 and KV_GROUP % 2 == 0
    return pl.pallas_call(
        _attention_kernel,
        grid=(B, H),
        in_specs=[
            pl.BlockSpec((1, 1, QK_PAD, S), lambda b, h: (b, h, 0, 0)),
            pl.BlockSpec((1, 1, S, QK_PAD), lambda b, h: (b, h, 0, 0)),
            pl.BlockSpec((1, 1, V_ROWS, S), lambda b, h: (b, h, 0, 0)),
        ],
        out_specs=pl.BlockSpec((1, S, V_HEAD), lambda b, h: (b, 0, h)),
        out_shape=jax.ShapeDtypeStruct((B, S, H * V_HEAD), BF16),
        scratch_shapes=[
            pltpu.VMEM((KV_TILE, tq), F32),
            pltpu.VMEM((KV_TILE, tq), F32),
            pltpu.VMEM((1, tq), F32),
            pltpu.VMEM((V_ROWS, tq), F32),
        ],
        compiler_params=_params(("arbitrary", "arbitrary")),
        name="attention",
    )(q_t, k, v_t)


def _mix_route_kernel(x_ref, ga_ref, sgb_ref, o_ref, wbp_ref, wout_ref, g1_ref, b1_ref, wr_ref, rb_ref,
                      h1_ref, h1t_ref, idx_ref, rank_ref, gw_ref, cnt_ref, base_ref):
    tm = x_ref.shape[0]

    @pl.when(pl.program_id(0) == 0)
    def _():
        base_ref[...] = jnp.zeros_like(base_ref)

    yb = _dot(o_ref[...], wbp_ref[...])
    merged = ga_ref[...].astype(F32) + sgb_ref[...].astype(F32) * yb
    mixed = _dot(merged.astype(BF16), wout_ref[...])
    h1 = _layer_norm(DN_ALPHA * x_ref[...] + mixed, g1_ref[...], b1_ref[...])
    h1_ref[...] = h1
    _tile_rows_store(h1t_ref, 0, tm, h1)

    scores_t = _sigmoid(_dot(h1.astype(BF16), wr_ref[...])).T
    biased = scores_t + rb_ref[...]

    riota = lax.broadcasted_iota(jnp.int32, (GROUP_SIZE, tm), 0)
    blocks, gscore = [], []
    for g in range(N_GROUPS):
        blk = biased[g * GROUP_SIZE:(g + 1) * GROUP_SIZE]
        m1 = jnp.max(blk, axis=0, keepdims=True)
        i1 = jnp.min(jnp.where(blk == m1, riota, GROUP_SIZE), axis=0, keepdims=True)
        m2 = jnp.max(jnp.where(riota == i1, NEG_INF, blk), axis=0, keepdims=True)
        blocks.append(blk)
        gscore.append(m1 + m2)

    selected = [jnp.zeros((1, tm), F32) for _ in range(N_GROUPS)]
    for _ in range(TOPK_GROUPS):
        best = functools.reduce(jnp.maximum, gscore)
        first = functools.reduce(
            jnp.minimum, [jnp.where(gscore[g] == best, g, N_GROUPS) for g in range(N_GROUPS)])
        for g in range(N_GROUPS):
            hit = first == g
            selected[g] = jnp.where(hit, 1.0, selected[g])
            gscore[g] = jnp.where(hit, NEG_INF, gscore[g])

    masked = jnp.concatenate(
        [jnp.where(jnp.broadcast_to(selected[g], blocks[g].shape) > 0.5, blocks[g], NEG_INF)
         for g in range(N_GROUPS)], axis=0)

    eiota = lax.broadcasted_iota(jnp.int32, (N_EXPERTS, tm), 0)
    picks, weights = [], []
    chosen = jnp.zeros((N_EXPERTS, tm), F32)
    for _ in range(TOP_K):
        best = jnp.max(masked, axis=0, keepdims=True)
        e = jnp.min(jnp.where(masked == best, eiota, N_EXPERTS), axis=0, keepdims=True)
        hit = eiota == e
        masked = jnp.where(hit, NEG_INF, masked)
        chosen = jnp.where(hit, 1.0, chosen)
        picks.append(e)
        weights.append(jnp.sum(jnp.where(hit, scores_t, 0.0), axis=0, keepdims=True))
    wsum = functools.reduce(lambda a, b: a + b, weights)

    before = (lax.broadcasted_iota(jnp.int32, (tm, tm), 0)
              < lax.broadcasted_iota(jnp.int32, (tm, tm), 1)).astype(BF16)
    rank_full = _dot(chosen.astype(BF16), before) + base_ref[:, :1]
    new_base = base_ref[:, :1] + jnp.sum(chosen, axis=1, keepdims=True)
    base_ref[...] = jnp.broadcast_to(new_base, base_ref.shape)
    cnt_ref[...] = jnp.broadcast_to(new_base, cnt_ref.shape)

    kiota = lax.broadcasted_iota(jnp.int32, (TOP_K, tm), 0)
    wiota = lax.broadcasted_iota(jnp.int32, (LANES, tm), 0)
    idx_out = jnp.zeros((TOP_K, tm), jnp.int32)
    rank_out = jnp.zeros((TOP_K, tm), jnp.int32)
    gw_t = jnp.zeros((LANES, tm), F32)
    for k in range(TOP_K):
        r = jnp.sum(jnp.where(eiota == picks[k], rank_full, 0.0), axis=0, keepdims=True)
        idx_out = jnp.where(kiota == k, picks[k], idx_out)
        rank_out = jnp.where(kiota == k, r.astype(jnp.int32), rank_out)
        gw_t = jnp.where(wiota == k, weights[k] / wsum * ROUTED_SCALE, gw_t)
    idx_ref[...] = idx_out
    rank_ref[...] = rank_out
    gw_ref[...] = gw_t.T


def _mix_route(x2, ga, sgb, o, w_bp, w_out, g1, b1, w_r, r_bias):
    T = x2.shape[0]
    tm = TOKEN_TILE
    row = lambda i: (i, 0)
    col = lambda i: (0, i)
    return pl.pallas_call(
        _mix_route_kernel,
        grid=(T // tm,),
        in_specs=[
            pl.BlockSpec((tm, D_MODEL), row),
            pl.BlockSpec((tm, D_MODEL), row),
            pl.BlockSpec((tm, D_MODEL), row),
            pl.BlockSpec((tm, D_MODEL), row),
            _const_spec((D_MODEL, D_MODEL)),
            _const_spec((D_MODEL, D_MODEL)),
            _const_spec((1, D_MODEL)),
            _const_spec((1, D_MODEL)),
            _const_spec((D_MODEL, N_EXPERTS)),
            _const_spec((N_EXPERTS, 1)),
        ],
        out_specs=[
            pl.BlockSpec((tm, D_MODEL), row),
            pl.BlockSpec((tm * SUBLANES, LANES), row),
            pl.BlockSpec((TOP_K, tm), col),
            pl.BlockSpec((TOP_K, tm), col),
            pl.BlockSpec((tm, LANES), row),
            _const_spec((N_EXPERTS, LANES)),
        ],
        out_shape=[
            jax.ShapeDtypeStruct((T, D_MODEL), F32),
            jax.ShapeDtypeStruct((T * SUBLANES, LANES), F32),
            jax.ShapeDtypeStruct((TOP_K, T), jnp.int32),
            jax.ShapeDtypeStruct((TOP_K, T), jnp.int32),
            jax.ShapeDtypeStruct((T, LANES), F32),
            jax.ShapeDtypeStruct((N_EXPERTS, LANES), F32),
        ],
        scratch_shapes=[pltpu.VMEM((N_EXPERTS, LANES), F32)],
        compiler_params=_params(("arbitrary",)),
        name="mix_route",
    )(x2, ga, sgb, o, w_bp, w_out, g1, b1, w_r, r_bias)


def _slot_kernel(starts_ref, idx_ref, rank_ref, slot_ref):
    idx = idx_ref[...]

    def lookup(e, acc):
        return jnp.where(idx == e, starts_ref[e], acc)

    slot_ref[...] = lax.fori_loop(0, N_EXPERTS, lookup, jnp.zeros_like(idx), unroll=8) + rank_ref[...]


def _slot_index(starts, idx_t, rank_t):
    T = idx_t.shape[1]
    tl = min(SLOT_LANES, T)
    col = lambda i: (0, i)
    return pl.pallas_call(
        _slot_kernel,
        grid=(T // tl,),
        in_specs=[
            pl.BlockSpec(memory_space=pltpu.SMEM),
            pl.BlockSpec((TOP_K, tl), col),
            pl.BlockSpec((TOP_K, tl), col),
        ],
        out_specs=pl.BlockSpec((TOP_K, tl), col),
        out_shape=jax.ShapeDtypeStruct((TOP_K, T), jnp.int32),
        compiler_params=_params(("arbitrary",)),
        name="slot_index",
    )(starts, idx_t, rank_t)


def _token_tile(ref, token):
    return ref.at[pl.ds(pl.multiple_of(token * SUBLANES, SUBLANES), SUBLANES), :]


def _dispatch_kernel(slot_ref, h1t_ref, xs_ref, sem):
    tm = h1t_ref.shape[0] // SUBLANES

    def issue(t, carry):
        src = _token_tile(h1t_ref, t)
        for k in range(TOP_K):
            pltpu.make_async_copy(src, _token_tile(xs_ref, slot_ref[k * tm + t]), sem).start(priority=k % 2)
        return carry

    lax.fori_loop(0, tm, issue, 0, unroll=ISSUE_UNROLL)
    for _ in range(TOP_K):
        pltpu.make_async_copy(h1t_ref, xs_ref.at[pl.ds(0, tm * SUBLANES), :], sem).wait()


def _dispatch(slots_tiled, h1t):
    T = h1t.shape[0] // SUBLANES
    tm = TOKEN_TILE
    return pl.pallas_call(
        _dispatch_kernel,
        grid=(T // tm,),
        in_specs=[
            pl.BlockSpec((TOP_K * tm,), lambda i: (i,), memory_space=pltpu.SMEM),
            pl.BlockSpec((tm * SUBLANES, LANES), lambda i: (i, 0)),
        ],
        out_specs=pl.BlockSpec(memory_space=pl.ANY),
        out_shape=jax.ShapeDtypeStruct((T * TOP_K * SUBLANES, LANES), F32),
        scratch_shapes=[pltpu.SemaphoreType.DMA(())],
        compiler_params=_params(("arbitrary",)),
        name="dispatch",
    )(slots_tiled, h1t)


def _weight_copies(expert, slot, w_hbm_refs, w_buf_refs, sems):
    return [pltpu.make_async_copy(w_hbm.at[expert], w_buf.at[slot], sems.at[slot, n])
            for n, (w_hbm, w_buf) in enumerate(zip(w_hbm_refs, w_buf_refs))]


def _experts_kernel(first_ref, exp_ref, lo_ref, hi_ref, next_ref, slot_ref,
                    xs_hbm, wg_hbm, wu_hbm, wd_hbm, ys_ref,
                    xbuf, x_ref, acc_ref, wg_buf, wu_buf, wd_buf, wg_bf_ref, wu_bf_ref, wd_bf_ref, sems, xsems):
    blk = pl.program_id(0)
    n_blocks = pl.num_programs(0)
    w_hbm_refs = (wg_hbm, wu_hbm, wd_hbm)
    w_buf_refs = (wg_buf, wu_buf, wd_buf)
    block_tiles = ROW_BLOCK * SUBLANES

    def block_copy(b):
        slot = b % X_RING
        start = pl.multiple_of(b * block_tiles, block_tiles)
        return pltpu.make_async_copy(xs_hbm.at[pl.ds(start, block_tiles), :], xbuf.at[slot], xsems.at[slot])

    @pl.when(blk == 0)
    def _():
        for b in range(X_RING - 1):
            block_copy(b).start()

    @pl.when(blk + X_RING - 1 < n_blocks)
    def _():
        block_copy(blk + X_RING - 1).start()

    block_copy(blk).wait()
    x_ref[...] = _tile_rows_load(xbuf.at[blk % X_RING], 0, ROW_BLOCK).astype(BF16)
    rows = blk * ROW_BLOCK + lax.broadcasted_iota(jnp.int32, (ROW_BLOCK, 1), 0)

    def item(i):
        expert = exp_ref[i]
        slot = slot_ref[i]

        @pl.when(i == 0)
        def _():
            for copy in _weight_copies(expert, slot, w_hbm_refs, w_buf_refs, sems):
                copy.start()

        @pl.when((i == 0) | (exp_ref[jnp.maximum(i - 1, 0)] != expert))
        def _():
            following = next_ref[i]

            @pl.when(following < N_EXPERTS)
            def _():
                for copy in _weight_copies(following, 1 - slot, w_hbm_refs, w_buf_refs, sems):
                    copy.start(priority=1)

            for copy in _weight_copies(expert, slot, w_hbm_refs, w_buf_refs, sems):
                copy.wait()
            wg_bf_ref[...] = wg_buf[slot].astype(BF16)
            wu_bf_ref[...] = wu_buf[slot].astype(BF16)
            wd_bf_ref[...] = wd_buf[slot].astype(BF16)

        x = x_ref[...]
        gate = _dot(x, wg_bf_ref[...])
        up = _dot(x, wu_bf_ref[...])
        hidden = (gate * _sigmoid(gate) * up).astype(BF16)
        y = _dot(hidden, wd_bf_ref[...])
        return y, (rows >= lo_ref[i]) & (rows < hi_ref[i])

    def middle_item(i, carry):
        y, keep = item(i)
        acc_ref[...] = jnp.where(keep, y, acc_ref[...])
        return carry

    first = first_ref[blk]
    last = first_ref[blk + 1] - 1

    @pl.when(first == last)
    def _():
        y, keep = item(first)
        _tile_rows_store(ys_ref, 0, ROW_BLOCK, jnp.where(keep, y, 0.0))

    @pl.when(first < last)
    def _():
        y, keep = item(first)
        acc_ref[...] = jnp.where(keep, y, 0.0)
        lax.fori_loop(first + 1, last, middle_item, 0)
        y, keep = item(last)
        _tile_rows_store(ys_ref, 0, ROW_BLOCK, jnp.where(keep, y, acc_ref[...]))


def _experts(block_first, item_exp, item_lo, item_hi, item_next, item_slot, xs, w_gate, w_up, w_down):
    n_rows = xs.shape[0] // SUBLANES
    block = (ROW_BLOCK * SUBLANES, LANES)
    grid_spec = pltpu.PrefetchScalarGridSpec(
        num_scalar_prefetch=6,
        grid=(n_rows // ROW_BLOCK,),
        in_specs=[
            pl.BlockSpec(memory_space=pl.ANY),
            pl.BlockSpec(memory_space=pl.ANY),
            pl.BlockSpec(memory_space=pl.ANY),
            pl.BlockSpec(memory_space=pl.ANY),
        ],
        out_specs=pl.BlockSpec(block, lambda b, *_: (b, 0)),
        scratch_shapes=[
            pltpu.VMEM((X_RING,) + block, F32),
            pltpu.VMEM((ROW_BLOCK, D_MODEL), BF16),
            pltpu.VMEM((ROW_BLOCK, D_MODEL), F32),
            pltpu.VMEM((2, D_MODEL, D_EXPERT), F32),
            pltpu.VMEM((2, D_MODEL, D_EXPERT), F32),
            pltpu.VMEM((2, D_EXPERT, D_MODEL), F32),
            pltpu.VMEM((D_MODEL, D_EXPERT), BF16),
            pltpu.VMEM((D_MODEL, D_EXPERT), BF16),
            pltpu.VMEM((D_EXPERT, D_MODEL), BF16),
            pltpu.SemaphoreType.DMA((2, 3)),
            pltpu.SemaphoreType.DMA((X_RING,)),
        ],
    )
    assert n_rows // ROW_BLOCK >= X_RING
    return pl.pallas_call(
        _experts_kernel,
        grid_spec=grid_spec,
        out_shape=jax.ShapeDtypeStruct((n_rows * SUBLANES, LANES), F32),
        compiler_params=_params(("arbitrary",)),
        name="experts",
    )(block_first, item_exp, item_lo, item_hi, item_next, item_slot, xs, w_gate, w_up, w_down)


def _combine_kernel(slot0_ref, slot_next_ref, gw_ref, h1_ref, ys_ref, wsg_ref, wsu_ref, wsd_ref,
                    g2_ref, b2_ref, out_ref, buf_ref, sems):
    tm = h1_ref.shape[0]
    i = pl.program_id(0)
    cur = i % 2

    def gather(slot_ref, buf, sem, first, last):
        def issue(t, carry):
            for k in range(TOP_K):
                pltpu.make_async_copy(
                    _token_tile(ys_ref, slot_ref[k * tm + t]), _token_tile(buf.at[k], t), sem).start(priority=k % 2)
            return carry

        lax.fori_loop(first, last, issue, 0, unroll=ISSUE_UNROLL)

    @pl.when(i == 0)
    def _():
        gather(slot0_ref, buf_ref.at[0], sems.at[0], 0, tm)

    has_next = i + 1 < pl.num_programs(0)

    @pl.when(has_next)
    def _():
        gather(slot_next_ref, buf_ref.at[1 - cur], sems.at[1 - cur], 0, tm // 2)

    h1 = h1_ref[...]
    hb = h1.astype(BF16)
    gate = _dot(hb, wsg_ref[...])
    hidden = (gate * _sigmoid(gate) * _dot(hb, wsu_ref[...])).astype(BF16)
    moe = _dot(hidden, wsd_ref[...])

    buf = buf_ref.at[cur]
    for k in range(TOP_K):
        pltpu.make_async_copy(ys_ref.at[pl.ds(0, tm * SUBLANES), :], buf.at[k], sems.at[cur]).wait()

    gw = gw_ref[...]
    for k in range(TOP_K):
        moe = moe + gw[:, k:k + 1] * _tile_rows_load(buf.at[k], 0, tm)
    out_ref[...] = _layer_norm(DN_ALPHA * h1 + moe, g2_ref[...], b2_ref[...])

    @pl.when(has_next)
    def _():
        gather(slot_next_ref, buf_ref.at[1 - cur], sems.at[1 - cur], tm // 2, tm)


def _combine(slots_tiled, gw, h1, ys, w_sg, w_su, w_sd, g2, b2):
    T = h1.shape[0]
    tm = TOKEN_TILE
    row = lambda i: (i, 0)
    n_steps = T // tm
    return pl.pallas_call(
        _combine_kernel,
        grid=(n_steps,),
        in_specs=[
            pl.BlockSpec((TOP_K * tm,), lambda i: (0,), memory_space=pltpu.SMEM),
            pl.BlockSpec((TOP_K * tm,), lambda i: (jnp.minimum(i + 1, n_steps - 1),), memory_space=pltpu.SMEM),
            pl.BlockSpec((tm, LANES), row),
            pl.BlockSpec((tm, D_MODEL), row),
            pl.BlockSpec(memory_space=pl.ANY),
            _const_spec((D_MODEL, D_EXPERT)),
            _const_spec((D_MODEL, D_EXPERT)),
            _const_spec((D_EXPERT, D_MODEL)),
            _const_spec((1, D_MODEL)),
            _const_spec((1, D_MODEL)),
        ],
        out_specs=pl.BlockSpec((tm, D_MODEL), row),
        out_shape=jax.ShapeDtypeStruct((T, D_MODEL), F32),
        scratch_shapes=[pltpu.VMEM((2, TOP_K, tm * SUBLANES, LANES), F32), pltpu.SemaphoreType.DMA((2,))],
        compiler_params=_params(("arbitrary",)),
        name="combine",
    )(slots_tiled, slots_tiled, gw, h1, ys, w_sg, w_su, w_sd, g2, b2)


def _expert_work_items(counts, n_rows):
    n_blocks = n_rows // ROW_BLOCK
    n_items = n_blocks + N_EXPERTS
    ends = jnp.cumsum(counts)
    starts = ends - counts
    first_blk = starts // ROW_BLOCK
    last_blk = jnp.maximum(ends - 1, 0) // ROW_BLOCK
    per_expert = jnp.where(counts > 0, last_blk - first_blk + 1, 0)
    item_end = jnp.cumsum(per_expert)
    item_start = item_end - per_expert
    total = item_end[-1]
    i = jnp.arange(n_items, dtype=jnp.int32)
    live = i < total
    j = jnp.minimum(i, total - 1)[:, None]
    owner = (item_start[None, :] <= j) & (j < item_end[None, :])

    def pick(per_expert_values):
        return jnp.sum(jnp.where(owner, per_expert_values[None, :], 0), axis=1)

    expert_ids = jnp.arange(N_EXPERTS, dtype=jnp.int32)
    nonempty = counts > 0
    later = (expert_ids[None, :] > expert_ids[:, None]) & nonempty[None, :]
    following = jnp.min(jnp.where(later, expert_ids[None, :], N_EXPERTS), axis=1)
    buffer_slot = (jnp.cumsum(nonempty) - 1) % 2
    exp = pick(expert_ids)
    blk = pick(first_blk - item_start) + j[:, 0]
    lo = jnp.maximum(pick(starts), blk * ROW_BLOCK)
    hi = jnp.minimum(pick(ends), (blk + 1) * ROW_BLOCK)
    lo = jnp.where(live, lo, 0)
    hi = jnp.where(live, hi, 0)
    block_ids = jnp.arange(n_blocks + 1, dtype=jnp.int32)
    block_first = jnp.sum(live[None, :] & (blk[None, :] < block_ids[:, None]), axis=1)
    as_i32 = lambda a: a.astype(jnp.int32)
    return (as_i32(starts), as_i32(block_first), as_i32(exp), as_i32(lo), as_i32(hi),
            as_i32(pick(following)), as_i32(pick(buffer_slot)))


def kernel(x, positions, w_in, gmlp_ln_g, gmlp_ln_b, w_spatial, b_spatial, w_a_proj, q_norm_g, w_uq, kv_norm_g, w_uk, w_uv, w_b_proj, w_out, ln1_g, ln1_b, w_router, router_bias, w_gate, w_up, w_down, w_sh_gate, w_sh_up, w_sh_down, ln2_g, ln2_b):
    B, S, D = x.shape
    T = B * S
    x2 = x.reshape(T, D)

    w = w_in[0]
    o_u, o_v, o_cq = 0, A_WIDTH, 2 * A_WIDTH
    o_ckv = o_cq + Q_LORA
    o_kr = o_ckv + KV_LORA
    o_ga = o_kr + QK_ROPE
    o_gb = o_ga + D_MODEL
    half_r = QK_ROPE // 2
    w_kr = w[:, o_kr:o_ga]
    w_kr_partner = jnp.concatenate([w_kr[:, half_r:], w_kr[:, :half_r]], axis=1)
    w_b = jnp.concatenate(
        [w[:, o_cq:o_kr], w_kr, w_kr, w_kr_partner, w_kr_partner, w[:, o_gb:]], axis=1).astype(BF16)
    uq = w_uq[0]
    uq_rope = uq[:, :, QK_NOPE:]
    uq_partner = jnp.concatenate([uq_rope[:, :, half_r:], uq_rope[:, :, :half_r]], axis=2)
    w_uqt = jnp.concatenate([uq, uq_partner], axis=2).reshape(Q_LORA, N_HEADS * QK_PAD).T.astype(BF16)
    w_ukm = w_uk[0].reshape(KV_LORA, N_HEADS * QK_NOPE).astype(BF16)
    w_uvt = w_uv[0].reshape(KV_LORA, N_HEADS * V_HEAD).T.astype(BF16)

    freq = ROPE_THETA ** (-jnp.arange(0, half_r, dtype=F32) * 2.0 / QK_ROPE)
    pos_f = positions.astype(F32).reshape(T)
    bias_full = jnp.repeat(b_spatial[0].T, A_GROUP_DIM, axis=1)

    ga = _gmlp_branch(
        x2, w[:, o_u:o_v].astype(BF16), w[:, o_v:o_cq].astype(BF16), w[:, o_ga:o_gb].astype(BF16),
        gmlp_ln_g[0].reshape(1, A_WIDTH), gmlp_ln_b[0].reshape(1, A_WIDTH),
        w_spatial[0].astype(BF16), bias_full, w_a_proj[0].astype(BF16))

    q_t, k, v_t, sgb = _mla_prep(
        x2, pos_f.reshape(T // TOKEN_TILE, 1, TOKEN_TILE), w_b,
        q_norm_g[0].reshape(1, Q_LORA), kv_norm_g[0].reshape(1, KV_LORA), w_uqt, w_ukm, w_uvt,
        freq.reshape(half_r, 1), B, S)

    o = _attention(q_t, k, v_t).reshape(T, N_HEADS * V_HEAD)

    h1, h1t, idx_t, rank_t, gw, cnt = _mix_route(
        x2, ga, sgb, o, w_b_proj[0].astype(BF16), w_out[0].astype(BF16),
        ln1_g[0].reshape(1, D), ln1_b[0].reshape(1, D), w_router[0].astype(BF16),
        router_bias[0].reshape(N_EXPERTS, 1))

    counts = cnt[:, 0].astype(jnp.int32)
    starts, block_first, item_exp, item_lo, item_hi, item_next, item_slot = _expert_work_items(counts, T * TOP_K)

    slots = _slot_index(starts, idx_t, rank_t)
    slots_tiled = slots.reshape(TOP_K, T // TOKEN_TILE, TOKEN_TILE).transpose(1, 0, 2).reshape(T * TOP_K)

    xs = _dispatch(slots_tiled, h1t)
    ys = _experts(block_first, item_exp, item_lo, item_hi, item_next, item_slot, xs, w_gate[0], w_up[0], w_down[0])
    out = _combine(
        slots_tiled, gw, h1, ys, w_sh_gate[0].astype(BF16), w_sh_up[0].astype(BF16),
        w_sh_down[0].astype(BF16), ln2_g[0].reshape(1, D), ln2_b[0].reshape(1, D))
    return out.reshape(B, S, D)
```
